```python
import math
import jax, jax.numpy as jnp
from jax import lax
import numpy as np


D_MODEL = 1024
BATCH = 8
SEQ = 4096
DEPTH = 1
DEC_BATCH = 32
DEC_SEQ = 4
PAST_LEN = 16384
PAGE_SIZE = 128

HEAD_DIM = 64
H_A = 8
H_B = 8
H_KV = 2
GROUP = H_B // H_KV
MOBA_BLOCK = 256
MOBA_TOPK = 3
CMP_LEN = 32
CMP_STRIDE = 16
SLC_BLOCK = 64
SLC_TOPK = 16
WINDOW = 512
CMP_HID = 2 * HEAD_DIM
D_FF = -(-8 * D_MODEL // (3 * 256)) * 256
QBLOCK = 16
FORCE_BONUS = 1e4
RMS_EPS = 1e-6
W_A = H_A * HEAD_DIM
W_B = H_B * HEAD_DIM
W_KV = H_KV * HEAD_DIM
N_IN = 3 * W_A + W_B + 6 * W_KV + 3 * H_B + 2 * D_MODEL
SPLITS = (W_A, 2 * W_A, 3 * W_A, 3 * W_A + W_B, 3 * W_A + W_B + 6 * W_KV,
          3 * W_A + W_B + 6 * W_KV + 3 * H_B)

kernel_name = 'moba_nsa_parallel_gated_decoder_step'


def rms_norm(x, g):
    xf = x.astype(jnp.float32)
    r = lax.rsqrt(jnp.mean(xf * xf, axis=-1, keepdims=True) + RMS_EPS)
    return (xf * r).astype(x.dtype) * g


def alibi_slopes(n):
    return jnp.exp2(-8.0 * jnp.arange(1, n + 1, dtype=jnp.float32) / n)


def masked_softmax(s, mask):
    s = jnp.where(mask, s.astype(jnp.float32), -jnp.inf)
    m = jnp.max(s, axis=-1, keepdims=True)
    m = jnp.where(jnp.isfinite(m), m, 0.0)
    e = jnp.where(mask, jnp.exp(s - m), 0.0)
    d = jnp.sum(e, axis=-1, keepdims=True)
    return e / jnp.where(d > 0, d, 1.0)


def compress(rows, pos, w1, w2):
    B, Lp, H, _ = rows.shape
    ch = rows.reshape(B, Lp // CMP_STRIDE, CMP_STRIDE, H, HEAD_DIM)
    blk = jnp.concatenate([ch[:, :-1], ch[:, 1:]], axis=2) + pos[None, None, :, None, :]
    hid = jax.nn.silu(jnp.einsum('bnlhd,ldf->bnhf', blk, w1))
    return jnp.einsum('bnhf,fd->bnhd', hid, w2)


def project(xn, w_in, b_in, g_qk_a, g_qk_b):
    B, T, _ = xn.shape
    z = jnp.einsum('btd,dn->btn', xn, w_in) + b_in
    qa, ka, va, qb, kvb, gb, gm = jnp.split(z, SPLITS, axis=-1)
    qa = rms_norm(qa.reshape(B, T, H_A, HEAD_DIM), g_qk_a[0])
    ka = rms_norm(ka.reshape(B, T, H_A, HEAD_DIM), g_qk_a[1])
    kv_a = jnp.stack([ka, va.reshape(B, T, H_A, HEAD_DIM)], axis=2)
    qb = rms_norm(qb.reshape(B, T, H_B, HEAD_DIM), g_qk_b[0])
    kvb = kvb.reshape(B, T, 6, H_KV, HEAD_DIM)
    kv_nsa = jnp.stack([kvb[:, :, 0], kvb[:, :, 1], rms_norm(kvb[:, :, 2], g_qk_b[2]), kvb[:, :, 3]], axis=2)
    kv_win = jnp.stack([rms_norm(kvb[:, :, 4], g_qk_b[3]), kvb[:, :, 5]], axis=2)
    gb = jax.nn.sigmoid(gb.reshape(B, T, H_B, 3))
    gm = jax.nn.sigmoid(gm.reshape(B, T, 2, D_MODEL))
    return qa, kv_a, qb, kv_nsa, kv_win, gb, gm


def sparse_mixers(qa, kv_a, qb, kv_nsa, kv_win, gb, q_pos0, w_pos0, cmp_pos, cmp_w1, cmp_w2, g_kc):
    f32 = jnp.float32
    B, T = qa.shape[0], qa.shape[1]
    Lp = kv_a.shape[1]
    NB = Lp // MOBA_BLOCK
    NS = Lp // SLC_BLOCK
    scale = HEAD_DIM ** -0.5
    blocks_a = kv_a.reshape(B, NB, MOBA_BLOCK, 2, H_A, HEAD_DIM)
    k_mean = jnp.mean(blocks_a[:, :, :, 0].astype(f32), axis=2).astype(qa.dtype)
    k_cmp = rms_norm(compress(kv_nsa[:, :, 0], cmp_pos[0], cmp_w1[0], cmp_w2[0]), g_kc)
    v_cmp = compress(kv_nsa[:, :, 1], cmp_pos[1], cmp_w1[1], cmp_w2[1])
    n_cmp = k_cmp.shape[1]
    cmp_end = jnp.arange(n_cmp) * CMP_STRIDE + (CMP_LEN - 1)
    blocks_s = kv_nsa[:, :, 2:].reshape(B, NS, SLC_BLOCK, 2, H_KV, HEAD_DIM)
    kv_wp = jnp.pad(kv_win, ((0, 0), (WINDOW, 0), (0, 0), (0, 0), (0, 0)))
    slopes_a = alibi_slopes(H_A)
    slopes_b = alibi_slopes(H_B).reshape(H_KV, GROUP)
    k_a = min(MOBA_TOPK, NB)
    k_s = min(SLC_TOPK, NS)
    tc = math.gcd(T, QBLOCK)
    n_win = WINDOW + tc - 1
    b_idx = jnp.arange(B)[:, None, None, None]
    ha_idx = jnp.arange(H_A)[None, None, :, None]
    hk_idx = jnp.arange(H_KV)[None, None, :, None]

    def one_block(c):
        s0 = c * tc
        t = q_pos0 + s0 + jnp.arange(tc)
        q = lax.dynamic_slice_in_dim(qa, s0, tc, axis=1)
        cur = t // MOBA_BLOCK
        gate = jnp.einsum('bthd,bnhd->bthn', q, k_mean).astype(f32)
        fully_past = jnp.arange(NB)[None, :] < cur[:, None]
        gate = jnp.where(fully_past[None, :, None, :], gate, -jnp.inf)
        top = lax.top_k(gate, k_a)[1]
        own = jnp.broadcast_to(cur[None, :, None, None], (B, tc, H_A, 1))
        sel = jnp.concatenate([top, own], axis=-1)
        ok = jnp.concatenate([top < own, own >= 0], axis=-1)
        kv = blocks_a[b_idx, sel, :, :, ha_idx]
        dist = t[None, :, None, None, None] - (sel[..., None] * MOBA_BLOCK + jnp.arange(MOBA_BLOCK))
        s = jnp.einsum('bthd,bthjsd->bthjs', q, kv[..., 0, :]).astype(f32) * scale - slopes_a[:, None, None] * dist
        mask = ok[..., None] & (dist >= 0)
        p = masked_softmax(s.reshape(B, tc, H_A, -1), mask.reshape(B, tc, H_A, -1)).reshape(s.shape)
        o_a = jnp.einsum('bthjs,bthjsd->bthd', p.astype(kv.dtype), kv[..., 1, :])
        q = lax.dynamic_slice_in_dim(qb, s0, tc, axis=1).reshape(B, tc, H_KV, GROUP, HEAD_DIM)
        g = lax.dynamic_slice_in_dim(gb, s0, tc, axis=1).reshape(B, tc, H_KV, GROUP, 3)
        d_c = t[:, None] - cmp_end[None, :]
        s = jnp.einsum('btkgd,bnkd->btkgn', q, k_cmp).astype(f32) * scale - slopes_b[:, :, None] * d_c[:, None, None, :]
        p_c = masked_softmax(s, (d_c >= 0)[:, None, None, :])
        o_c = jnp.einsum('btkgn,bnkd->btkgd', p_c.astype(v_cmp.dtype), v_cmp)
        imp = jnp.sum(p_c, axis=3)
        imp = (jnp.pad(imp, ((0, 0), (0, 0), (0, 0), (0, 1))) + jnp.pad(imp, ((0, 0), (0, 0), (0, 0), (1, 0))))
        imp = imp.reshape(B, tc, H_KV, NS, SLC_BLOCK // CMP_STRIDE).sum(-1)
        cur_s = t // SLC_BLOCK
        j = jnp.arange(NS)[None, :]
        causal = j <= cur_s[:, None]
        forced = (j == 0) | (j >= cur_s[:, None] - 1)
        imp = jnp.where(forced[None, :, None, :], imp + FORCE_BONUS, imp)
        imp = jnp.where(causal[None, :, None, :], imp, -jnp.inf)
        top = lax.top_k(imp, k_s)[1]
        ok = top <= cur_s[None, :, None, None]
        kv = blocks_s[b_idx, top, :, :, hk_idx]
        dist = t[None, :, None, None, None] - (top[..., None] * SLC_BLOCK + jnp.arange(SLC_BLOCK))
        s = jnp.einsum('btkgd,btkjsd->btkgjs', q, kv[..., 0, :]).astype(f32) * scale - slopes_b[:, :, None, None] * dist[:, :, :, None]
        mask = jnp.broadcast_to((ok[..., None] & (dist >= 0))[:, :, :, None], s.shape)
        p = masked_softmax(s.reshape(B, tc, H_KV, GROUP, -1), mask.reshape(B, tc, H_KV, GROUP, -1)).reshape(s.shape)
        o_s = jnp.einsum('btkgjs,btkjsd->btkgd', p.astype(kv.dtype), kv[..., 1, :])
        kv = lax.dynamic_slice_in_dim(kv_wp, q_pos0 - w_pos0 + s0 + 1, n_win, axis=1)
        pos = q_pos0 + s0 - WINDOW + 1 + jnp.arange(n_win)
        dist = t[:, None] - pos[None, :]
        mask = (dist >= 0) & (dist < WINDOW) & (pos >= w_pos0)[None, :]
        s = jnp.einsum('btkgd,bskd->btkgs', q, kv[:, :, 0]).astype(f32) * scale - slopes_b[:, :, None] * dist[:, None, None, :]
        p = masked_softmax(s, mask[:, None, None, :])
        o_w = jnp.einsum('btkgs,bskd->btkgd', p.astype(kv.dtype), kv[:, :, 1])
        o_b = g[..., 0:1] * o_c + g[..., 1:2] * o_s + g[..., 2:3] * o_w
        return o_a, o_b.reshape(B, tc, H_B, HEAD_DIM)

    o_a, o_b = lax.map(one_block, jnp.arange(T // tc))
    o_a = jnp.moveaxis(o_a, 0, 1).reshape(B, T, W_A)
    o_b = jnp.moveaxis(o_b, 0, 1).reshape(B, T, W_B)
    return o_a, o_b


def assemble(past, new, length):
    parts = ([] if past is None else [past]) + [new]
    cur = sum(a.shape[1] for a in parts)
    if length > cur:
        parts.append(jnp.zeros((new.shape[0], length - cur) + new.shape[2:], new.dtype))
    return jnp.concatenate(parts, axis=1) if len(parts) > 1 else new


def decoder_layer(x, past_a, past_nsa, past_win, w):
    (g_na, w_in, b_in, g_qk_a, g_qk_b, cmp_pos, cmp_w1, cmp_w2, w_ba, w_bb, w_out, g_nf, w_up, w_down) = w
    B, T, _ = x.shape
    past_len = 0 if past_a is None else past_a.shape[1]
    L = past_len + T
    Lp = -(-L // MOBA_BLOCK) * MOBA_BLOCK
    xn = rms_norm(x, g_na)
    qa, kv_a, qb, kv_nsa, kv_win, gb, gm = project(xn, w_in, b_in, g_qk_a, g_qk_b)
    full_a = assemble(past_a, kv_a, Lp)
    full_nsa = assemble(past_nsa, kv_nsa, Lp)
    if past_win is None:
        full_win, w_pos0, keep = kv_win, 0, min(WINDOW, T)
    else:
        full_win = jnp.concatenate([past_win, kv_win], axis=1)
        w_pos0, keep = past_len - past_win.shape[1], past_win.shape[1]
    o_a, o_b = sparse_mixers(qa, full_a, qb, full_nsa, full_win, gb, past_len, w_pos0,
                             cmp_pos, cmp_w1, cmp_w2, g_qk_b[1])
    merged = gm[:, :, 0] * (o_a @ w_ba) + gm[:, :, 1] * (o_b @ w_bb)
    h = x + merged @ w_out
    gate, up = jnp.split(rms_norm(h, g_nf) @ w_up, 2, axis=-1)
    y = h + (jax.nn.silu(gate) * up) @ w_down
    return y, kv_a, kv_nsa, full_win[:, full_win.shape[1] - keep:]


def setup_inputs(seed: int = 0) -> dict:
    key = jax.random.key(seed)
    ks = jax.random.split(key, 20)
    n_pages = PAST_LEN // PAGE_SIZE
    n_pool = (DEC_BATCH * n_pages * 5) // 4
    win_buf = min(WINDOW, PAST_LEN)

    def nrm(k, shape, s):
        return jax.random.normal(k, shape, jnp.float32) * s

    page_table = jax.random.permutation(ks[0], n_pool)[:DEC_BATCH * n_pages]
    page_table = page_table.reshape(DEC_BATCH, n_pages).astype(jnp.int32)
    return {
        'x_prompt': nrm(ks[1], (BATCH, SEQ, D_MODEL), 1.0),
        'x_sample': nrm(ks[2], (DEC_BATCH, DEC_SEQ, D_MODEL), 1.0),
        'cache_moba_kv': nrm(ks[3], (DEPTH, n_pool, PAGE_SIZE, 2, H_A, HEAD_DIM), 1.0),
        'cache_nsa_kv': nrm(ks[4], (DEPTH, n_pool, PAGE_SIZE, 4, H_KV, HEAD_DIM), 1.0),
        'state_win_kv': nrm(ks[5], (DEPTH, DEC_BATCH, win_buf, 2, H_KV, HEAD_DIM), 1.0),
        'page_table': page_table,
        'g_norm_attn': 1.0 + nrm(ks[6], (DEPTH, D_MODEL), 0.01),
        'w_in': nrm(ks[7], (DEPTH, D_MODEL, N_IN), D_MODEL ** -0.5),
        'b_in': nrm(ks[8], (DEPTH, N_IN), 0.01),
        'g_qk_moba': 1.0 + nrm(ks[9], (DEPTH, 2, HEAD_DIM), 0.01),
        'g_qk_nsa': 1.0 + nrm(ks[10], (DEPTH, 4, HEAD_DIM), 0.01),
        'cmp_pos': nrm(ks[11], (DEPTH, 2, CMP_LEN, HEAD_DIM), 0.1),
        'cmp_w1': nrm(ks[12], (DEPTH, 2, CMP_LEN, HEAD_DIM, CMP_HID), (CMP_LEN * HEAD_DIM) ** -0.5),
        'cmp_w2': nrm(ks[13], (DEPTH, 2, CMP_HID, HEAD_DIM), CMP_HID ** -0.5),
        'w_br_moba': nrm(ks[14], (DEPTH, W_A, D_MODEL), W_A ** -0.5),
        'w_br_nsa': nrm(ks[15], (DEPTH, W_B, D_MODEL), W_B ** -0.5),
        'w_out': nrm(ks[16], (DEPTH, D_MODEL, D_MODEL), D_MODEL ** -0.5),
        'g_norm_ffn': 1.0 + nrm(ks[17], (DEPTH, D_MODEL), 0.01),
        'w_up': nrm(ks[18], (DEPTH, D_MODEL, 2 * D_FF), D_MODEL ** -0.5),
        'w_down': nrm(ks[19], (DEPTH, D_FF, D_MODEL), D_FF ** -0.5),
    }


def reference(x_prompt, x_sample, cache_moba_kv, cache_nsa_kv, state_win_kv, page_table,
              g_norm_attn, w_in, b_in, g_qk_moba, g_qk_nsa, cmp_pos, cmp_w1, cmp_w2,
              w_br_moba, w_br_nsa, w_out, g_norm_ffn, w_up, w_down):
    n_seq, n_pages = page_table.shape
    past_len = n_pages * PAGE_SIZE
    y_prompt, y_sample = x_prompt, x_sample
    moba_p, nsa_p, win_p, moba_s, nsa_s, win_s = [], [], [], [], [], []
    for l in range(DEPTH):
        w = (g_norm_attn[l], w_in[l], b_in[l], g_qk_moba[l], g_qk_nsa[l], cmp_pos[l], cmp_w1[l],
             cmp_w2[l], w_br_moba[l], w_br_nsa[l], w_out[l], g_norm_ffn[l], w_up[l], w_down[l])
        y_prompt, a_p, n_p, wn_p = decoder_layer(y_prompt, None, None, None, w)
        past_a = cache_moba_kv[l, page_table].reshape((n_seq, past_len) + cache_moba_kv.shape[3:])
        past_nsa = cache_nsa_kv[l, page_table].reshape((n_seq, past_len) + cache_nsa_kv.shape[3:])
        y_sample, a_s, n_s, wn_s = decoder_layer(y_sample, past_a, past_nsa, state_win_kv[l], w)
        moba_p.append(a_p)
        nsa_p.append(n_p)
        win_p.append(wn_p)
        moba_s.append(a_s)
        nsa_s.append(n_s)
        win_s.append(wn_s)
    moba_kv_prompt = jnp.stack(moba_p)
    nsa_kv_prompt = jnp.stack(nsa_p)
    win_kv_prompt = jnp.stack(win_p)
    moba_kv_sample = jnp.stack(moba_s)
    nsa_kv_sample = jnp.stack(nsa_s)
    win_kv_sample = jnp.stack(win_s)
    return (y_prompt, y_sample, moba_kv_prompt, nsa_kv_prompt, win_kv_prompt, moba_kv_sample, nsa_kv_sample, win_kv_sample)
```

```python
import functools
import math

import numpy as np
import jax
import jax.numpy as jnp
from jax import lax
from jax.experimental import pallas as pl
from jax.experimental.pallas import tpu as pltpu

F32 = jnp.float32
BF16 = jnp.bfloat16

D_MODEL = 1024
PAGE_SIZE = 128
HEAD_DIM = 64
H_A = 8
H_B = 8
H_KV = 2
GROUP = H_B // H_KV
MOBA_BLOCK = 256
MOBA_TOPK = 3
CMP_LEN = 32
CMP_STRIDE = 16
SLC_BLOCK = 64
SLC_TOPK = 16
WINDOW = 512
CMP_HID = 2 * HEAD_DIM
FORCE_BONUS = 1e4
RMS_EPS = 1e-6
W_A = H_A * HEAD_DIM
W_B = H_B * HEAD_DIM
W_KV = H_KV * HEAD_DIM
QK_SCALE = HEAD_DIM ** -0.5

LANES = 128
VMEM_LIMIT = 56 * 1024 * 1024
NEG_INF = float("-inf")

SLOPES_A = tuple(2.0 ** (-8.0 * (i + 1) / H_A) for i in range(H_A))
SLOPES_B = tuple(2.0 ** (-8.0 * (i + 1) / H_B) for i in range(H_B))


def _round_up(x, m):
    return -(-x // m) * m


def _cparams(*sem):
    return pltpu.CompilerParams(dimension_semantics=sem, vmem_limit_bytes=VMEM_LIMIT)


def _dot(a, b):
    return jnp.dot(a, b, preferred_element_type=F32)


def _dot_t(a, b):
    return lax.dot_general(a, b, (((1,), (1,)), ((), ())), preferred_element_type=F32)


def _head_sumsq(z, bd):
    zz = z * z
    hi = zz.astype(BF16)
    lo = (zz - hi.astype(F32)).astype(BF16)
    return _dot(hi, bd) + _dot(lo, bd)


def _head_norm(z, g, bd):
    ss = _head_sumsq(z, bd)
    return (z * lax.rsqrt(ss * (1.0 / HEAD_DIM) + RMS_EPS)) * g


C_QA, C_KA, C_VA, C_QB, C_KVB, C_END = 0, 512, 1024, 1536, 2048, 2816


def _inproj_kernel(x_ref, gna_ref, w1_ref, b1_ref, gv_ref, wg_ref, bg_ref, wgm_ref, bgm_ref, bd_ref,
                   qa_ref, kva_ref, kab_ref, vab_ref, qb_ref, kvn_ref, ksb_ref, vsb_ref, kvw_ref,
                   kwb_ref, vwb_ref, gb_ref, gm_ref, *maybe_km, n_blk):
    x = x_ref[...]
    r = lax.rsqrt(jnp.mean(x * x, axis=-1, keepdims=True) + RMS_EPS)
    xb = ((x * r) * gna_ref[...]).astype(BF16)
    bd = bd_ref[...]
    bd1 = bd_ref[:LANES, :LANES]

    def sec(a, b):
        return _dot(xb, w1_ref[:, a:b]) + b1_ref[:, a:b]

    qa = _head_norm(sec(C_QA, C_KA), gv_ref[:, C_QA:C_KA], bd)
    qa_ref[...] = (qa * QK_SCALE).astype(BF16)

    ka = _head_norm(sec(C_KA, C_VA), gv_ref[:, C_KA:C_VA], bd)
    va = sec(C_VA, C_QB)
    kva_ref[:, :W_A] = ka
    kva_ref[:, W_A:] = va
    kab_ref[...] = ka.astype(BF16)
    vab_ref[...] = va.astype(BF16)
    if n_blk:
        km_ref = maybe_km[0]
        km_ref[0] = jnp.mean(ka.reshape(n_blk, MOBA_BLOCK, W_A), axis=1)

    qb = _head_norm(sec(C_QB, C_KVB), gv_ref[:, C_QB:C_KVB], bd)
    qb_ref[...] = (qb * QK_SCALE).astype(BF16)

    kvb = sec(C_KVB, C_END)
    ks = _head_norm(kvb[:, 256:384], gv_ref[:, C_KVB + 256:C_KVB + 384], bd1)
    vs = kvb[:, 384:512]
    kw = _head_norm(kvb[:, 512:640], gv_ref[:, C_KVB + 512:C_KVB + 640], bd1)
    vw = kvb[:, 640:768]
    kvn_ref[:, 0:256] = kvb[:, 0:256]
    kvn_ref[:, 256:384] = ks
    kvn_ref[:, 384:512] = vs
    ksb_ref[...] = ks.astype(BF16)
    vsb_ref[...] = vs.astype(BF16)
    kvw_ref[:, 0:128] = kw
    kvw_ref[:, 128:256] = vw
    kwb_ref[...] = kw.astype(BF16)
    vwb_ref[...] = vw.astype(BF16)

    gb_ref[...] = jax.nn.sigmoid(_dot(xb, wg_ref[...]) + bg_ref[...])
    gm_ref[...] = jax.nn.sigmoid(_dot(xb, wgm_ref[...]) + bgm_ref[...])


def _inproj(x, wts, with_kmean):
    n = x.shape[0]
    tm = 512 if n % 512 == 0 else n
    n_blk = tm // MOBA_BLOCK if with_kmean else 0
    grid = (n // tm,)

    def rows(w):
        return pl.BlockSpec((tm, w), lambda i: (i, 0))

    def full(a):
        return pl.BlockSpec(a.shape, lambda i: (0,) * a.ndim)

    ins = [x, wts["gna"], wts["w1"], wts["b1"], wts["gv"], wts["wg"], wts["bg"], wts["wgm"], wts["bgm"], wts["bd"]]
    in_specs = [rows(D_MODEL)] + [full(a) for a in ins[1:]]
    outs = [
        (W_A, BF16), (2 * W_A, F32), (W_A, BF16), (W_A, BF16), (W_B, BF16), (4 * W_KV, F32),
        (W_KV, BF16), (W_KV, BF16), (2 * W_KV, F32), (W_KV, BF16), (W_KV, BF16), (LANES, F32), (2 * D_MODEL, F32),
    ]
    out_shape = [jax.ShapeDtypeStruct((n, w), dt) for w, dt in outs]
    out_specs = [rows(w) for w, _ in outs]
    if n_blk:
        out_shape.append(jax.ShapeDtypeStruct((n // tm, n_blk, W_A), F32))
        out_specs.append(pl.BlockSpec((1, n_blk, W_A), lambda i: (i, 0, 0)))
    res = pl.pallas_call(
        functools.partial(_inproj_kernel, n_blk=n_blk),
        grid=grid, in_specs=in_specs, out_specs=out_specs, out_shape=out_shape,
        compiler_params=_cparams("parallel"),
    )(*ins)
    names = ["qa", "kva", "kab", "vab", "qb", "kvn", "ksb", "vsb", "kvw", "kwb", "vwb", "gb", "gm"]
    out = dict(zip(names, res))
    if n_blk:
        out["kmean"] = res[-1].reshape(n // MOBA_BLOCK, W_A)
    return out


def _compress_kernel(x_ref, xh_ref, pos_ref, wa_ref, wb_ref, w2_ref, gk_ref, bd_ref, kc_ref, vc_ref, *, row_w, tc):
    bd1 = bd_ref[:LANES, :LANES]

    def half_proj(xr, w_ref, half):
        outs = []
        for br in range(2):
            acc = None
            for l in range(CMP_STRIDE):
                a = l * row_w + br * LANES
                xl = (xr[0, :, a:a + LANES] + pos_ref[half, br, l]).astype(BF16)
                t = _dot(xl, w_ref[br, l])
                acc = t if acc is None else acc + t
            outs.append(acc)
        return outs

    a_k, a_v = half_proj(x_ref, wa_ref, 0)
    b_k, b_v = half_proj(x_ref, wb_ref, 1)
    bh_k, bh_v = half_proj(xh_ref, wb_ref, 1)
    rows = lax.broadcasted_iota(jnp.int32, (tc, 2 * CMP_HID), 0)

    def shift_up(b, bh):
        rolled = pltpu.roll(b, tc - 1, 0)
        return jnp.where(rows == tc - 1, jnp.broadcast_to(bh[0:1], b.shape), rolled)

    hid_k = a_k + shift_up(b_k, bh_k)
    hid_v = a_v + shift_up(b_v, bh_v)
    hid_k = hid_k * jax.nn.sigmoid(hid_k)
    hid_v = hid_v * jax.nn.sigmoid(hid_v)
    out_k = _dot(hid_k.astype(BF16), w2_ref[0])
    out_v = _dot(hid_v.astype(BF16), w2_ref[1])
    kc_ref[0] = _head_norm(out_k, gk_ref[...], bd1).astype(BF16)
    vc_ref[0] = out_v.astype(BF16)


def _compress(rows3, row_w, wts, tc):
    b, nc, cw = rows3.shape
    nt = nc // tc
    last_h = nc // 8 - 1
    full = lambda a: pl.BlockSpec(a.shape, lambda i, j: (0,) * a.ndim)
    ins = [rows3, rows3, wts["cpos"], wts["cwa"], wts["cwb"], wts["cw2"], wts["gkc"], wts["bd"]]
    in_specs = [
        pl.BlockSpec((1, tc, cw), lambda i, j: (i, j, 0)),
        pl.BlockSpec((1, 8, cw), lambda i, j: (i, jnp.minimum((j + 1) * (tc // 8), last_h), 0)),
    ] + [full(a) for a in ins[2:]]
    out_shape = [jax.ShapeDtypeStruct((b, nc, LANES), BF16)] * 2
    out_specs = [pl.BlockSpec((1, tc, LANES), lambda i, j: (i, j, 0))] * 2
    return pl.pallas_call(
        functools.partial(_compress_kernel, row_w=row_w, tc=tc),
        grid=(b, nt), in_specs=in_specs, out_specs=out_specs, out_shape=out_shape,
        compiler_params=_cparams("parallel", "arbitrary"),
    )(*ins)


def _flash_init(m_ref, l_ref, acc_ref):
    m_ref[...] = jnp.full(m_ref.shape, NEG_INF, F32)
    l_ref[...] = jnp.zeros(l_ref.shape, F32)
    acc_ref[...] = jnp.zeros(acc_ref.shape, F32)


def _flash_step(s, valid, v, m_ref, l_ref, acc_ref, i):
    s = jnp.where(valid, s, NEG_INF)
    m_old = m_ref[i]
    m_new = jnp.maximum(m_old, jnp.max(s, axis=1, keepdims=True))
    m_safe = jnp.where(m_new == NEG_INF, 0.0, m_new)
    alpha = jnp.exp(m_old - m_safe)
    p = jnp.exp(s - m_safe[:, :1])
    l_ref[i] = alpha * l_ref[i] + jnp.sum(p, axis=1, keepdims=True)
    acc_ref[i] = alpha * acc_ref[i] + _dot(p.astype(BF16), v)
    m_ref[i] = m_new


def _flash_out(l_ref, acc_ref, i):
    l = l_ref[i]
    return acc_ref[i] / jnp.where(l > 0, l, 1.0)


def _topk_mask(v, idx_iota, k):
    width = v.shape[1]

    def body(_, carry):
        v, sel = carry
        mx = jnp.max(v, axis=1, keepdims=True)
        idx = jnp.min(jnp.where(v == mx, idx_iota, width), axis=1, keepdims=True)
        hit = idx_iota == idx
        sel = jnp.where(hit & (mx > NEG_INF), 1.0, sel)
        return jnp.where(hit, NEG_INF, v), sel

    return lax.fori_loop(0, k, body, (v, jnp.zeros(v.shape, F32)))[1]


def _cmp_to_block_matrix(ncp, nsp, n_cmp, n_s):
    m = np.zeros((ncp, nsp), np.float32)
    per = SLC_BLOCK // CMP_STRIDE
    for i in range(n_s * per):
        for t in (i, i - 1):
            if 0 <= t < n_cmp:
                m[t, i // per] += 1.0
    return jnp.asarray(m)


def _attn_prompt_kernel(qa_ref, qb_ref, gb_ref, km_ref, ka_ref, va_ref, kc_ref, vc_ref, ks_ref, vs_ref,
                        kw_ref, vw_ref, mm_ref, oa_ref, ob_ref,
                        m_sc, l_sc, acc_sc, sela_sc, sels_sc, oc_sc, os_sc, *, n_cmp, k_a, k_s):
    c = pl.program_id(1)
    tq = MOBA_BLOCK
    row = lax.broadcasted_iota(jnp.int32, (tq, 1), 0)
    lane = lax.broadcasted_iota(jnp.int32, (tq, LANES), 1)
    is_lo = lane < HEAD_DIM
    col = lax.broadcasted_iota(jnp.int32, (tq, tq), 1)
    rc_i = row - col
    rc = rc_i.astype(F32)
    t_abs = c * tq + row

    def masked_q(ref, tile, half):
        q = ref[0, :, tile * LANES:(tile + 1) * LANES]
        keep = is_lo if half == 0 else jnp.logical_not(is_lo)
        return jnp.where(keep, q, jnp.zeros_like(q))

    nbp = km_ref.shape[1]
    nb_iota = lax.broadcasted_iota(jnp.int32, (tq, nbp), 1)
    for h in range(H_A):
        j, half = divmod(h, 2)
        kmp = km_ref[0, :, j * LANES:(j + 1) * LANES].astype(BF16)
        gate = _dot_t(masked_q(qa_ref, j, half), kmp)
        gate = jnp.where(nb_iota < c, gate, NEG_INF)
        sela_sc[h] = _topk_mask(gate, nb_iota, k_a)

    def moba_block(n, own):
        koff = pl.multiple_of(n * tq, tq)
        dist = rc + ((c - n) * tq).astype(F32)
        for h in range(H_A):
            j, half = divmod(h, 2)
            kp = ka_ref[0, pl.ds(koff, tq), j * LANES:(j + 1) * LANES]
            vp = va_ref[0, pl.ds(koff, tq), j * LANES:(j + 1) * LANES]
            s = _dot_t(masked_q(qa_ref, j, half), kp) - SLOPES_A[h] * dist
            if own:
                valid = rc_i >= 0
            else:
                picked = jnp.sum(jnp.where(nb_iota == n, sela_sc[h], 0.0), axis=1, keepdims=True)
                valid = picked > 0.5
            _flash_step(s, valid, vp, m_sc, l_sc, acc_sc, h)

    _flash_init(m_sc, l_sc, acc_sc)
    moba_block(c, True)

    def moba_body(n, carry):
        moba_block(n, False)
        return carry

    lax.fori_loop(0, c, moba_body, 0)
    for j in range(H_A // 2):
        o = jnp.where(is_lo, _flash_out(l_sc, acc_sc, 2 * j), _flash_out(l_sc, acc_sc, 2 * j + 1))
        oa_ref[0, :, j * LANES:(j + 1) * LANES] = o.astype(BF16)

    ncp = kc_ref.shape[1]
    nsp = mm_ref.shape[1]
    jc = lax.broadcasted_iota(jnp.int32, (tq, ncp), 1)
    d_c = t_abs - (jc * CMP_STRIDE + (CMP_LEN - 1))
    c_valid = (d_c >= 0) & (jc < n_cmp)
    d_cf = d_c.astype(F32)
    sj = lax.broadcasted_iota(jnp.int32, (tq, nsp), 1)
    cur_s = t_abs // SLC_BLOCK
    forced = (sj == 0) | (sj >= cur_s - 1)
    causal_s = sj <= cur_s
    for k in range(H_KV):
        imp = jnp.zeros((tq, ncp), F32)
        for g in range(GROUP):
            i = k * GROUP + g
            s = _dot_t(masked_q(qb_ref, g, k), kc_ref[0]) - SLOPES_B[i] * d_cf
            s = jnp.where(c_valid, s, NEG_INF)
            m = jnp.max(s, axis=1, keepdims=True)
            m = jnp.where(m == NEG_INF, 0.0, m)
            e = jnp.where(c_valid, jnp.exp(s - m), 0.0)
            d = jnp.sum(e, axis=1, keepdims=True)
            p = e / jnp.where(d > 0, d, 1.0)
            imp = imp + p
            oc_sc[i] = _dot(p.astype(BF16), vc_ref[0])
        impb = jnp.dot(imp, mm_ref[...], preferred_element_type=F32, precision=lax.Precision.HIGHEST)
        impb = jnp.where(forced, impb + FORCE_BONUS, impb)
        impb = jnp.where(causal_s, impb, NEG_INF)
        sels_sc[k] = _topk_mask(impb, sj, k_s)

    e_row = lax.broadcasted_iota(jnp.int32, (nsp, tq), 0)
    e_col = lax.broadcasted_iota(jnp.int32, (nsp, tq), 1) // SLC_BLOCK
    per_tile = tq // SLC_BLOCK

    def slc_block(n, own):
        koff = pl.multiple_of(n * tq, tq)
        dist = rc + ((c - n) * tq).astype(F32)
        expand = (e_row == n * per_tile + e_col).astype(BF16)
        for k in range(H_KV):
            kp = ks_ref[0, pl.ds(koff, tq), :]
            vp = vs_ref[0, pl.ds(koff, tq), :]
            valid = _dot(sels_sc[k].astype(BF16), expand) > 0.5
            if own:
                valid = valid & (rc_i >= 0)
            for g in range(GROUP):
                i = k * GROUP + g
                s = _dot_t(masked_q(qb_ref, g, k), kp) - SLOPES_B[i] * dist
                _flash_step(s, valid, vp, m_sc, l_sc, acc_sc, i)

    _flash_init(m_sc, l_sc, acc_sc)
    slc_block(c, True)

    def slc_body(n, carry):
        slc_block(n, False)
        return carry

    lax.fori_loop(0, c, slc_body, 0)
    for i in range(H_B):
        os_sc[i] = _flash_out(l_sc, acc_sc, i)

    def win_block(back):
        n = c - back
        koff = pl.multiple_of(n * tq, tq)
        dist_i = rc_i + back * tq
        dist = dist_i.astype(F32)
        valid = (dist_i >= 0) & (dist_i < WINDOW)
        kp = kw_ref[0, pl.ds(koff, tq), :]
        vp = vw_ref[0, pl.ds(koff, tq), :]
        for k in range(H_KV):
            for g in range(GROUP):
                i = k * GROUP + g
                s = _dot_t(masked_q(qb_ref, g, k), kp) - SLOPES_B[i] * dist
                _flash_step(s, valid, vp, m_sc, l_sc, acc_sc, i)

    _flash_init(m_sc, l_sc, acc_sc)
    win_block(0)
    for back in range(1, WINDOW // tq + 1):
        pl.when(c >= back)(functools.partial(win_block, back))

    gb = gb_ref[0]
    for j in range(GROUP):
        def comb(i):
            return (gb[:, 3 * i:3 * i + 1] * oc_sc[i] + gb[:, 3 * i + 1:3 * i + 2] * os_sc[i]
                    + gb[:, 3 * i + 2:3 * i + 3] * _flash_out(l_sc, acc_sc, i))
        ob_ref[0, :, j * LANES:(j + 1) * LANES] = jnp.where(is_lo, comb(j), comb(GROUP + j)).astype(BF16)


def _attn_prompt(p, kc, vc, b, t):
    tq = MOBA_BLOCK
    nb = t // tq
    n_s = t // SLC_BLOCK
    n_cmp = t // CMP_STRIDE - 1
    nbp = _round_up(nb, LANES)
    nsp = _round_up(n_s, LANES)
    ncp = kc.shape[1]
    km = jnp.pad(p["kmean"].reshape(b, nb, W_A), ((0, 0), (0, nbp - nb), (0, 0)))
    mm = _cmp_to_block_matrix(ncp, nsp, n_cmp, n_s)
    r3 = lambda a: a.reshape(b, t, a.shape[-1])
    qtile = lambda w: pl.BlockSpec((1, tq, w), lambda i, c: (i, c, 0))
    seq = lambda a: pl.BlockSpec((1,) + a.shape[1:], lambda i, c: (i, 0, 0))
    ins = [r3(p["qa"]), r3(p["qb"]), r3(p["gb"]), km, r3(p["kab"]), r3(p["vab"]), kc, vc,
           r3(p["ksb"]), r3(p["vsb"]), r3(p["kwb"]), r3(p["vwb"]), mm]
    in_specs = [qtile(W_A), qtile(W_B), qtile(LANES)] + [seq(a) for a in ins[3:12]] + [
        pl.BlockSpec(mm.shape, lambda i, c: (0, 0))]
    out_shape = [jax.ShapeDtypeStruct((b, t, W_A), BF16), jax.ShapeDtypeStruct((b, t, W_B), BF16)]
    out_specs = [qtile(W_A), qtile(W_B)]
    scratch = [
        pltpu.VMEM((H_A, tq, LANES), F32), pltpu.VMEM((H_A, tq, LANES), F32), pltpu.VMEM((H_A, tq, LANES), F32),
        pltpu.VMEM((H_A, tq, nbp), F32), pltpu.VMEM((H_KV, tq, nsp), F32),
        pltpu.VMEM((H_B, tq, LANES), F32), pltpu.VMEM((H_B, tq, LANES), F32),
    ]
    oa, ob = pl.pallas_call(
        functools.partial(_attn_prompt_kernel, n_cmp=n_cmp, k_a=min(MOBA_TOPK, nb), k_s=min(SLC_TOPK, n_s)),
        grid=(b, nb), in_specs=in_specs, out_specs=out_specs, out_shape=out_shape, scratch_shapes=scratch,
        compiler_params=_cparams("parallel", "arbitrary"),
    )(*ins)
    return oa.reshape(b * t, W_A), ob.reshape(b * t, W_B)


def _assemble_kernel(pt_ref, moba_ref, nsa_ref, ka_ref, va_ref, cmp_ref, ks_ref, vs_ref, ksum_ref):
    del pt_ref
    m = moba_ref[0]
    k = m[:, :W_A]
    ka_ref[0] = k.astype(BF16)
    va_ref[0] = m[:, W_A:].astype(BF16)
    ksum_ref[0, 0] = jnp.sum(k, axis=0, keepdims=True)
    n = nsa_ref[0]
    cmp_ref[0] = n[:, :2 * W_KV]
    ks_ref[0] = n[:, 2 * W_KV:3 * W_KV].astype(BF16)
    vs_ref[0] = n[:, 3 * W_KV:].astype(BF16)


def _assemble(page_table, cache_moba, cache_nsa):
    s, n_pages = page_table.shape
    past = n_pages * PAGE_SIZE
    moba = cache_moba.reshape(cache_moba.shape[0], PAGE_SIZE, 2 * W_A)
    nsa = cache_nsa.reshape(cache_nsa.shape[0], PAGE_SIZE, 4 * W_KV)
    page = lambda w: pl.BlockSpec((1, PAGE_SIZE, w), lambda b, p, pt: (pt[b, p], 0, 0))
    dst = lambda w: pl.BlockSpec((1, PAGE_SIZE, w), lambda b, p, pt: (b, p, 0))
    outs = [(W_A, BF16), (W_A, BF16), (2 * W_KV, F32), (W_KV, BF16), (W_KV, BF16)]
    out_shape = [jax.ShapeDtypeStruct((s, past, w), dt) for w, dt in outs]
    out_shape.append(jax.ShapeDtypeStruct((s, n_pages, 1, W_A), F32))
    out_specs = [dst(w) for w, _ in outs] + [pl.BlockSpec((1, 1, 1, W_A), lambda b, p, pt: (b, p, 0, 0))]
    res = pl.pallas_call(
        _assemble_kernel,
        grid_spec=pltpu.PrefetchScalarGridSpec(
            num_scalar_prefetch=1, grid=(s, n_pages),
            in_specs=[page(2 * W_A), page(4 * W_KV)], out_specs=out_specs),
        out_shape=out_shape, compiler_params=_cparams("parallel", "arbitrary"),
    )(page_table, moba, nsa)
    pages_per_blk = MOBA_BLOCK // PAGE_SIZE
    return list(res[:5]) + [res[5].reshape(s, n_pages // pages_per_blk, pages_per_blk * W_A)]


SAMPLE_KV_TILE = 2048


def _attn_sample_kernel(qa8_ref, qb16_ref, g16_ref, ksum_ref, kc_ref, vc_ref, mm_ref, ka_ref, va_ref, ks_ref, vs_ref,
                        kan_ref, van_ref, ksn_ref, vsn_ref, kwa_ref, vwa_ref, oa_ref, ob_ref,
                        ma_sc, la_sc, acca_sc, mb_sc, lb_sc, accb_sc, sela_sc, sels_sc, oc_sc, os_sc,
                        *, past, t_new, nb, k_a, k_s):
    kt = pl.program_id(1)
    r_tile = ka_ref.shape[1]
    n_pair = H_A // 2
    ra, rb = 2 * t_new, GROUP * t_new
    row_a = lax.broadcasted_iota(jnp.int32, (ra, 1), 0)
    row_b = lax.broadcasted_iota(jnp.int32, (rb, 1), 0)
    t_a = past + row_a % t_new
    t_b = past + row_b % t_new
    slope_a = [jnp.where(row_a < t_new, SLOPES_A[2 * j], SLOPES_A[2 * j + 1]).astype(F32) for j in range(n_pair)]
    slope_b = []
    for k in range(H_KV):
        sl = jnp.full((rb, 1), SLOPES_B[k * GROUP + GROUP - 1], F32)
        for g in range(GROUP - 2, -1, -1):
            sl = jnp.where(row_b < (g + 1) * t_new, SLOPES_B[k * GROUP + g], sl)
        slope_b.append(sl)
    nb_iota = lax.broadcasted_iota(jnp.int32, (ra, nb), 1)
    nsp = mm_ref.shape[1]
    cur_s = past // SLC_BLOCK

    @pl.when(kt == 0)
    def _():
        _flash_init(ma_sc, la_sc, acca_sc)
        _flash_init(mb_sc, lb_sc, accb_sc)
        km = (ksum_ref[0, :, :W_A] + ksum_ref[0, :, W_A:]) * (1.0 / MOBA_BLOCK)
        for j in range(n_pair):
            gate = _dot_t(qa8_ref[0, j], km[:, j * LANES:(j + 1) * LANES].astype(BF16))
            sela_sc[j] = _topk_mask(gate, nb_iota, k_a)
        nc = kc_ref.shape[1]
        jc = lax.broadcasted_iota(jnp.int32, (rb, nc), 1)
        d_c = t_b - (jc * CMP_STRIDE + (CMP_LEN - 1))
        c_valid = d_c >= 0
        d_cf = d_c.astype(F32)
        sj = lax.broadcasted_iota(jnp.int32, (rb, nsp), 1)
        forced = (sj == 0) | (sj >= cur_s - 1)
        causal_s = sj <= cur_s
        gr = lax.broadcasted_iota(jnp.int32, (rb, rb), 0) % t_new
        gc = lax.broadcasted_iota(jnp.int32, (rb, rb), 1) % t_new
        group_sum = (gr == gc).astype(F32)
        for k in range(H_KV):
            s = _dot_t(qb16_ref[0, k], kc_ref[0]) - slope_b[k] * d_cf
            s = jnp.where(c_valid, s, NEG_INF)
            m = jnp.max(s, axis=1, keepdims=True)
            m = jnp.where(m == NEG_INF, 0.0, m)
            e = jnp.where(c_valid, jnp.exp(s - m), 0.0)
            d = jnp.sum(e, axis=1, keepdims=True)
            p = e / jnp.where(d > 0, d, 1.0)
            oc_sc[k] = _dot(p.astype(BF16), vc_ref[0])
            pb = jnp.dot(p, mm_ref[...], preferred_element_type=F32, precision=lax.Precision.HIGHEST)
            impb = jnp.dot(group_sum, pb, preferred_element_type=F32, precision=lax.Precision.HIGHEST)
            impb = jnp.where(forced, impb + FORCE_BONUS, impb)
            impb = jnp.where(causal_s, impb, NEG_INF)
            sels_sc[k] = _topk_mask(impb, sj, k_s)

    blk_per_tile = r_tile // MOBA_BLOCK
    col_blk = lax.broadcasted_iota(jnp.int32, (ra, MOBA_BLOCK), 1)
    for i in range(blk_per_tile):
        n = kt * blk_per_tile + i
        dist = (t_a - (n * MOBA_BLOCK + col_blk)).astype(F32)
        for j in range(n_pair):
            kp = ka_ref[0, i * MOBA_BLOCK:(i + 1) * MOBA_BLOCK, j * LANES:(j + 1) * LANES]
            vp = va_ref[0, i * MOBA_BLOCK:(i + 1) * MOBA_BLOCK, j * LANES:(j + 1) * LANES]
            s = _dot_t(qa8_ref[0, j], kp) - slope_a[j] * dist
            picked = jnp.sum(jnp.where(nb_iota == n, sela_sc[j], 0.0), axis=1, keepdims=True)
            _flash_step(s, picked > 0.5, vp, ma_sc, la_sc, acca_sc, j)

    blocks_per_tile = r_tile // SLC_BLOCK
    first_blk = kt * blocks_per_tile
    lane_tile = pl.multiple_of((first_blk // LANES) * LANES, LANES)
    e_row = lax.broadcasted_iota(jnp.int32, (LANES, r_tile), 0)
    e_col = lax.broadcasted_iota(jnp.int32, (LANES, r_tile), 1) // SLC_BLOCK
    expand = (e_row == first_blk % LANES + e_col).astype(BF16)
    col_t = lax.broadcasted_iota(jnp.int32, (rb, r_tile), 1)
    dist_t = (t_b - (kt * r_tile + col_t)).astype(F32)
    for k in range(H_KV):
        sel_t = sels_sc[k, :, pl.ds(lane_tile, LANES)]
        valid = _dot(sel_t.astype(BF16), expand) > 0.5
        s = _dot_t(qb16_ref[0, k], ks_ref[0]) - slope_b[k] * dist_t
        _flash_step(s, valid, vs_ref[0], mb_sc, lb_sc, accb_sc, k)

    @pl.when(kt == pl.num_programs(1) - 1)
    def _():
        dist_a = row_a % t_new - lax.broadcasted_iota(jnp.int32, (ra, LANES), 1)
        for j in range(n_pair):
            s = _dot_t(qa8_ref[0, j], kan_ref[0, :, j * LANES:(j + 1) * LANES]) - slope_a[j] * dist_a.astype(F32)
            _flash_step(s, dist_a >= 0, van_ref[0, :, j * LANES:(j + 1) * LANES], ma_sc, la_sc, acca_sc, j)
            oa_ref[0, j] = _flash_out(la_sc, acca_sc, j)
        dist_b = row_b % t_new - lax.broadcasted_iota(jnp.int32, (rb, LANES), 1)
        sj = lax.broadcasted_iota(jnp.int32, (rb, nsp), 1)
        for k in range(H_KV):
            own = jnp.sum(jnp.where(sj == cur_s, sels_sc[k], 0.0), axis=1, keepdims=True) > 0.5
            s = _dot_t(qb16_ref[0, k], ksn_ref[0]) - slope_b[k] * dist_b.astype(F32)
            _flash_step(s, own & (dist_b >= 0), vsn_ref[0], mb_sc, lb_sc, accb_sc, k)
            os_sc[k] = _flash_out(lb_sc, accb_sc, k)
        _flash_init(mb_sc, lb_sc, accb_sc)
        n_w = kwa_ref.shape[1]
        dist_w = WINDOW + row_b % t_new - lax.broadcasted_iota(jnp.int32, (rb, n_w), 1)
        valid_w = (dist_w >= 0) & (dist_w < WINDOW)
        lane_b = lax.broadcasted_iota(jnp.int32, (rb, LANES), 1)
        comb = []
        for k in range(H_KV):
            s = _dot_t(qb16_ref[0, k], kwa_ref[0]) - slope_b[k] * dist_w.astype(F32)
            _flash_step(s, valid_w, vwa_ref[0], mb_sc, lb_sc, accb_sc, k)
            g = g16_ref[0, k]
            comb.append(g[:, 0:1] * oc_sc[k] + g[:, 1:2] * os_sc[k] + g[:, 2:3] * _flash_out(lb_sc, accb_sc, k))
        ob_ref[0] = jnp.where(lane_b < HEAD_DIM, comb[0], comb[1])


def _attn_sample(p, asm, kc, vc, state_win, s, t_new, past):
    r_tile = SAMPLE_KV_TILE
    assert past % r_tile == 0 and state_win.shape[1] == WINDOW and LANES % (r_tile // SLC_BLOCK) == 0
    ka, va, _, ks, vs, ksum = asm
    nb = past // MOBA_BLOCK
    lp = _round_up(past + t_new, MOBA_BLOCK)
    n_s = lp // SLC_BLOCK
    n_cmp = lp // CMP_STRIDE - 1
    nsp = _round_up(n_s, LANES)
    nc = kc.shape[1]
    mm = _cmp_to_block_matrix(nc, nsp, min(n_cmp, nc), n_s)
    lane = jnp.arange(LANES)
    lo = (lane < HEAD_DIM)

    qa = p["qa"].reshape(s, t_new, H_A // 2, LANES).transpose(0, 2, 1, 3)
    zero = jnp.zeros_like(qa)
    qa8 = jnp.concatenate([jnp.where(lo, qa, zero), jnp.where(lo, zero, qa)], axis=2)
    qb = p["qb"].reshape(s, t_new, GROUP, LANES).transpose(0, 2, 1, 3).reshape(s, 1, GROUP * t_new, LANES)
    zero = jnp.zeros_like(qb)
    qb16 = jnp.concatenate([jnp.where(lo, qb, zero), jnp.where(lo, zero, qb)], axis=1)
    g16 = p["gb"][:, :3 * H_B].reshape(s, t_new, H_KV, GROUP, 3).transpose(0, 2, 3, 1, 4)
    g16 = jnp.pad(g16.reshape(s, H_KV, GROUP * t_new, 3), ((0, 0), (0, 0), (0, 0), (0, LANES - 3)))

    pad_new = lambda a: jnp.pad(a.reshape(s, t_new, a.shape[-1]), ((0, 0), (0, LANES - t_new), (0, 0)))
    win = state_win.reshape(s, WINDOW, 2, W_KV).astype(BF16)
    kwa = jnp.concatenate([win[:, :, 0], pad_new(p["kwb"])], axis=1)
    vwa = jnp.concatenate([win[:, :, 1], pad_new(p["vwb"])], axis=1)

    ins = [qa8, qb16, g16, ksum, kc, vc, mm, ka, va, ks, vs,
           pad_new(p["kab"]), pad_new(p["vab"]), pad_new(p["ksb"]), pad_new(p["vsb"]), kwa, vwa]
    seq = lambda a: pl.BlockSpec((1,) + a.shape[1:], lambda i, k: (i,) + (0,) * (a.ndim - 1))
    tile = lambda w: pl.BlockSpec((1, r_tile, w), lambda i, k: (i, k, 0))
    in_specs = ([seq(a) for a in ins[:6]] + [pl.BlockSpec(mm.shape, lambda i, k: (0, 0))]
                + [tile(W_A), tile(W_A), tile(W_KV), tile(W_KV)] + [seq(a) for a in ins[11:]])
    ra, rb = 2 * t_new, GROUP * t_new
    out_shape = [jax.ShapeDtypeStruct((s, H_A // 2, ra, LANES), F32), jax.ShapeDtypeStruct((s, rb, LANES), F32)]
    out_specs = [pl.BlockSpec((1, H_A // 2, ra, LANES), lambda i, k: (i, 0, 0, 0)),
                 pl.BlockSpec((1, rb, LANES), lambda i, k: (i, 0, 0))]
    scratch = [
        pltpu.VMEM((H_A // 2, ra, LANES), F32), pltpu.VMEM((H_A // 2, ra, LANES), F32), pltpu.VMEM((H_A // 2, ra, LANES), F32),
        pltpu.VMEM((H_KV, rb, LANES), F32), pltpu.VMEM((H_KV, rb, LANES), F32), pltpu.VMEM((H_KV, rb, LANES), F32),
        pltpu.VMEM((H_A // 2, ra, nb), F32), pltpu.VMEM((H_KV, rb, nsp), F32),
        pltpu.VMEM((H_KV, rb, LANES), F32), pltpu.VMEM((H_KV, rb, LANES), F32),
    ]
    oa8, ob16 = pl.pallas_call(
        functools.partial(_attn_sample_kernel, past=past, t_new=t_new, nb=nb,
                          k_a=min(MOBA_TOPK, lp // MOBA_BLOCK), k_s=min(SLC_TOPK, n_s)),
        grid=(s, past // r_tile), in_specs=in_specs, out_specs=out_specs, out_shape=out_shape,
        scratch_shapes=scratch, compiler_params=_cparams("parallel", "arbitrary"),
    )(*ins)
    oa = jnp.where(lo, oa8[:, :, :t_new], oa8[:, :, t_new:]).transpose(0, 2, 1, 3).reshape(s * t_new, W_A)
    ob = ob16.reshape(s, GROUP, t_new, LANES).transpose(0, 2, 1, 3).reshape(s * t_new, W_B)
    return oa.astype(BF16), ob.astype(BF16)


def _post_kernel(oa_ref, ob_ref, gm_ref, x_ref, wba_ref, wbb_ref, wout_ref, gnf_ref, wg_ref, wu_ref, wd_ref,
                 y_ref, h_sc, hn_sc, acc_sc):
    j = pl.program_id(1)

    @pl.when(j == 0)
    def _():
        ma = _dot(oa_ref[...], wba_ref[...])
        mb = _dot(ob_ref[...], wbb_ref[...])
        merged = gm_ref[:, :D_MODEL] * ma + gm_ref[:, D_MODEL:] * mb
        h = x_ref[...] + _dot(merged.astype(BF16), wout_ref[...])
        h_sc[...] = h
        r = lax.rsqrt(jnp.mean(h * h, axis=-1, keepdims=True) + RMS_EPS)
        hn_sc[...] = ((h * r) * gnf_ref[...]).astype(BF16)
        acc_sc[...] = jnp.zeros(acc_sc.shape, F32)

    hn = hn_sc[...]
    g = _dot(hn, wg_ref[...])
    u = _dot(hn, wu_ref[...])
    act = (g * jax.nn.sigmoid(g)) * u
    acc_sc[...] += _dot(act.astype(BF16), wd_ref[...])

    @pl.when(j == pl.num_programs(1) - 1)
    def _():
        y_ref[...] = h_sc[...] + acc_sc[...]


def _post(oa, ob, gm, x, wts):
    n = x.shape[0]
    tm = 512 if n % 512 == 0 else n
    d_ff = wts["wd"].shape[0]
    n_ff = 2 if (d_ff // 2) % LANES == 0 else 1
    fc = d_ff // n_ff
    rows = lambda w: pl.BlockSpec((tm, w), lambda i, j: (i, 0))
    full = lambda a: pl.BlockSpec(a.shape, lambda i, j: (0,) * a.ndim)
    ins = [oa, ob, gm, x, wts["wba"], wts["wbb"], wts["wout"], wts["gnf"], wts["wgate"], wts["wup"], wts["wd"]]
    in_specs = [rows(W_A), rows(W_B), rows(2 * D_MODEL), rows(D_MODEL)] + [full(a) for a in ins[4:8]] + [
        pl.BlockSpec((D_MODEL, fc), lambda i, j: (0, j)),
        pl.BlockSpec((D_MODEL, fc), lambda i, j: (0, j)),
        pl.BlockSpec((fc, D_MODEL), lambda i, j: (j, 0)),
    ]
    return pl.pallas_call(
        _post_kernel, grid=(n // tm, n_ff), in_specs=in_specs, out_specs=rows(D_MODEL),
        out_shape=jax.ShapeDtypeStruct((n, D_MODEL), F32),
        scratch_shapes=[pltpu.VMEM((tm, D_MODEL), F32), pltpu.VMEM((tm, D_MODEL), BF16), pltpu.VMEM((tm, D_MODEL), F32)],
        compiler_params=_cparams("parallel", "arbitrary"),
    )(*ins)


def _blockdiag2(w):
    z = jnp.zeros_like(w)
    return jnp.concatenate([jnp.concatenate([w, z], axis=-1), jnp.concatenate([z, w], axis=-1)], axis=-2)


def _prep_weights(g_na, w_in, b_in, g_qk_a, g_qk_b, cmp_pos, cmp_w1, cmp_w2, w_ba, w_bb, w_out, g_nf, w_up, w_down):
    def perm_qb_cols(a):
        lead = a.shape[:-1]
        return a.reshape(lead + (H_KV, GROUP, HEAD_DIM)).swapaxes(-3, -2).reshape(lead + (W_B,))

    def main_cols(a):
        return jnp.concatenate([a[..., :C_QB], perm_qb_cols(a[..., C_QB:C_KVB]), a[..., C_KVB:C_END]], axis=-1)

    ones = lambda n: jnp.ones((n,), F32)
    gv = jnp.concatenate([
        jnp.tile(g_qk_a[0], H_A), jnp.tile(g_qk_a[1], H_A), ones(W_A), jnp.tile(g_qk_b[0], H_B),
        ones(2 * W_KV), jnp.tile(g_qk_b[2], H_KV), ones(W_KV), jnp.tile(g_qk_b[3], H_KV), ones(W_KV)])
    n_gb = 3 * H_B
    hd_idx = np.arange(W_A) // HEAD_DIM
    bd = jnp.asarray((hd_idx[:, None] == hd_idx[None, :]).astype(np.float32)).astype(BF16)
    d_ff = w_down.shape[0]
    w1 = cmp_w1.astype(BF16)
    pos2 = jnp.concatenate([cmp_pos, cmp_pos], axis=-1)
    return {
        "gna": g_na.reshape(1, D_MODEL),
        "w1": main_cols(w_in).astype(BF16), "b1": main_cols(b_in).reshape(1, C_END), "gv": gv.reshape(1, C_END),
        "wg": jnp.pad(w_in[:, C_END:C_END + n_gb], ((0, 0), (0, LANES - n_gb))).astype(BF16),
        "bg": jnp.pad(b_in[C_END:C_END + n_gb], (0, LANES - n_gb)).reshape(1, LANES),
        "wgm": w_in[:, C_END + n_gb:].astype(BF16), "bgm": b_in[C_END + n_gb:].reshape(1, 2 * D_MODEL),
        "bd": bd,
        "cpos": pos2.reshape(2, 2, CMP_STRIDE, 1, LANES).swapaxes(0, 1),
        "cwa": _blockdiag2(w1[:, :CMP_STRIDE]), "cwb": _blockdiag2(w1[:, CMP_STRIDE:]),
        "cw2": _blockdiag2(cmp_w2.astype(BF16)),
        "gkc": jnp.tile(g_qk_b[1], H_KV).reshape(1, LANES),
        "wba": w_ba.astype(BF16),
        "wbb": w_bb.reshape(H_KV, GROUP, HEAD_DIM, D_MODEL).swapaxes(0, 1).reshape(W_B, D_MODEL).astype(BF16),
        "wout": w_out.astype(BF16), "gnf": g_nf.reshape(1, D_MODEL),
        "wgate": w_up[:, :d_ff].astype(BF16), "wup": w_up[:, d_ff:].astype(BF16), "wd": w_down.astype(BF16),
    }


def _prompt_layer(x, wts):
    b, t, _ = x.shape
    x2 = x.reshape(b * t, D_MODEL)
    p = _inproj(x2, wts, with_kmean=True)
    nc = t // CMP_STRIDE
    kc, vc = _compress(p["kvn"].reshape(b, nc, CMP_STRIDE * 4 * W_KV), 4 * W_KV, wts, tc=nc)
    oa, ob = _attn_prompt(p, kc, vc, b, t)
    y = _post(oa, ob, p["gm"], x2, wts)
    keep = min(WINDOW, t)
    return (y.reshape(b, t, D_MODEL),
            p["kva"].reshape(b, t, 2, H_A, HEAD_DIM),
            p["kvn"].reshape(b, t, 4, H_KV, HEAD_DIM),
            p["kvw"].reshape(b, t, 2, H_KV, HEAD_DIM)[:, t - keep:])


def _sample_layer(x, cache_moba, cache_nsa, state_win, page_table, wts):
    s, t_new, _ = x.shape
    past = page_table.shape[1] * PAGE_SIZE
    x2 = x.reshape(s * t_new, D_MODEL)
    p = _inproj(x2, wts, with_kmean=False)
    asm = _assemble(page_table, cache_moba, cache_nsa)
    nc = past // CMP_STRIDE
    kc, vc = _compress(asm[2].reshape(s, nc, CMP_STRIDE * 2 * W_KV), 2 * W_KV, wts, tc=min(nc, 256))
    oa, ob = _attn_sample(p, asm, kc, vc, state_win, s, t_new, past)
    y = _post(oa, ob, p["gm"], x2, wts)
    kvw_new = p["kvw"].reshape(s, t_new, 2, H_KV, HEAD_DIM)
    win = jnp.concatenate([state_win, kvw_new], axis=1)[:, t_new:]
    return (y.reshape(s, t_new, D_MODEL),
            p["kva"].reshape(s, t_new, 2, H_A, HEAD_DIM),
            p["kvn"].reshape(s, t_new, 4, H_KV, HEAD_DIM),
            win)


def kernel(x_prompt, x_sample, cache_moba_kv, cache_nsa_kv, state_win_kv, page_table, g_norm_attn, w_in, b_in,
           g_qk_moba, g_qk_nsa, cmp_pos, cmp_w1, cmp_w2, w_br_moba, w_br_nsa, w_out, g_norm_ffn, w_up, w_down):
    assert g_norm_attn.shape[0] == 1, "single-layer step"
    wts = _prep_weights(g_norm_attn[0], w_in[0], b_in[0], g_qk_moba[0], g_qk_nsa[0], cmp_pos[0], cmp_w1[0],
                        cmp_w2[0], w_br_moba[0], w_br_nsa[0], w_out[0], g_norm_ffn[0], w_up[0], w_down[0])
    y_p, a_p, n_p, wn_p = _prompt_layer(x_prompt, wts)
    y_s, a_s, n_s, wn_s = _sample_layer(x_sample, cache_moba_kv[0], cache_nsa_kv[0], state_win_kv[0], page_table, wts)
    return (y_p, y_s, a_p[None], n_p[None], wn_p[None], a_s[None], n_s[None], wn_s[None])
```

```python
import functools
import math

import numpy as np
import jax
import jax.numpy as jnp
from jax import lax
from jax.experimental import pallas as pl
from jax.experimental.pallas import tpu as pltpu

F32 = jnp.float32
BF16 = jnp.bfloat16

D_MODEL = 1024
PAGE_SIZE = 128
HEAD_DIM = 64
H_A = 8
H_B = 8
H_KV = 2
GROUP = H_B // H_KV
MOBA_BLOCK = 256
MOBA_TOPK = 3
CMP_LEN = 32
CMP_STRIDE = 16
SLC_BLOCK = 64
SLC_TOPK = 16
WINDOW = 512
CMP_HID = 2 * HEAD_DIM
FORCE_BONUS = 1e4
RMS_EPS = 1e-6
W_A = H_A * HEAD_DIM
W_B = H_B * HEAD_DIM
W_KV = H_KV * HEAD_DIM
QK_SCALE = HEAD_DIM ** -0.5

LANES = 128
VMEM_LIMIT = 56 * 1024 * 1024
NEG_INF = float("-inf")

SLOPES_A = tuple(2.0 ** (-8.0 * (i + 1) / H_A) for i in range(H_A))
SLOPES_B = tuple(2.0 ** (-8.0 * (i + 1) / H_B) for i in range(H_B))


def _round_up(x, m):
    return -(-x // m) * m


def _cparams(*sem):
    return pltpu.CompilerParams(dimension_semantics=sem, vmem_limit_bytes=VMEM_LIMIT)


def _dot(a, b):
    return jnp.dot(a, b, preferred_element_type=F32)


def _dot_t(a, b):
    return lax.dot_general(a, b, (((1,), (1,)), ((), ())), preferred_element_type=F32)


def _head_sumsq(z, bd):
    zz = z * z
    hi = zz.astype(BF16)
    lo = (zz - hi.astype(F32)).astype(BF16)
    return _dot(hi, bd) + _dot(lo, bd)


def _head_norm(z, g, bd):
    ss = _head_sumsq(z, bd)
    return (z * lax.rsqrt(ss * (1.0 / HEAD_DIM) + RMS_EPS)) * g


C_QA, C_KA, C_VA, C_QB, C_KVB, C_END = 0, 512, 1024, 1536, 2048, 2816


def _inproj_kernel(x_ref, gna_ref, w1_ref, b1_ref, gv_ref, wg_ref, bg_ref, wgm_ref, bgm_ref, bd_ref,
                   qa_ref, kva_ref, kab_ref, vab_ref, qb_ref, kvn_ref, cmp_ref, ksb_ref, vsb_ref, kvw_ref,
                   kwb_ref, vwb_ref, gb_ref, gm_ref, *maybe_km, n_blk):
    x = x_ref[...]
    r = lax.rsqrt(jnp.mean(x * x, axis=-1, keepdims=True) + RMS_EPS)
    xb = ((x * r) * gna_ref[...]).astype(BF16)
    bd = bd_ref[...]
    bd1 = bd_ref[:LANES, :LANES]

    def sec(a, b):
        return _dot(xb, w1_ref[:, a:b]) + b1_ref[:, a:b]

    qa = _head_norm(sec(C_QA, C_KA), gv_ref[:, C_QA:C_KA], bd)
    qa_ref[...] = (qa * QK_SCALE).astype(BF16)

    ka = _head_norm(sec(C_KA, C_VA), gv_ref[:, C_KA:C_VA], bd)
    va = sec(C_VA, C_QB)
    for h in range(H_A):
        kva_ref[:, 0, h, :] = ka[:, h * HEAD_DIM:(h + 1) * HEAD_DIM]
        kva_ref[:, 1, h, :] = va[:, h * HEAD_DIM:(h + 1) * HEAD_DIM]
    kab_ref[...] = ka.astype(BF16)
    vab_ref[...] = va.astype(BF16)
    if n_blk:
        km_ref = maybe_km[0]
        km_ref[0] = jnp.mean(ka.reshape(n_blk, MOBA_BLOCK, W_A), axis=1)

    qb = _head_norm(sec(C_QB, C_KVB), gv_ref[:, C_QB:C_KVB], bd)
    qb_ref[...] = (qb * QK_SCALE).astype(BF16)

    kvb = sec(C_KVB, C_END)
    ks = _head_norm(kvb[:, 256:384], gv_ref[:, C_KVB + 256:C_KVB + 384], bd1)
    vs = kvb[:, 384:512]
    kw = _head_norm(kvb[:, 512:640], gv_ref[:, C_KVB + 512:C_KVB + 640], bd1)
    vw = kvb[:, 640:768]
    cmp_ref[...] = kvb[:, 0:256]
    for h in range(H_KV):
        lanes = slice(h * HEAD_DIM, (h + 1) * HEAD_DIM)
        kvn_ref[:, 0, h, :] = kvb[:, 0:128][:, lanes]
        kvn_ref[:, 1, h, :] = kvb[:, 128:256][:, lanes]
        kvn_ref[:, 2, h, :] = ks[:, lanes]
        kvn_ref[:, 3, h, :] = vs[:, lanes]
        kvw_ref[:, 0, h, :] = kw[:, lanes]
        kvw_ref[:, 1, h, :] = vw[:, lanes]
    ksb_ref[...] = ks.astype(BF16)
    vsb_ref[...] = vs.astype(BF16)
    kwb_ref[...] = kw.astype(BF16)
    vwb_ref[...] = vw.astype(BF16)

    gb_ref[...] = jax.nn.sigmoid(_dot(xb, wg_ref[...]) + bg_ref[...])
    gm_ref[...] = jax.nn.sigmoid(_dot(xb, wgm_ref[...]) + bgm_ref[...])


def _inproj(x, wts, with_kmean):
    n = x.shape[0]
    tm = MOBA_BLOCK if n % MOBA_BLOCK == 0 else n
    n_blk = tm // MOBA_BLOCK if with_kmean else 0
    grid = (n // tm,)

    def rows(w):
        if isinstance(w, tuple):
            return pl.BlockSpec((tm,) + w, lambda i: (i,) + (0,) * len(w))
        return pl.BlockSpec((tm, w), lambda i: (i, 0))

    def full(a):
        return pl.BlockSpec(a.shape, lambda i: (0,) * a.ndim)

    ins = [x, wts["gna"], wts["w1"], wts["b1"], wts["gv"], wts["wg"], wts["bg"], wts["wgm"], wts["bgm"], wts["bd"]]
    in_specs = [rows(D_MODEL)] + [full(a) for a in ins[1:]]
    outs = [
        (W_A, BF16), ((2, H_A, HEAD_DIM), F32), (W_A, BF16), (W_A, BF16), (W_B, BF16), ((4, H_KV, HEAD_DIM), F32),
        (2 * W_KV, F32), (W_KV, BF16), (W_KV, BF16), ((2, H_KV, HEAD_DIM), F32), (W_KV, BF16), (W_KV, BF16),
        (LANES, F32), (2 * D_MODEL, F32),
    ]
    out_shape = [jax.ShapeDtypeStruct((n,) + (w if isinstance(w, tuple) else (w,)), dt) for w, dt in outs]
    out_specs = [rows(w) for w, _ in outs]
    if n_blk:
        out_shape.append(jax.ShapeDtypeStruct((n // tm, n_blk, W_A), F32))
        out_specs.append(pl.BlockSpec((1, n_blk, W_A), lambda i: (i, 0, 0)))
    res = pl.pallas_call(
        functools.partial(_inproj_kernel, n_blk=n_blk),
        grid=grid, in_specs=in_specs, out_specs=out_specs, out_shape=out_shape,
        compiler_params=_cparams("parallel"),
    )(*ins)
    names = ["qa", "kva", "kab", "vab", "qb", "kvn", "cmp", "ksb", "vsb", "kvw", "kwb", "vwb", "gb", "gm"]
    out = dict(zip(names, res))
    if n_blk:
        out["kmean"] = res[-1].reshape(n // MOBA_BLOCK, W_A)
    return out


def _compress_kernel(x_ref, xh_ref, pos_ref, wa_ref, wb_ref, w2_ref, gk_ref, bd_ref, kc_ref, vc_ref, *, row_w, tc):
    bd1 = bd_ref[:LANES, :LANES]

    def half_proj(xr, w_ref, half):
        outs = []
        for br in range(2):
            acc = None
            for l in range(CMP_STRIDE):
                a = l * row_w + br * LANES
                xl = (xr[0, :, a:a + LANES] + pos_ref[half, br, l]).astype(BF16)
                t = _dot(xl, w_ref[br, l])
                acc = t if acc is None else acc + t
            outs.append(acc)
        return outs

    a_k, a_v = half_proj(x_ref, wa_ref, 0)
    b_k, b_v = half_proj(x_ref, wb_ref, 1)
    bh_k, bh_v = half_proj(xh_ref, wb_ref, 1)
    rows = lax.broadcasted_iota(jnp.int32, (tc, 2 * CMP_HID), 0)

    def shift_up(b, bh):
        rolled = pltpu.roll(b, tc - 1, 0)
        return jnp.where(rows == tc - 1, jnp.broadcast_to(bh[0:1], b.shape), rolled)

    hid_k = a_k + shift_up(b_k, bh_k)
    hid_v = a_v + shift_up(b_v, bh_v)
    hid_k = hid_k * jax.nn.sigmoid(hid_k)
    hid_v = hid_v * jax.nn.sigmoid(hid_v)
    out_k = _dot(hid_k.astype(BF16), w2_ref[0])
    out_v = _dot(hid_v.astype(BF16), w2_ref[1])
    kc_ref[0] = _head_norm(out_k, gk_ref[...], bd1).astype(BF16)
    vc_ref[0] = out_v.astype(BF16)


def _compress(rows3, row_w, wts, tc):
    b, nc, cw = rows3.shape
    nt = nc // tc
    last_h = nc // 8 - 1
    full = lambda a: pl.BlockSpec(a.shape, lambda i, j: (0,) * a.ndim)
    ins = [rows3, rows3, wts["cpos"], wts["cwa"], wts["cwb"], wts["cw2"], wts["gkc"], wts["bd"]]
    in_specs = [
        pl.BlockSpec((1, tc, cw), lambda i, j: (i, j, 0)),
        pl.BlockSpec((1, 8, cw), lambda i, j: (i, jnp.minimum((j + 1) * (tc // 8), last_h), 0)),
    ] + [full(a) for a in ins[2:]]
    out_shape = [jax.ShapeDtypeStruct((b, nc, LANES), BF16)] * 2
    out_specs = [pl.BlockSpec((1, tc, LANES), lambda i, j: (i, j, 0))] * 2
    return pl.pallas_call(
        functools.partial(_compress_kernel, row_w=row_w, tc=tc),
        grid=(b, nt), in_specs=in_specs, out_specs=out_specs, out_shape=out_shape,
        compiler_params=_cparams("parallel", "arbitrary"),
    )(*ins)


def _flash_init(m_ref, l_ref, acc_ref):
    m_ref[...] = jnp.full(m_ref.shape, NEG_INF, F32)
    l_ref[...] = jnp.zeros(l_ref.shape, F32)
    acc_ref[...] = jnp.zeros(acc_ref.shape, F32)


def _flash_step(s, valid, v, m_ref, l_ref, acc_ref, i):
    s = jnp.where(valid, s, NEG_INF)
    m_old = m_ref[i]
    m_new = jnp.maximum(m_old, jnp.max(s, axis=1, keepdims=True))
    m_safe = jnp.where(m_new == NEG_INF, 0.0, m_new)
    alpha = jnp.exp(m_old - m_safe)
    p = jnp.exp(s - m_safe[:, :1])
    l_ref[i] = alpha * l_ref[i] + jnp.sum(p, axis=1, keepdims=True)
    acc_ref[i] = alpha * acc_ref[i] + _dot(p.astype(BF16), v)
    m_ref[i] = m_new


def _flash_out(l_ref, acc_ref, i):
    l = l_ref[i]
    return acc_ref[i] / jnp.where(l > 0, l, 1.0)


def _topk_mask(v, idx_iota, k):
    width = v.shape[1]

    def body(_, carry):
        v, sel = carry
        mx = jnp.max(v, axis=1, keepdims=True)
        idx = jnp.min(jnp.where(v == mx, idx_iota, width), axis=1, keepdims=True)
        hit = idx_iota == idx
        sel = jnp.where(hit & (mx > NEG_INF), 1.0, sel)
        return jnp.where(hit, NEG_INF, v), sel

    return lax.fori_loop(0, k, body, (v, jnp.zeros(v.shape, F32)))[1]


def _cmp_to_block_matrix(ncp, nsp, n_cmp, n_s):
    m = np.zeros((ncp, nsp), np.float32)
    per = SLC_BLOCK // CMP_STRIDE
    for i in range(n_s * per):
        for t in (i, i - 1):
            if 0 <= t < n_cmp:
                m[t, i // per] += 1.0
    return jnp.asarray(m)


MASK_BIG = 2.0 ** 30
AUX_R = LANES - 1


def _key_aux_table(t, block):
    assert t // block <= AUX_R and block <= 256
    key = np.arange(t)
    tab = np.zeros((t, LANES), np.float32)
    tab[key, key // block] = 1.0
    tab[:, AUX_R] = key % block
    return jnp.asarray(tab, BF16)


KEY_CHUNK = 64


def _flash_t_step(s_ref, p_ref, mask, v_t, m_ref, l_ref, acc_ref, i):
    n_keys, n_q = s_ref.shape
    chunks = [slice(r, r + KEY_CHUNK) for r in range(0, n_keys, KEY_CHUNK)]

    def scores(rows):
        s = s_ref[rows, :]
        if mask is not None:
            s = jnp.where(mask[rows, :], s, -MASK_BIG)
        return s.reshape(KEY_CHUNK // 8, 8, n_q)

    m_old = m_ref[i]
    m8 = None
    for rows in chunks:
        cm = jnp.max(scores(rows), axis=0)
        m8 = cm if m8 is None else jnp.maximum(m8, cm)
    m_new = jnp.maximum(m_old, jnp.max(m8, axis=0, keepdims=True))
    alpha = jnp.exp(m_old - m_new)
    l8 = None
    for rows in chunks:
        p = jnp.exp(scores(rows) - m_new[None])
        ps = jnp.sum(p, axis=0)
        l8 = ps if l8 is None else l8 + ps
        p_ref[rows, :] = p.reshape(KEY_CHUNK, n_q).astype(BF16)
    l_ref[i] = alpha * l_ref[i] + l8
    acc_ref[i] = alpha[0:1] * acc_ref[i] + _dot(v_t, p_ref[...])
    m_ref[i] = m_new


def _flash_t_out(l_ref, acc_ref, i):
    l = jnp.sum(l_ref[i], axis=0, keepdims=True)
    return acc_ref[i] / jnp.where(l > 0, l, 1.0)


def _topk_mask_t(v, idx, k):
    n = v.shape[0]

    def body(_, carry):
        v, sel = carry
        mx = jnp.max(v, axis=0, keepdims=True)
        first = jnp.min(jnp.where(v == mx, idx, n), axis=0, keepdims=True)
        hit = idx == first
        sel = jnp.where(hit & (mx > NEG_INF), 1.0, sel)
        return jnp.where(hit, NEG_INF, v), sel

    return lax.fori_loop(0, k, body, (v, jnp.zeros(v.shape, F32)))[1]


def _attn_prompt_kernel(qa_ref, qb_ref, gb_ref, km_ref, ka_ref, va_ref, kc_ref, vc_ref, ks_ref, vs_ref,
                        kw_ref, vw_ref, auxa_ref, auxs_ref, mm_ref, oa_ref, ob_ref,
                        m_sc, l_sc, acc_sc, oc_sc, os_sc, qx_sc, s_sc, p_sc, *, n_cmp, k_a, k_s):
    c = pl.program_id(1)
    tq = MOBA_BLOCK
    sub = lax.broadcasted_iota(jnp.int32, (LANES, tq), 0)
    is_lo = sub < HEAD_DIM
    is_r = sub == AUX_R
    key_r = lax.broadcasted_iota(jnp.int32, (tq, tq), 0)
    qry_r = lax.broadcasted_iota(jnp.int32, (tq, tq), 1)
    causal = key_r <= qry_r

    def masked_q(ref, tile, half):
        q = ref[0, tile * LANES:(tile + 1) * LANES, :]
        keep = is_lo if half == 0 else jnp.logical_not(is_lo)
        return jnp.where(keep, q, jnp.zeros_like(q))

    def set_query(i, q_t, aux):
        qx_sc[i, :LANES, :] = q_t
        qx_sc[i, LANES:, :] = aux.astype(BF16)

    def flash_init():
        m_sc[...] = jnp.full(m_sc.shape, NEG_INF, F32)
        l_sc[...] = jnp.zeros(l_sc.shape, F32)
        acc_sc[...] = jnp.zeros(acc_sc.shape, F32)

    def sweep(k_ref, k_lanes, aux_ref, v_ref, v_rows, heads, n, mask):
        koff = pl.multiple_of(n * tq, tq)
        kaux = aux_ref[pl.ds(koff, tq), :]
        n_buf = s_sc.shape[0]

        def stage_scores(i):
            kcat = jnp.concatenate([k_ref[0, pl.ds(koff, tq), k_lanes(i)], kaux], axis=1)
            s_sc[i % n_buf] = _dot(kcat, qx_sc[i])

        for i in heads[:n_buf - 1]:
            stage_scores(i)
        for pos, i in enumerate(heads):
            if pos + n_buf - 1 < len(heads):
                stage_scores(heads[pos + n_buf - 1])
            _flash_t_step(s_sc.at[i % n_buf], p_sc.at[i % 2], mask, v_ref[0, n, v_rows(i), :], m_sc, l_sc, acc_sc, i)

    def past_loop(block_fn):
        def body(n, carry):
            block_fn(n)
            return carry
        lax.fori_loop(0, c, body, 0)

    blk = sub
    back = (c - blk).astype(F32)
    for h in range(H_A):
        j, half = divmod(h, 2)
        q_t = masked_q(qa_ref, j, half)
        gate = _dot(km_ref[0, :, j * LANES:(j + 1) * LANES].astype(BF16), q_t)
        sel = _topk_mask_t(jnp.where(blk < c, gate, NEG_INF), blk, k_a)
        bias = jnp.where(sel > 0.5, 0.0, -MASK_BIG) - (SLOPES_A[h] * tq) * back
        set_query(h, q_t, jnp.where(is_r, SLOPES_A[h], jnp.where(blk < c, bias, 0.0)))

    pair_lanes = lambda i: slice((i // 2) * LANES, (i // 2 + 1) * LANES)
    all_lanes = lambda i: slice(0, LANES)
    heads = range(H_A)
    flash_init()
    sweep(ka_ref, pair_lanes, auxa_ref, va_ref, pair_lanes, heads, c, causal)
    past_loop(lambda n: sweep(ka_ref, pair_lanes, auxa_ref, va_ref, pair_lanes, heads, n, None))
    for j in range(H_A // 2):
        o = jnp.where(is_lo, _flash_t_out(l_sc, acc_sc, 2 * j), _flash_t_out(l_sc, acc_sc, 2 * j + 1))
        oa_ref[0, j * LANES:(j + 1) * LANES, :] = o.astype(BF16)

    ncp = kc_ref.shape[1]
    nsp = mm_ref.shape[0]
    jc = lax.broadcasted_iota(jnp.int32, (ncp, tq), 0)
    d_c = c * tq + lax.broadcasted_iota(jnp.int32, (ncp, tq), 1) - (jc * CMP_STRIDE + (CMP_LEN - 1))
    c_valid = (d_c >= 0) & (jc < n_cmp)
    d_cf = d_c.astype(F32)
    sblk = lax.broadcasted_iota(jnp.int32, (nsp, tq), 0)
    cur_s = (c * tq + lax.broadcasted_iota(jnp.int32, (nsp, tq), 1)) // SLC_BLOCK
    forced = (sblk == 0) | (sblk >= cur_s - 1)
    causal_s = sblk <= cur_s
    rel_s = (sblk - c * (tq // SLC_BLOCK)).astype(F32)
    for k in range(H_KV):
        imp = jnp.zeros((ncp, tq), F32)
        for g in range(GROUP):
            i = k * GROUP + g
            s = _dot(kc_ref[0], masked_q(qb_ref, g, k)) - SLOPES_B[i] * d_cf
            s = jnp.where(c_valid, s, NEG_INF)
            m = jnp.max(s, axis=0, keepdims=True)
            m = jnp.where(m == NEG_INF, 0.0, m)
            e = jnp.where(c_valid, jnp.exp(s - m), 0.0)
            d = jnp.sum(e, axis=0, keepdims=True)
            p = e / jnp.where(d > 0, d, 1.0)
            imp = imp + p
            oc_sc[i] = _dot(vc_ref[0], p.astype(BF16))
        impb = jnp.dot(mm_ref[...], imp, preferred_element_type=F32, precision=lax.Precision.HIGHEST)
        impb = jnp.where(forced, impb + FORCE_BONUS, impb)
        impb = jnp.where(causal_s, impb, NEG_INF)
        sel = _topk_mask_t(impb, sblk, k_s)
        for g in range(GROUP):
            i = k * GROUP + g
            bias = jnp.where(sel > 0.5, 0.0, -MASK_BIG) + (SLOPES_B[i] * SLC_BLOCK) * rel_s
            set_query(i, masked_q(qb_ref, g, k), jnp.where(is_r, SLOPES_B[i], bias))

    heads = range(H_B)
    flash_init()
    sweep(ks_ref, all_lanes, auxs_ref, vs_ref, all_lanes, heads, c, causal)
    past_loop(lambda n: sweep(ks_ref, all_lanes, auxs_ref, vs_ref, all_lanes, heads, n, None))
    for i in range(H_B):
        os_sc[i] = _flash_t_out(l_sc, acc_sc, i)

    for i in range(H_B):
        qx_sc[i, LANES:, :] = jnp.where(is_r, SLOPES_B[i], -(SLOPES_B[i] * tq) * back).astype(BF16)
    flash_init()
    sweep(kw_ref, all_lanes, auxa_ref, vw_ref, all_lanes, heads, c, causal)
    n_back = WINDOW // tq
    for b in range(1, n_back + 1):
        mask = None if b < n_back else key_r > qry_r
        pl.when(c >= b)(functools.partial(sweep, kw_ref, all_lanes, auxa_ref, vw_ref, all_lanes, heads, c - b, mask))

    gb = gb_ref[0]
    for j in range(GROUP):
        def comb(i):
            return (gb[3 * i:3 * i + 1] * oc_sc[i] + gb[3 * i + 1:3 * i + 2] * os_sc[i]
                    + gb[3 * i + 2:3 * i + 3] * _flash_t_out(l_sc, acc_sc, i))
        ob_ref[0, j * LANES:(j + 1) * LANES, :] = jnp.where(is_lo, comb(j), comb(GROUP + j)).astype(BF16)


def _attn_prompt(p, kc, vc, b, t):
    tq = MOBA_BLOCK
    nb = t // tq
    n_s = t // SLC_BLOCK
    n_cmp = t // CMP_STRIDE - 1
    nsp = _round_up(n_s, LANES)
    ncp = kc.shape[1]
    assert nb <= AUX_R and n_s <= AUX_R and nsp == LANES and WINDOW % tq == 0
    km = jnp.pad(p["kmean"].reshape(b, nb, W_A), ((0, 0), (0, LANES - nb), (0, 0)))
    mm = _cmp_to_block_matrix(ncp, nsp, n_cmp, n_s).T
    r3 = lambda a: a.reshape(b, t, a.shape[-1])
    tr = lambda a: r3(a).transpose(0, 2, 1)
    qtile = lambda w: pl.BlockSpec((1, w, tq), lambda i, c: (i, 0, c))
    trb = lambda a: a.reshape(b, nb, tq, a.shape[-1]).transpose(0, 1, 3, 2)
    seq = lambda a: pl.BlockSpec((1,) + a.shape[1:], lambda i, c: (i,) + (0,) * (a.ndim - 1))
    const = lambda a: pl.BlockSpec(a.shape, lambda i, c: (0, 0))
    ins = [tr(p["qa"]), tr(p["qb"]), tr(p["gb"]), km, r3(p["kab"]), trb(p["vab"]), kc, vc.transpose(0, 2, 1),
           r3(p["ksb"]), trb(p["vsb"]), r3(p["kwb"]), trb(p["vwb"]),
           _key_aux_table(t, MOBA_BLOCK), _key_aux_table(t, SLC_BLOCK), mm]
    in_specs = [qtile(W_A), qtile(W_B), qtile(LANES)] + [seq(a) for a in ins[3:12]] + [const(a) for a in ins[12:]]
    out_shape = [jax.ShapeDtypeStruct((b, W_A, t), BF16), jax.ShapeDtypeStruct((b, W_B, t), BF16)]
    out_specs = [qtile(W_A), qtile(W_B)]
    scratch = [
        pltpu.VMEM((H_A, 8, tq), F32), pltpu.VMEM((H_A, 8, tq), F32), pltpu.VMEM((H_A, LANES, tq), F32),
        pltpu.VMEM((H_B, LANES, tq), F32), pltpu.VMEM((H_B, LANES, tq), F32), pltpu.VMEM((H_A, 2 * LANES, tq), BF16),
        pltpu.VMEM((6, tq, tq), F32), pltpu.VMEM((2, tq, tq), BF16),
    ]
    oa, ob = pl.pallas_call(
        functools.partial(_attn_prompt_kernel, n_cmp=n_cmp, k_a=min(MOBA_TOPK, nb), k_s=min(SLC_TOPK, n_s)),
        grid=(b, nb), in_specs=in_specs, out_specs=out_specs, out_shape=out_shape, scratch_shapes=scratch,
        compiler_params=_cparams("parallel", "arbitrary"),
    )(*ins)
    return oa.transpose(0, 2, 1).reshape(b * t, W_A), ob.transpose(0, 2, 1).reshape(b * t, W_B)


def _assemble_kernel(pt_ref, moba_ref, nsa_ref, ka_ref, va_ref, cmp_ref, ks_ref, vs_ref, ksum_ref):
    del pt_ref
    m = moba_ref[0]
    k = m[:, :W_A]
    ka_ref[0] = k.astype(BF16)
    va_ref[0] = m[:, W_A:].astype(BF16)
    ksum_ref[0, 0] = jnp.sum(k, axis=0, keepdims=True)
    n = nsa_ref[0]
    cmp_ref[0] = n[:, :2 * W_KV]
    ks_ref[0] = n[:, 2 * W_KV:3 * W_KV].astype(BF16)
    vs_ref[0] = n[:, 3 * W_KV:].astype(BF16)


def _assemble(page_table, cache_moba, cache_nsa):
    s, n_pages = page_table.shape
    past = n_pages * PAGE_SIZE
    cache_moba = cache_moba.reshape(cache_moba.shape[0], PAGE_SIZE, 2 * W_A)
    cache_nsa = cache_nsa.reshape(cache_nsa.shape[0], PAGE_SIZE, 4 * W_KV)
    page = lambda a: pl.BlockSpec((1,) + a.shape[1:], lambda b, p, pt: (pt[b, p], 0, 0))
    dst = lambda w: pl.BlockSpec((1, PAGE_SIZE, w), lambda b, p, pt: (b, p, 0))
    outs = [(W_A, BF16), (W_A, BF16), (2 * W_KV, F32), (W_KV, BF16), (W_KV, BF16)]
    out_shape = [jax.ShapeDtypeStruct((s, past, w), dt) for w, dt in outs]
    out_shape.append(jax.ShapeDtypeStruct((s, n_pages, 1, W_A), F32))
    out_specs = [dst(w) for w, _ in outs] + [pl.BlockSpec((1, 1, 1, W_A), lambda b, p, pt: (b, p, 0, 0))]
    res = pl.pallas_call(
        _assemble_kernel,
        grid_spec=pltpu.PrefetchScalarGridSpec(
            num_scalar_prefetch=1, grid=(s, n_pages),
            in_specs=[page(cache_moba), page(cache_nsa)], out_specs=out_specs),
        out_shape=out_shape, compiler_params=_cparams("parallel", "arbitrary"),
    )(page_table, cache_moba, cache_nsa)
    pages_per_blk = MOBA_BLOCK // PAGE_SIZE
    return list(res[:5]) + [res[5].reshape(s, n_pages // pages_per_blk, pages_per_blk * W_A)]


SAMPLE_KV_TILE = 2048


def _attn_sample_kernel(qa8_ref, qb16_ref, g16_ref, ksum_ref, kc_ref, vc_ref, mm_ref, ka_ref, va_ref, ks_ref, vs_ref,
                        kan_ref, van_ref, ksn_ref, vsn_ref, kwa_ref, vwa_ref, oa_ref, ob_ref,
                        ma_sc, la_sc, acca_sc, mb_sc, lb_sc, accb_sc, sela_sc, sels_sc, oc_sc, os_sc,
                        *, past, t_new, nb, k_a, k_s):
    kt = pl.program_id(1)
    r_tile = ka_ref.shape[1]
    n_pair = H_A // 2
    ra, rb = 2 * t_new, GROUP * t_new
    row_a = lax.broadcasted_iota(jnp.int32, (ra, 1), 0)
    row_b = lax.broadcasted_iota(jnp.int32, (rb, 1), 0)
    t_a = past + row_a % t_new
    t_b = past + row_b % t_new
    slope_a = [jnp.where(row_a < t_new, SLOPES_A[2 * j], SLOPES_A[2 * j + 1]).astype(F32) for j in range(n_pair)]
    slope_b = []
    for k in range(H_KV):
        sl = jnp.full((rb, 1), SLOPES_B[k * GROUP + GROUP - 1], F32)
        for g in range(GROUP - 2, -1, -1):
            sl = jnp.where(row_b < (g + 1) * t_new, SLOPES_B[k * GROUP + g], sl)
        slope_b.append(sl)
    nb_iota = lax.broadcasted_iota(jnp.int32, (ra, nb), 1)
    nsp = mm_ref.shape[1]
    cur_s = past // SLC_BLOCK

    @pl.when(kt == 0)
    def _():
        _flash_init(ma_sc, la_sc, acca_sc)
        _flash_init(mb_sc, lb_sc, accb_sc)
        km = (ksum_ref[0, :, :W_A] + ksum_ref[0, :, W_A:]) * (1.0 / MOBA_BLOCK)
        for j in range(n_pair):
            gate = _dot_t(qa8_ref[0, j], km[:, j * LANES:(j + 1) * LANES].astype(BF16))
            sela_sc[j] = _topk_mask(gate, nb_iota, k_a)
        nc = kc_ref.shape[1]
        jc = lax.broadcasted_iota(jnp.int32, (rb, nc), 1)
        d_c = t_b - (jc * CMP_STRIDE + (CMP_LEN - 1))
        c_valid = d_c >= 0
        d_cf = d_c.astype(F32)
        sj = lax.broadcasted_iota(jnp.int32, (rb, nsp), 1)
        forced = (sj == 0) | (sj >= cur_s - 1)
        causal_s = sj <= cur_s
        gr = lax.broadcasted_iota(jnp.int32, (rb, rb), 0) % t_new
        gc = lax.broadcasted_iota(jnp.int32, (rb, rb), 1) % t_new
        group_sum = (gr == gc).astype(F32)
        for k in range(H_KV):
            s = _dot_t(qb16_ref[0, k], kc_ref[0]) - slope_b[k] * d_cf
            s = jnp.where(c_valid, s, NEG_INF)
            m = jnp.max(s, axis=1, keepdims=True)
            m = jnp.where(m == NEG_INF, 0.0, m)
            e = jnp.where(c_valid, jnp.exp(s - m), 0.0)
            d = jnp.sum(e, axis=1, keepdims=True)
            p = e / jnp.where(d > 0, d, 1.0)
            oc_sc[k] = _dot(p.astype(BF16), vc_ref[0])
            pb = jnp.dot(p, mm_ref[...], preferred_element_type=F32, precision=lax.Precision.HIGHEST)
            impb = jnp.dot(group_sum, pb, preferred_element_type=F32, precision=lax.Precision.HIGHEST)
            impb = jnp.where(forced, impb + FORCE_BONUS, impb)
            impb = jnp.where(causal_s, impb, NEG_INF)
            sels_sc[k] = _topk_mask(impb, sj, k_s)

    ea_row = lax.broadcasted_iota(jnp.int32, (nb, r_tile), 0)
    ea_col = lax.broadcasted_iota(jnp.int32, (nb, r_tile), 1) // MOBA_BLOCK
    expand_a = (ea_row == kt * (r_tile // MOBA_BLOCK) + ea_col).astype(BF16)
    col_a = lax.broadcasted_iota(jnp.int32, (ra, r_tile), 1)
    dist_a_t = (t_a - (kt * r_tile + col_a)).astype(F32)
    for j in range(n_pair):
        valid = _dot(sela_sc[j].astype(BF16), expand_a) > 0.5
        s = _dot_t(qa8_ref[0, j], ka_ref[0, :, j * LANES:(j + 1) * LANES]) - slope_a[j] * dist_a_t
        _flash_step(s, valid, va_ref[0, :, j * LANES:(j + 1) * LANES], ma_sc, la_sc, acca_sc, j)

    blocks_per_tile = r_tile // SLC_BLOCK
    first_blk = kt * blocks_per_tile
    lane_tile = pl.multiple_of((first_blk // LANES) * LANES, LANES)
    e_row = lax.broadcasted_iota(jnp.int32, (LANES, r_tile), 0)
    e_col = lax.broadcasted_iota(jnp.int32, (LANES, r_tile), 1) // SLC_BLOCK
    expand = (e_row == first_blk % LANES + e_col).astype(BF16)
    col_t = lax.broadcasted_iota(jnp.int32, (rb, r_tile), 1)
    dist_t = (t_b - (kt * r_tile + col_t)).astype(F32)
    for k in range(H_KV):
        sel_t = sels_sc[k, :, pl.ds(lane_tile, LANES)]
        valid = _dot(sel_t.astype(BF16), expand) > 0.5
        s = _dot_t(qb16_ref[0, k], ks_ref[0]) - slope_b[k] * dist_t
        _flash_step(s, valid, vs_ref[0], mb_sc, lb_sc, accb_sc, k)

    @pl.when(kt == pl.num_programs(1) - 1)
    def _():
        dist_a = row_a % t_new - lax.broadcasted_iota(jnp.int32, (ra, LANES), 1)
        for j in range(n_pair):
            s = _dot_t(qa8_ref[0, j], kan_ref[0, :, j * LANES:(j + 1) * LANES]) - slope_a[j] * dist_a.astype(F32)
            _flash_step(s, dist_a >= 0, van_ref[0, :, j * LANES:(j + 1) * LANES], ma_sc, la_sc, acca_sc, j)
            oa_ref[0, j] = _flash_out(la_sc, acca_sc, j)
        dist_b = row_b % t_new - lax.broadcasted_iota(jnp.int32, (rb, LANES), 1)
        sj = lax.broadcasted_iota(jnp.int32, (rb, nsp), 1)
        for k in range(H_KV):
            own = jnp.sum(jnp.where(sj == cur_s, sels_sc[k], 0.0), axis=1, keepdims=True) > 0.5
            s = _dot_t(qb16_ref[0, k], ksn_ref[0]) - slope_b[k] * dist_b.astype(F32)
            _flash_step(s, own & (dist_b >= 0), vsn_ref[0], mb_sc, lb_sc, accb_sc, k)
            os_sc[k] = _flash_out(lb_sc, accb_sc, k)
        _flash_init(mb_sc, lb_sc, accb_sc)
        n_w = kwa_ref.shape[1]
        dist_w = WINDOW + row_b % t_new - lax.broadcasted_iota(jnp.int32, (rb, n_w), 1)
        valid_w = (dist_w >= 0) & (dist_w < WINDOW)
        lane_b = lax.broadcasted_iota(jnp.int32, (rb, LANES), 1)
        comb = []
        for k in range(H_KV):
            s = _dot_t(qb16_ref[0, k], kwa_ref[0]) - slope_b[k] * dist_w.astype(F32)
            _flash_step(s, valid_w, vwa_ref[0], mb_sc, lb_sc, accb_sc, k)
            g = g16_ref[0, k]
            comb.append(g[:, 0:1] * oc_sc[k] + g[:, 1:2] * os_sc[k] + g[:, 2:3] * _flash_out(lb_sc, accb_sc, k))
        ob_ref[0] = jnp.where(lane_b < HEAD_DIM, comb[0], comb[1])


def _attn_sample(p, asm, kc, vc, state_win, s, t_new, past):
    r_tile = SAMPLE_KV_TILE
    assert past % r_tile == 0 and state_win.shape[1] == WINDOW and LANES % (r_tile // SLC_BLOCK) == 0
    ka, va, _, ks, vs, ksum = asm
    nb = past // MOBA_BLOCK
    lp = _round_up(past + t_new, MOBA_BLOCK)
    n_s = lp // SLC_BLOCK
    n_cmp = lp // CMP_STRIDE - 1
    nsp = _round_up(n_s, LANES)
    nc = kc.shape[1]
    mm = _cmp_to_block_matrix(nc, nsp, min(n_cmp, nc), n_s)
    lane = jnp.arange(LANES)
    lo = (lane < HEAD_DIM)

    qa = p["qa"].reshape(s, t_new, H_A // 2, LANES).transpose(0, 2, 1, 3)
    zero = jnp.zeros_like(qa)
    qa8 = jnp.concatenate([jnp.where(lo, qa, zero), jnp.where(lo, zero, qa)], axis=2)
    qb = p["qb"].reshape(s, t_new, GROUP, LANES).transpose(0, 2, 1, 3).reshape(s, 1, GROUP * t_new, LANES)
    zero = jnp.zeros_like(qb)
    qb16 = jnp.concatenate([jnp.where(lo, qb, zero), jnp.where(lo, zero, qb)], axis=1)
    g16 = p["gb"][:, :3 * H_B].reshape(s, t_new, H_KV, GROUP, 3).transpose(0, 2, 3, 1, 4)
    g16 = jnp.pad(g16.reshape(s, H_KV, GROUP * t_new, 3), ((0, 0), (0, 0), (0, 0), (0, LANES - 3)))

    pad_new = lambda a: jnp.pad(a.reshape(s, t_new, a.shape[-1]), ((0, 0), (0, LANES - t_new), (0, 0)))
    win = state_win.reshape(s, WINDOW, 2, W_KV).astype(BF16)
    kwa = jnp.concatenate([win[:, :, 0], pad_new(p["kwb"])], axis=1)
    vwa = jnp.concatenate([win[:, :, 1], pad_new(p["vwb"])], axis=1)

    ins = [qa8, qb16, g16, ksum, kc, vc, mm, ka, va, ks, vs,
           pad_new(p["kab"]), pad_new(p["vab"]), pad_new(p["ksb"]), pad_new(p["vsb"]), kwa, vwa]
    seq = lambda a: pl.BlockSpec((1,) + a.shape[1:], lambda i, k: (i,) + (0,) * (a.ndim - 1))
    tile = lambda w: pl.BlockSpec((1, r_tile, w), lambda i, k: (i, k, 0))
    in_specs = ([seq(a) for a in ins[:6]] + [pl.BlockSpec(mm.shape, lambda i, k: (0, 0))]
                + [tile(W_A), tile(W_A), tile(W_KV), tile(W_KV)] + [seq(a) for a in ins[11:]])
    ra, rb = 2 * t_new, GROUP * t_new
    out_shape = [jax.ShapeDtypeStruct((s, H_A // 2, ra, LANES), F32), jax.ShapeDtypeStruct((s, rb, LANES), F32)]
    out_specs = [pl.BlockSpec((1, H_A // 2, ra, LANES), lambda i, k: (i, 0, 0, 0)),
                 pl.BlockSpec((1, rb, LANES), lambda i, k: (i, 0, 0))]
    scratch = [
        pltpu.VMEM((H_A // 2, ra, LANES), F32), pltpu.VMEM((H_A // 2, ra, LANES), F32), pltpu.VMEM((H_A // 2, ra, LANES), F32),
        pltpu.VMEM((H_KV, rb, LANES), F32), pltpu.VMEM((H_KV, rb, LANES), F32), pltpu.VMEM((H_KV, rb, LANES), F32),
        pltpu.VMEM((H_A // 2, ra, nb), F32), pltpu.VMEM((H_KV, rb, nsp), F32),
        pltpu.VMEM((H_KV, rb, LANES), F32), pltpu.VMEM((H_KV, rb, LANES), F32),
    ]
    oa8, ob16 = pl.pallas_call(
        functools.partial(_attn_sample_kernel, past=past, t_new=t_new, nb=nb,
                          k_a=min(MOBA_TOPK, lp // MOBA_BLOCK), k_s=min(SLC_TOPK, n_s)),
        grid=(s, past // r_tile), in_specs=in_specs, out_specs=out_specs, out_shape=out_shape,
        scratch_shapes=scratch, compiler_params=_cparams("parallel", "arbitrary"),
    )(*ins)
    oa = jnp.where(lo, oa8[:, :, :t_new], oa8[:, :, t_new:]).transpose(0, 2, 1, 3).reshape(s * t_new, W_A)
    ob = ob16.reshape(s, GROUP, t_new, LANES).transpose(0, 2, 1, 3).reshape(s * t_new, W_B)
    return oa.astype(BF16), ob.astype(BF16)


def _post_kernel(oa_ref, ob_ref, gm_ref, x_ref, wba_ref, wbb_ref, wout_ref, gnf_ref, wg_ref, wu_ref, wd_ref,
                 y_ref, h_sc, hn_sc, acc_sc):
    j = pl.program_id(1)

    @pl.when(j == 0)
    def _():
        ma = _dot(oa_ref[...], wba_ref[...])
        mb = _dot(ob_ref[...], wbb_ref[...])
        merged = gm_ref[:, :D_MODEL] * ma + gm_ref[:, D_MODEL:] * mb
        h = x_ref[...] + _dot(merged.astype(BF16), wout_ref[...])
        h_sc[...] = h
        r = lax.rsqrt(jnp.mean(h * h, axis=-1, keepdims=True) + RMS_EPS)
        hn_sc[...] = ((h * r) * gnf_ref[...]).astype(BF16)
        acc_sc[...] = jnp.zeros(acc_sc.shape, F32)

    hn = hn_sc[...]
    g = _dot(hn, wg_ref[...])
    u = _dot(hn, wu_ref[...])
    act = (g * jax.nn.sigmoid(g)) * u
    acc_sc[...] += _dot(act.astype(BF16), wd_ref[...])

    @pl.when(j == pl.num_programs(1) - 1)
    def _():
        y_ref[...] = h_sc[...] + acc_sc[...]


def _post(oa, ob, gm, x, wts):
    n = x.shape[0]
    tm = 512 if n % 512 == 0 else n
    d_ff = wts["wd"].shape[0]
    n_ff = 2 if (d_ff // 2) % LANES == 0 else 1
    fc = d_ff // n_ff
    rows = lambda w: pl.BlockSpec((tm, w), lambda i, j: (i, 0))
    full = lambda a: pl.BlockSpec(a.shape, lambda i, j: (0,) * a.ndim)
    ins = [oa, ob, gm, x, wts["wba"], wts["wbb"], wts["wout"], wts["gnf"], wts["wgate"], wts["wup"], wts["wd"]]
    in_specs = [rows(W_A), rows(W_B), rows(2 * D_MODEL), rows(D_MODEL)] + [full(a) for a in ins[4:8]] + [
        pl.BlockSpec((D_MODEL, fc), lambda i, j: (0, j)),
        pl.BlockSpec((D_MODEL, fc), lambda i, j: (0, j)),
        pl.BlockSpec((fc, D_MODEL), lambda i, j: (j, 0)),
    ]
    return pl.pallas_call(
        _post_kernel, grid=(n // tm, n_ff), in_specs=in_specs, out_specs=rows(D_MODEL),
        out_shape=jax.ShapeDtypeStruct((n, D_MODEL), F32),
        scratch_shapes=[pltpu.VMEM((tm, D_MODEL), F32), pltpu.VMEM((tm, D_MODEL), BF16), pltpu.VMEM((tm, D_MODEL), F32)],
        compiler_params=_cparams("parallel", "arbitrary"),
    )(*ins)


def _blockdiag2(w):
    z = jnp.zeros_like(w)
    return jnp.concatenate([jnp.concatenate([w, z], axis=-1), jnp.concatenate([z, w], axis=-1)], axis=-2)


def _prep_weights(g_na, w_in, b_in, g_qk_a, g_qk_b, cmp_pos, cmp_w1, cmp_w2, w_ba, w_bb, w_out, g_nf, w_up, w_down):
    def perm_qb_cols(a):
        lead = a.shape[:-1]
        return a.reshape(lead + (H_KV, GROUP, HEAD_DIM)).swapaxes(-3, -2).reshape(lead + (W_B,))

    def main_cols(a):
        return jnp.concatenate([a[..., :C_QB], perm_qb_cols(a[..., C_QB:C_KVB]), a[..., C_KVB:C_END]], axis=-1)

    ones = lambda n: jnp.ones((n,), F32)
    gv = jnp.concatenate([
        jnp.tile(g_qk_a[0], H_A), jnp.tile(g_qk_a[1], H_A), ones(W_A), jnp.tile(g_qk_b[0], H_B),
        ones(2 * W_KV), jnp.tile(g_qk_b[2], H_KV), ones(W_KV), jnp.tile(g_qk_b[3], H_KV), ones(W_KV)])
    n_gb = 3 * H_B
    hd_idx = np.arange(W_A) // HEAD_DIM
    bd = jnp.asarray((hd_idx[:, None] == hd_idx[None, :]).astype(np.float32)).astype(BF16)
    d_ff = w_down.shape[0]
    w1 = cmp_w1.astype(BF16)
    pos2 = jnp.concatenate([cmp_pos, cmp_pos], axis=-1)
    return {
        "gna": g_na.reshape(1, D_MODEL),
        "w1": main_cols(w_in).astype(BF16), "b1": main_cols(b_in).reshape(1, C_END), "gv": gv.reshape(1, C_END),
        "wg": jnp.pad(w_in[:, C_END:C_END + n_gb], ((0, 0), (0, LANES - n_gb))).astype(BF16),
        "bg": jnp.pad(b_in[C_END:C_END + n_gb], (0, LANES - n_gb)).reshape(1, LANES),
        "wgm": w_in[:, C_END + n_gb:].astype(BF16), "bgm": b_in[C_END + n_gb:].reshape(1, 2 * D_MODEL),
        "bd": bd,
        "cpos": pos2.reshape(2, 2, CMP_STRIDE, 1, LANES).swapaxes(0, 1),
        "cwa": _blockdiag2(w1[:, :CMP_STRIDE]), "cwb": _blockdiag2(w1[:, CMP_STRIDE:]),
        "cw2": _blockdiag2(cmp_w2.astype(BF16)),
        "gkc": jnp.tile(g_qk_b[1], H_KV).reshape(1, LANES),
        "wba": w_ba.astype(BF16),
        "wbb": w_bb.reshape(H_KV, GROUP, HEAD_DIM, D_MODEL).swapaxes(0, 1).reshape(W_B, D_MODEL).astype(BF16),
        "wout": w_out.astype(BF16), "gnf": g_nf.reshape(1, D_MODEL),
        "wgate": w_up[:, :d_ff].astype(BF16), "wup": w_up[:, d_ff:].astype(BF16), "wd": w_down.astype(BF16),
    }


def _prompt_layer(x, wts):
    b, t, _ = x.shape
    x2 = x.reshape(b * t, D_MODEL)
    p = _inproj(x2, wts, with_kmean=True)
    nc = t // CMP_STRIDE
    kc, vc = _compress(p["cmp"].reshape(b, nc, CMP_STRIDE * 2 * W_KV), 2 * W_KV, wts, tc=nc)
    oa, ob = _attn_prompt(p, kc, vc, b, t)
    y = _post(oa, ob, p["gm"], x2, wts)
    keep = min(WINDOW, t)
    return (y.reshape(b, t, D_MODEL),
            p["kva"].reshape(b, t, 2, H_A, HEAD_DIM),
            p["kvn"].reshape(b, t, 4, H_KV, HEAD_DIM),
            p["kvw"].reshape(b, t, 2, H_KV, HEAD_DIM)[:, t - keep:])


def _sample_layer(x, cache_moba, cache_nsa, state_win, page_table, wts):
    s, t_new, _ = x.shape
    past = page_table.shape[1] * PAGE_SIZE
    x2 = x.reshape(s * t_new, D_MODEL)
    p = _inproj(x2, wts, with_kmean=False)
    asm = _assemble(page_table, cache_moba, cache_nsa)
    nc = past // CMP_STRIDE
    kc, vc = _compress(asm[2].reshape(s, nc, CMP_STRIDE * 2 * W_KV), 2 * W_KV, wts, tc=min(nc, 256))
    oa, ob = _attn_sample(p, asm, kc, vc, state_win, s, t_new, past)
    y = _post(oa, ob, p["gm"], x2, wts)
    kvw_new = p["kvw"].reshape(s, t_new, 2, H_KV, HEAD_DIM)
    win = jnp.concatenate([state_win, kvw_new], axis=1)[:, t_new:]
    return (y.reshape(s, t_new, D_MODEL),
            p["kva"].reshape(s, t_new, 2, H_A, HEAD_DIM),
            p["kvn"].reshape(s, t_new, 4, H_KV, HEAD_DIM),
            win)


def kernel(x_prompt, x_sample, cache_moba_kv, cache_nsa_kv, state_win_kv, page_table, g_norm_attn, w_in, b_in,
           g_qk_moba, g_qk_nsa, cmp_pos, cmp_w1, cmp_w2, w_br_moba, w_br_nsa, w_out, g_norm_ffn, w_up, w_down):
    assert g_norm_attn.shape[0] == 1, "single-layer step"
    wts = _prep_weights(g_norm_attn[0], w_in[0], b_in[0], g_qk_moba[0], g_qk_nsa[0], cmp_pos[0], cmp_w1[0],
                        cmp_w2[0], w_br_moba[0], w_br_nsa[0], w_out[0], g_norm_ffn[0], w_up[0], w_down[0])
    y_p, a_p, n_p, wn_p = _prompt_layer(x_prompt, wts)
    y_s, a_s, n_s, wn_s = _sample_layer(x_sample, cache_moba_kv[0], cache_nsa_kv[0], state_win_kv[0], page_table, wts)
    return (y_p, y_s, a_p[None], n_p[None], wn_p[None], a_s[None], n_s[None], wn_s[None])
```

```python
import functools
import math

import numpy as np
import jax
import jax.numpy as jnp
from jax import lax
from jax.experimental import pallas as pl
from jax.experimental.pallas import tpu as pltpu

F32 = jnp.float32
BF16 = jnp.bfloat16

D_MODEL = 1024
PAGE_SIZE = 128
HEAD_DIM = 64
H_A = 8
H_B = 8
H_KV = 2
GROUP = H_B // H_KV
MOBA_BLOCK = 256
MOBA_TOPK = 3
CMP_LEN = 32
CMP_STRIDE = 16
SLC_BLOCK = 64
SLC_TOPK = 16
WINDOW = 512
CMP_HID = 2 * HEAD_DIM
FORCE_BONUS = 1e4
RMS_EPS = 1e-6
W_A = H_A * HEAD_DIM
W_B = H_B * HEAD_DIM
W_KV = H_KV * HEAD_DIM
QK_SCALE = HEAD_DIM ** -0.5

LANES = 128
VMEM_LIMIT = 56 * 1024 * 1024
NEG_INF = float("-inf")

SLOPES_A = tuple(2.0 ** (-8.0 * (i + 1) / H_A) for i in range(H_A))
SLOPES_B = tuple(2.0 ** (-8.0 * (i + 1) / H_B) for i in range(H_B))


def _round_up(x, m):
    return -(-x // m) * m


def _cparams(*sem):
    return pltpu.CompilerParams(dimension_semantics=sem, vmem_limit_bytes=VMEM_LIMIT)


def _dot(a, b):
    return jnp.dot(a, b, preferred_element_type=F32)


def _dot_t(a, b):
    return lax.dot_general(a, b, (((1,), (1,)), ((), ())), preferred_element_type=F32)


def _head_sumsq(z, bd):
    zz = z * z
    hi = zz.astype(BF16)
    lo = (zz - hi.astype(F32)).astype(BF16)
    return _dot(hi, bd) + _dot(lo, bd)


def _head_norm(z, g, bd):
    ss = _head_sumsq(z, bd)
    return (z * lax.rsqrt(ss * (1.0 / HEAD_DIM) + RMS_EPS)) * g


C_QA, C_KA, C_VA, C_QB, C_KVB, C_END = 0, 512, 1024, 1536, 2048, 2816


def _inproj_kernel(x_ref, gna_ref, w1_ref, b1_ref, gv_ref, wg_ref, bg_ref, wgm_ref, bgm_ref, bd_ref,
                   qa_ref, kva_ref, kab_ref, vab_ref, qb_ref, kvn_ref, cmpk_ref, cmpv_ref, ksb_ref, vsb_ref, kvw_ref,
                   kwb_ref, vwb_ref, gb_ref, gm_ref, *maybe_km, n_blk):
    x = x_ref[...]
    r = lax.rsqrt(jnp.mean(x * x, axis=-1, keepdims=True) + RMS_EPS)
    xb = ((x * r) * gna_ref[...]).astype(BF16)
    bd = bd_ref[...]
    bd1 = bd_ref[:LANES, :LANES]

    def sec(a, b):
        return _dot(xb, w1_ref[:, a:b]) + b1_ref[:, a:b]

    qa = _head_norm(sec(C_QA, C_KA), gv_ref[:, C_QA:C_KA], bd)
    qa_ref[...] = (qa * QK_SCALE).astype(BF16)

    ka = _head_norm(sec(C_KA, C_VA), gv_ref[:, C_KA:C_VA], bd)
    va = sec(C_VA, C_QB)
    for h in range(H_A):
        kva_ref[:, 0, h, :] = ka[:, h * HEAD_DIM:(h + 1) * HEAD_DIM]
        kva_ref[:, 1, h, :] = va[:, h * HEAD_DIM:(h + 1) * HEAD_DIM]
    kab_ref[...] = ka.astype(BF16)
    vab_ref[...] = va.astype(BF16)
    if n_blk:
        km_ref = maybe_km[0]
        km_ref[0] = jnp.mean(ka.reshape(n_blk, MOBA_BLOCK, W_A), axis=1)

    qb = _head_norm(sec(C_QB, C_KVB), gv_ref[:, C_QB:C_KVB], bd)
    qb_ref[...] = (qb * QK_SCALE).astype(BF16)

    kvb = sec(C_KVB, C_END)
    ks = _head_norm(kvb[:, 256:384], gv_ref[:, C_KVB + 256:C_KVB + 384], bd1)
    vs = kvb[:, 384:512]
    kw = _head_norm(kvb[:, 512:640], gv_ref[:, C_KVB + 512:C_KVB + 640], bd1)
    vw = kvb[:, 640:768]
    cmpk_ref[...] = kvb[:, 0:128]
    cmpv_ref[...] = kvb[:, 128:256]
    for h in range(H_KV):
        lanes = slice(h * HEAD_DIM, (h + 1) * HEAD_DIM)
        kvn_ref[:, 0, h, :] = kvb[:, 0:128][:, lanes]
        kvn_ref[:, 1, h, :] = kvb[:, 128:256][:, lanes]
        kvn_ref[:, 2, h, :] = ks[:, lanes]
        kvn_ref[:, 3, h, :] = vs[:, lanes]
        kvw_ref[:, 0, h, :] = kw[:, lanes]
        kvw_ref[:, 1, h, :] = vw[:, lanes]
    ksb_ref[...] = ks.astype(BF16)
    vsb_ref[...] = vs.astype(BF16)
    kwb_ref[...] = kw.astype(BF16)
    vwb_ref[...] = vw.astype(BF16)

    gb_ref[...] = jax.nn.sigmoid(_dot(xb, wg_ref[...]) + bg_ref[...])
    gm_ref[...] = jax.nn.sigmoid(_dot(xb, wgm_ref[...]) + bgm_ref[...])


def _inproj(x, wts, with_kmean):
    n = x.shape[0]
    tm = MOBA_BLOCK if n % MOBA_BLOCK == 0 else n
    n_blk = tm // MOBA_BLOCK if with_kmean else 0
    grid = (n // tm,)

    def rows(w):
        if isinstance(w, tuple):
            return pl.BlockSpec((tm,) + w, lambda i: (i,) + (0,) * len(w))
        return pl.BlockSpec((tm, w), lambda i: (i, 0))

    def full(a):
        return pl.BlockSpec(a.shape, lambda i: (0,) * a.ndim)

    ins = [x, wts["gna"], wts["w1"], wts["b1"], wts["gv"], wts["wg"], wts["bg"], wts["wgm"], wts["bgm"], wts["bd"]]
    in_specs = [rows(D_MODEL)] + [full(a) for a in ins[1:]]
    outs = [
        (W_A, BF16), ((2, H_A, HEAD_DIM), F32), (W_A, BF16), (W_A, BF16), (W_B, BF16), ((4, H_KV, HEAD_DIM), F32),
        (W_KV, F32), (W_KV, F32), (W_KV, BF16), (W_KV, BF16), ((2, H_KV, HEAD_DIM), F32), (W_KV, BF16), (W_KV, BF16),
        (LANES, F32), (2 * D_MODEL, F32),
    ]
    out_shape = [jax.ShapeDtypeStruct((n,) + (w if isinstance(w, tuple) else (w,)), dt) for w, dt in outs]
    out_specs = [rows(w) for w, _ in outs]
    if n_blk:
        out_shape.append(jax.ShapeDtypeStruct((n // tm, n_blk, W_A), F32))
        out_specs.append(pl.BlockSpec((1, n_blk, W_A), lambda i: (i, 0, 0)))
    res = pl.pallas_call(
        functools.partial(_inproj_kernel, n_blk=n_blk),
        grid=grid, in_specs=in_specs, out_specs=out_specs, out_shape=out_shape,
        compiler_params=_cparams("parallel"),
    )(*ins)
    names = ["qa", "kva", "kab", "vab", "qb", "kvn", "cmpk", "cmpv", "ksb", "vsb", "kvw", "kwb", "vwb", "gb", "gm"]
    out = dict(zip(names, res))
    if n_blk:
        out["kmean"] = res[-1].reshape(n // MOBA_BLOCK, W_A)
    return out


def _compress_kernel(xk_ref, xv_ref, hk_ref, hv_ref, pos_ref, wa_ref, wb_ref, w2_ref, gk_ref, bd_ref, kc_ref, vc_ref,
                     *, tc):
    bd1 = bd_ref[:LANES, :LANES]

    def half_proj(refs, n_chunks, w_ref, half):
        outs = []
        for br in range(2):
            acc = None
            for l in range(CMP_STRIDE):
                xl = (refs[br][0, pl.ds(l, n_chunks, stride=CMP_STRIDE), :] + pos_ref[half, br, l]).astype(BF16)
                t = _dot(xl, w_ref[br, l])
                acc = t if acc is None else acc + t
            outs.append(acc)
        return outs

    a_k, a_v = half_proj((xk_ref, xv_ref), tc, wa_ref, 0)
    b_k, b_v = half_proj((xk_ref, xv_ref), tc, wb_ref, 1)
    bh_k, bh_v = half_proj((hk_ref, hv_ref), 8, wb_ref, 1)
    rows = lax.broadcasted_iota(jnp.int32, (tc, 2 * CMP_HID), 0)

    def shift_up(b, bh):
        rolled = pltpu.roll(b, tc - 1, 0)
        return jnp.where(rows == tc - 1, jnp.broadcast_to(bh[0:1], b.shape), rolled)

    hid_k = a_k + shift_up(b_k, bh_k)
    hid_v = a_v + shift_up(b_v, bh_v)
    hid_k = hid_k * jax.nn.sigmoid(hid_k)
    hid_v = hid_v * jax.nn.sigmoid(hid_v)
    out_k = _dot(hid_k.astype(BF16), w2_ref[0])
    out_v = _dot(hid_v.astype(BF16), w2_ref[1])
    kc_ref[0] = _head_norm(out_k, gk_ref[...], bd1).astype(BF16)
    vc_ref[0] = out_v.astype(BF16)


def _compress(rows_k, rows_v, wts, tc):
    b, length, _ = rows_k.shape
    nc = length // CMP_STRIDE
    nt = nc // tc
    last_h = nc // 8 - 1
    full = lambda a: pl.BlockSpec(a.shape, lambda i, j: (0,) * a.ndim)
    main = pl.BlockSpec((1, tc * CMP_STRIDE, LANES), lambda i, j: (i, j, 0))
    halo = pl.BlockSpec((1, 8 * CMP_STRIDE, LANES), lambda i, j: (i, jnp.minimum((j + 1) * (tc // 8), last_h), 0))
    ins = [rows_k, rows_v, rows_k, rows_v, wts["cpos"], wts["cwa"], wts["cwb"], wts["cw2"], wts["gkc"], wts["bd"]]
    in_specs = [main, main, halo, halo] + [full(a) for a in ins[4:]]
    out_shape = [jax.ShapeDtypeStruct((b, nc, LANES), BF16)] * 2
    out_specs = [pl.BlockSpec((1, tc, LANES), lambda i, j: (i, j, 0))] * 2
    return pl.pallas_call(
        functools.partial(_compress_kernel, tc=tc),
        grid=(b, nt), in_specs=in_specs, out_specs=out_specs, out_shape=out_shape,
        compiler_params=_cparams("parallel", "arbitrary"),
    )(*ins)


def _flash_init(m_ref, l_ref, acc_ref):
    m_ref[...] = jnp.full(m_ref.shape, NEG_INF, F32)
    l_ref[...] = jnp.zeros(l_ref.shape, F32)
    acc_ref[...] = jnp.zeros(acc_ref.shape, F32)


def _flash_step(s, valid, v, m_ref, l_ref, acc_ref, i):
    s = jnp.where(valid, s, NEG_INF)
    m_old = m_ref[i]
    m_new = jnp.maximum(m_old, jnp.max(s, axis=1, keepdims=True))
    m_safe = jnp.where(m_new == NEG_INF, 0.0, m_new)
    alpha = jnp.exp(m_old - m_safe)
    p = jnp.exp(s - m_safe[:, :1])
    l_ref[i] = alpha * l_ref[i] + jnp.sum(p, axis=1, keepdims=True)
    acc_ref[i] = alpha * acc_ref[i] + _dot(p.astype(BF16), v)
    m_ref[i] = m_new


def _flash_out(l_ref, acc_ref, i):
    l = l_ref[i]
    return acc_ref[i] / jnp.where(l > 0, l, 1.0)


def _topk_mask(v, idx_iota, k):
    width = v.shape[1]

    def body(_, carry):
        v, sel = carry
        mx = jnp.max(v, axis=1, keepdims=True)
        idx = jnp.min(jnp.where(v == mx, idx_iota, width), axis=1, keepdims=True)
        hit = idx_iota == idx
        sel = jnp.where(hit & (mx > NEG_INF), 1.0, sel)
        return jnp.where(hit, NEG_INF, v), sel

    return lax.fori_loop(0, k, body, (v, jnp.zeros(v.shape, F32)))[1]


def _cmp_to_block_matrix(ncp, nsp, n_cmp, n_s):
    m = np.zeros((ncp, nsp), np.float32)
    per = SLC_BLOCK // CMP_STRIDE
    for i in range(n_s * per):
        for t in (i, i - 1):
            if 0 <= t < n_cmp:
                m[t, i // per] += 1.0
    return jnp.asarray(m)


MASK_BIG = 2.0 ** 30
AUX_R = LANES - 1


def _key_aux_table(t, block):
    assert t // block <= AUX_R and block <= 256
    key = np.arange(t)
    tab = np.zeros((t, LANES), np.float32)
    tab[key, key // block] = 1.0
    tab[:, AUX_R] = key % block
    return jnp.asarray(tab, BF16)


KEY_CHUNK = 64


def _flash_t_step(s_ref, p_ref, mask, v_t, m_ref, l_ref, acc_ref, i):
    n_keys, n_q = s_ref.shape
    chunks = [slice(r, r + KEY_CHUNK) for r in range(0, n_keys, KEY_CHUNK)]

    def scores(rows):
        s = s_ref[rows, :]
        if mask is not None:
            s = jnp.where(mask[rows, :], s, -MASK_BIG)
        return s.reshape(KEY_CHUNK // 8, 8, n_q)

    m_old = m_ref[i]
    m8 = None
    for rows in chunks:
        cm = jnp.max(scores(rows), axis=0)
        m8 = cm if m8 is None else jnp.maximum(m8, cm)
    m_new = jnp.maximum(m_old, jnp.max(m8, axis=0, keepdims=True))
    alpha = jnp.exp(m_old - m_new)
    l8 = None
    for rows in chunks:
        p = jnp.exp(scores(rows) - m_new[None])
        ps = jnp.sum(p, axis=0)
        l8 = ps if l8 is None else l8 + ps
        p_ref[rows, :] = p.reshape(KEY_CHUNK, n_q).astype(BF16)
    l_ref[i] = alpha * l_ref[i] + l8
    acc_ref[i] = alpha[0:1] * acc_ref[i] + _dot(v_t, p_ref[...])
    m_ref[i] = m_new


def _flash_t_out(l_ref, acc_ref, i):
    l = jnp.sum(l_ref[i], axis=0, keepdims=True)
    return acc_ref[i] / jnp.where(l > 0, l, 1.0)


def _topk_mask_t(v, idx, k):
    n = v.shape[0]

    def body(_, carry):
        v, sel = carry
        mx = jnp.max(v, axis=0, keepdims=True)
        first = jnp.min(jnp.where(v == mx, idx, n), axis=0, keepdims=True)
        hit = idx == first
        sel = jnp.where(hit & (mx > NEG_INF), 1.0, sel)
        return jnp.where(hit, NEG_INF, v), sel

    return lax.fori_loop(0, k, body, (v, jnp.zeros(v.shape, F32)))[1]


def _attn_prompt_kernel(qa_ref, qb_ref, gb_ref, km_ref, ka_ref, va_ref, kc_ref, vc_ref, ks_ref, vs_ref,
                        kw_ref, vw_ref, auxa_ref, auxs_ref, mm_ref, oa_ref, ob_ref,
                        m_sc, l_sc, acc_sc, oc_sc, os_sc, qx_sc, s_sc, p_sc, *, n_cmp, k_a, k_s):
    c = pl.program_id(1)
    tq = MOBA_BLOCK
    sub = lax.broadcasted_iota(jnp.int32, (LANES, tq), 0)
    is_lo = sub < HEAD_DIM
    is_r = sub == AUX_R
    key_r = lax.broadcasted_iota(jnp.int32, (tq, tq), 0)
    qry_r = lax.broadcasted_iota(jnp.int32, (tq, tq), 1)
    causal = key_r <= qry_r

    def masked_q(ref, tile, half):
        q = ref[0, tile * LANES:(tile + 1) * LANES, :]
        keep = is_lo if half == 0 else jnp.logical_not(is_lo)
        return jnp.where(keep, q, jnp.zeros_like(q))

    def set_query(i, q_t, aux):
        qx_sc[i, :LANES, :] = q_t
        qx_sc[i, LANES:, :] = aux.astype(BF16)

    def flash_init():
        m_sc[...] = jnp.full(m_sc.shape, NEG_INF, F32)
        l_sc[...] = jnp.zeros(l_sc.shape, F32)
        acc_sc[...] = jnp.zeros(acc_sc.shape, F32)

    def sweep(k_ref, k_lanes, aux_ref, v_ref, v_rows, heads, n, mask):
        koff = pl.multiple_of(n * tq, tq)
        kaux = aux_ref[pl.ds(koff, tq), :]
        n_buf = s_sc.shape[0]

        def stage_scores(i):
            kcat = jnp.concatenate([k_ref[0, pl.ds(koff, tq), k_lanes(i)], kaux], axis=1)
            s_sc[i % n_buf] = _dot(kcat, qx_sc[i])

        for i in heads[:n_buf - 1]:
            stage_scores(i)
        for pos, i in enumerate(heads):
            if pos + n_buf - 1 < len(heads):
                stage_scores(heads[pos + n_buf - 1])
            _flash_t_step(s_sc.at[i % n_buf], p_sc.at[i % 2], mask, v_ref[0, n, v_rows(i), :], m_sc, l_sc, acc_sc, i)

    def past_loop(block_fn):
        def body(n, carry):
            block_fn(n)
            return carry
        lax.fori_loop(0, c, body, 0)

    blk = sub
    back = (c - blk).astype(F32)
    for h in range(H_A):
        j, half = divmod(h, 2)
        q_t = masked_q(qa_ref, j, half)
        gate = _dot(km_ref[0, :, j * LANES:(j + 1) * LANES].astype(BF16), q_t)
        sel = _topk_mask_t(jnp.where(blk < c, gate, NEG_INF), blk, k_a)
        bias = jnp.where(sel > 0.5, 0.0, -MASK_BIG) - (SLOPES_A[h] * tq) * back
        set_query(h, q_t, jnp.where(is_r, SLOPES_A[h], jnp.where(blk < c, bias, 0.0)))

    pair_lanes = lambda i: slice((i // 2) * LANES, (i // 2 + 1) * LANES)
    all_lanes = lambda i: slice(0, LANES)
    heads = range(H_A)
    flash_init()
    sweep(ka_ref, pair_lanes, auxa_ref, va_ref, pair_lanes, heads, c, causal)
    past_loop(lambda n: sweep(ka_ref, pair_lanes, auxa_ref, va_ref, pair_lanes, heads, n, None))
    for j in range(H_A // 2):
        o = jnp.where(is_lo, _flash_t_out(l_sc, acc_sc, 2 * j), _flash_t_out(l_sc, acc_sc, 2 * j + 1))
        oa_ref[0, j * LANES:(j + 1) * LANES, :] = o.astype(BF16)

    ncp = kc_ref.shape[1]
    nsp = mm_ref.shape[0]
    jc = lax.broadcasted_iota(jnp.int32, (ncp, tq), 0)
    d_c = c * tq + lax.broadcasted_iota(jnp.int32, (ncp, tq), 1) - (jc * CMP_STRIDE + (CMP_LEN - 1))
    c_valid = (d_c >= 0) & (jc < n_cmp)
    d_cf = d_c.astype(F32)
    sblk = lax.broadcasted_iota(jnp.int32, (nsp, tq), 0)
    cur_s = (c * tq + lax.broadcasted_iota(jnp.int32, (nsp, tq), 1)) // SLC_BLOCK
    forced = (sblk == 0) | (sblk >= cur_s - 1)
    causal_s = sblk <= cur_s
    rel_s = (sblk - c * (tq // SLC_BLOCK)).astype(F32)
    for k in range(H_KV):
        imp = jnp.zeros((ncp, tq), F32)
        for g in range(GROUP):
            i = k * GROUP + g
            s = _dot(kc_ref[0], masked_q(qb_ref, g, k)) - SLOPES_B[i] * d_cf
            s = jnp.where(c_valid, s, NEG_INF)
            m = jnp.max(s, axis=0, keepdims=True)
            m = jnp.where(m == NEG_INF, 0.0, m)
            e = jnp.where(c_valid, jnp.exp(s - m), 0.0)
            d = jnp.sum(e, axis=0, keepdims=True)
            p = e / jnp.where(d > 0, d, 1.0)
            imp = imp + p
            oc_sc[i] = _dot(vc_ref[0], p.astype(BF16))
        impb = jnp.dot(mm_ref[...], imp, preferred_element_type=F32, precision=lax.Precision.HIGHEST)
        impb = jnp.where(forced, impb + FORCE_BONUS, impb)
        impb = jnp.where(causal_s, impb, NEG_INF)
        sel = _topk_mask_t(impb, sblk, k_s)
        for g in range(GROUP):
            i = k * GROUP + g
            bias = jnp.where(sel > 0.5, 0.0, -MASK_BIG) + (SLOPES_B[i] * SLC_BLOCK) * rel_s
            set_query(i, masked_q(qb_ref, g, k), jnp.where(is_r, SLOPES_B[i], bias))

    heads = range(H_B)
    flash_init()
    sweep(ks_ref, all_lanes, auxs_ref, vs_ref, all_lanes, heads, c, causal)
    past_loop(lambda n: sweep(ks_ref, all_lanes, auxs_ref, vs_ref, all_lanes, heads, n, None))
    for i in range(H_B):
        os_sc[i] = _flash_t_out(l_sc, acc_sc, i)

    for i in range(H_B):
        qx_sc[i, LANES:, :] = jnp.where(is_r, SLOPES_B[i], -(SLOPES_B[i] * tq) * back).astype(BF16)
    flash_init()
    sweep(kw_ref, all_lanes, auxa_ref, vw_ref, all_lanes, heads, c, causal)
    n_back = WINDOW // tq
    for b in range(1, n_back + 1):
        mask = None if b < n_back else key_r > qry_r
        pl.when(c >= b)(functools.partial(sweep, kw_ref, all_lanes, auxa_ref, vw_ref, all_lanes, heads, c - b, mask))

    gb = gb_ref[0]
    for j in range(GROUP):
        def comb(i):
            return (gb[3 * i:3 * i + 1] * oc_sc[i] + gb[3 * i + 1:3 * i + 2] * os_sc[i]
                    + gb[3 * i + 2:3 * i + 3] * _flash_t_out(l_sc, acc_sc, i))
        ob_ref[0, j * LANES:(j + 1) * LANES, :] = jnp.where(is_lo, comb(j), comb(GROUP + j)).astype(BF16)


def _attn_prompt(p, kc, vc, b, t):
    tq = MOBA_BLOCK
    nb = t // tq
    n_s = t // SLC_BLOCK
    n_cmp = t // CMP_STRIDE - 1
    nsp = _round_up(n_s, LANES)
    ncp = kc.shape[1]
    assert nb <= AUX_R and n_s <= AUX_R and nsp == LANES and WINDOW % tq == 0
    km = jnp.pad(p["kmean"].reshape(b, nb, W_A), ((0, 0), (0, LANES - nb), (0, 0)))
    mm = _cmp_to_block_matrix(ncp, nsp, n_cmp, n_s).T
    r3 = lambda a: a.reshape(b, t, a.shape[-1])
    tr = lambda a: r3(a).transpose(0, 2, 1)
    qtile = lambda w: pl.BlockSpec((1, w, tq), lambda i, c: (i, 0, c))
    trb = lambda a: a.reshape(b, nb, tq, a.shape[-1]).transpose(0, 1, 3, 2)
    seq = lambda a: pl.BlockSpec((1,) + a.shape[1:], lambda i, c: (i,) + (0,) * (a.ndim - 1))
    const = lambda a: pl.BlockSpec(a.shape, lambda i, c: (0, 0))
    ins = [tr(p["qa"]), tr(p["qb"]), tr(p["gb"]), km, r3(p["kab"]), trb(p["vab"]), kc, vc.transpose(0, 2, 1),
           r3(p["ksb"]), trb(p["vsb"]), r3(p["kwb"]), trb(p["vwb"]),
           _key_aux_table(t, MOBA_BLOCK), _key_aux_table(t, SLC_BLOCK), mm]
    in_specs = [qtile(W_A), qtile(W_B), qtile(LANES)] + [seq(a) for a in ins[3:12]] + [const(a) for a in ins[12:]]
    out_shape = [jax.ShapeDtypeStruct((b, W_A, t), BF16), jax.ShapeDtypeStruct((b, W_B, t), BF16)]
    out_specs = [qtile(W_A), qtile(W_B)]
    scratch = [
        pltpu.VMEM((H_A, 8, tq), F32), pltpu.VMEM((H_A, 8, tq), F32), pltpu.VMEM((H_A, LANES, tq), F32),
        pltpu.VMEM((H_B, LANES, tq), F32), pltpu.VMEM((H_B, LANES, tq), F32), pltpu.VMEM((H_A, 2 * LANES, tq), BF16),
        pltpu.VMEM((6, tq, tq), F32), pltpu.VMEM((2, tq, tq), BF16),
    ]
    oa, ob = pl.pallas_call(
        functools.partial(_attn_prompt_kernel, n_cmp=n_cmp, k_a=min(MOBA_TOPK, nb), k_s=min(SLC_TOPK, n_s)),
        grid=(b, nb), in_specs=in_specs, out_specs=out_specs, out_shape=out_shape, scratch_shapes=scratch,
        compiler_params=_cparams("parallel", "arbitrary"),
    )(*ins)
    return oa.transpose(0, 2, 1).reshape(b * t, W_A), ob.transpose(0, 2, 1).reshape(b * t, W_B)


def _assemble_kernel(pt_ref, nsa_ref, kc_ref, vc_ref, ks_ref, vs_ref):
    del pt_ref
    n = nsa_ref[0]
    kc_ref[0] = n[:, :W_KV]
    vc_ref[0] = n[:, W_KV:2 * W_KV]
    ks_ref[0] = n[:, 2 * W_KV:3 * W_KV].astype(BF16)
    vs_ref[0] = n[:, 3 * W_KV:].astype(BF16)


def _assemble(page_table, cache_nsa):
    s, n_pages = page_table.shape
    past = n_pages * PAGE_SIZE
    cache_nsa = cache_nsa.reshape(cache_nsa.shape[0], PAGE_SIZE, 4 * W_KV)
    page = pl.BlockSpec((1, PAGE_SIZE, 4 * W_KV), lambda b, p, pt: (pt[b, p], 0, 0))
    dst = pl.BlockSpec((1, PAGE_SIZE, W_KV), lambda b, p, pt: (b, p, 0))
    out_shape = [jax.ShapeDtypeStruct((s, past, W_KV), dt) for dt in (F32, F32, BF16, BF16)]
    return pl.pallas_call(
        _assemble_kernel,
        grid_spec=pltpu.PrefetchScalarGridSpec(
            num_scalar_prefetch=1, grid=(s, n_pages), in_specs=[page], out_specs=[dst] * 4),
        out_shape=out_shape, compiler_params=_cparams("parallel", "arbitrary"),
    )(page_table, cache_nsa)


MOBA_SAMPLE_PAGES = 4


def _moba_sample_tables(t_new):
    rows = H_A * t_new
    head = np.arange(rows) // t_new
    qidx = np.arange(rows) % t_new
    slope = np.asarray(SLOPES_A, np.float32)[head]
    col = np.arange(PAGE_SIZE * H_A)
    match = head[:, None] == (col % H_A)[None, :]
    b0 = np.where(match, slope[:, None] * (col // H_A)[None, :], -MASK_BIG).astype(np.float32)
    coln = np.arange(t_new * H_A)
    ok = (head[:, None] == (coln % H_A)[None, :]) & ((coln // H_A)[None, :] <= qidx[:, None])
    bnew = np.where(ok, slope[:, None] * (coln // H_A)[None, :], -MASK_BIG).astype(np.float32)
    hm = match.astype(np.float32)
    slope_l = np.broadcast_to(slope[:, None], (rows, LANES)).astype(np.float32)
    return jnp.asarray(b0), jnp.asarray(bnew), jnp.asarray(hm), jnp.asarray(slope_l)


def _moba_sample_kernel(pt_ref, q_ref, b0_ref, bnew_ref, hm_ref, slope_ref, new_ref, *rest, n_pages, k_a):
    g_pages = MOBA_SAMPLE_PAGES
    page_refs, o_ref = rest[:g_pages], rest[g_pages]
    g_sc, m_sc, l_sc, o_sc = rest[g_pages + 1:]
    del pt_ref
    step = pl.program_id(1)
    q = q_ref[0]
    rows = q.shape[0]
    slope = slope_ref[:, 0:1]

    def partial(s, v2):
        m = jnp.max(s, axis=1, keepdims=True)
        e = jnp.exp(s - m)
        l = jnp.sum(e, axis=1, keepdims=True)
        return m, l, _dot(e.astype(BF16), v2)

    for g in range(g_pages):
        p = step * g_pages + g
        k2 = page_refs[g][0, :, 0].reshape(PAGE_SIZE * H_A, HEAD_DIM).astype(BF16)
        v2 = page_refs[g][0, :, 1].reshape(PAGE_SIZE * H_A, HEAD_DIM).astype(BF16)
        qk = _dot_t(q, k2)
        g_sc[p] = jnp.broadcast_to(jnp.sum(qk * hm_ref[...], axis=1, keepdims=True), (rows, LANES))
        m, l, o = partial(qk + b0_ref[...] + slope * ((p - n_pages) * PAGE_SIZE).astype(F32), v2)
        m_sc[p] = jnp.broadcast_to(m, (rows, LANES))
        l_sc[p] = jnp.broadcast_to(l, (rows, LANES))
        o_sc[p] = o

    @pl.when(step == pl.num_programs(1) - 1)
    def _():
        per_blk = MOBA_BLOCK // PAGE_SIZE
        nb = n_pages // per_blk
        gate = jnp.sum(g_sc[...].reshape(nb, per_blk, rows, LANES), axis=1)
        sel = _topk_mask_t(gate, lax.broadcasted_iota(jnp.int32, gate.shape, 0), k_a)
        sel = jnp.broadcast_to(sel[:, None], (nb, per_blk, rows, LANES)).reshape(n_pages, rows, LANES) > 0.5
        t_new = new_ref.shape[1]
        k_new = new_ref[0, :, 0].reshape(t_new * H_A, HEAD_DIM).astype(BF16)
        v_new = new_ref[0, :, 1].reshape(t_new * H_A, HEAD_DIM).astype(BF16)
        m_o, l_o, o_o = partial(_dot_t(q, k_new) + bnew_ref[...], v_new)
        m_all = m_sc[...]
        m_star = jnp.maximum(jnp.max(jnp.where(sel, m_all, -MASK_BIG), axis=0), m_o)
        w = jnp.where(sel, jnp.exp(m_all - m_star[None]), 0.0)
        w_o = jnp.exp(m_o - m_star)
        l_star = jnp.sum(w * l_sc[...], axis=0) + w_o * l_o
        o_star = jnp.sum(w[:, :, :HEAD_DIM] * o_sc[...], axis=0) + w_o[:, :HEAD_DIM] * o_o
        o_ref[0] = o_star / l_star[:, :HEAD_DIM]


def _moba_sample(p, cache_moba, page_table, s, t_new):
    n_pages = page_table.shape[1]
    g_pages = MOBA_SAMPLE_PAGES
    assert n_pages % g_pages == 0 and n_pages % (MOBA_BLOCK // PAGE_SIZE) == 0
    rows = H_A * t_new
    nb = n_pages // (MOBA_BLOCK // PAGE_SIZE)
    q = p["qa"].reshape(s, t_new, H_A, HEAD_DIM).transpose(0, 2, 1, 3).reshape(s, rows, HEAD_DIM)
    new = p["kva"].reshape(s, t_new, 2, H_A, HEAD_DIM)
    tables = _moba_sample_tables(t_new)
    const = lambda a: pl.BlockSpec(a.shape, lambda b, i, pt: (0, 0))
    seq = lambda a: pl.BlockSpec((1,) + a.shape[1:], lambda b, i, pt: (b,) + (0,) * (a.ndim - 1))
    page = lambda g: pl.BlockSpec((1,) + cache_moba.shape[1:], lambda b, i, pt: (pt[b, i * g_pages + g], 0, 0, 0, 0))
    out = pl.pallas_call(
        functools.partial(_moba_sample_kernel, n_pages=n_pages, k_a=min(MOBA_TOPK, nb)),
        grid_spec=pltpu.PrefetchScalarGridSpec(
            num_scalar_prefetch=1, grid=(s, n_pages // g_pages),
            in_specs=[seq(q)] + [const(a) for a in tables] + [seq(new)] + [page(g) for g in range(g_pages)],
            out_specs=pl.BlockSpec((1, rows, HEAD_DIM), lambda b, i, pt: (b, 0, 0)),
            scratch_shapes=[pltpu.VMEM((n_pages, rows, LANES), F32)] * 3 + [pltpu.VMEM((n_pages, rows, HEAD_DIM), F32)]),
        out_shape=jax.ShapeDtypeStruct((s, rows, HEAD_DIM), F32),
        compiler_params=_cparams("parallel", "arbitrary"),
    )(page_table, q, *tables, new, *([cache_moba] * g_pages))
    return out.reshape(s, H_A, t_new, HEAD_DIM).transpose(0, 2, 1, 3).reshape(s * t_new, W_A).astype(BF16)


SAMPLE_KV_TILE = 2048


def _attn_sample_kernel(qb16_ref, g16_ref, kc_ref, vc_ref, mm_ref, ks_ref, vs_ref,
                        ksn_ref, vsn_ref, kwa_ref, vwa_ref, ob_ref,
                        mb_sc, lb_sc, accb_sc, sels_sc, oc_sc, os_sc, *, past, t_new, k_s):
    kt = pl.program_id(1)
    r_tile = ks_ref.shape[1]
    rb = GROUP * t_new
    row_b = lax.broadcasted_iota(jnp.int32, (rb, 1), 0)
    t_b = past + row_b % t_new
    slope_b = []
    for k in range(H_KV):
        sl = jnp.full((rb, 1), SLOPES_B[k * GROUP + GROUP - 1], F32)
        for g in range(GROUP - 2, -1, -1):
            sl = jnp.where(row_b < (g + 1) * t_new, SLOPES_B[k * GROUP + g], sl)
        slope_b.append(sl)
    nsp = mm_ref.shape[1]
    cur_s = past // SLC_BLOCK

    @pl.when(kt == 0)
    def _():
        _flash_init(mb_sc, lb_sc, accb_sc)
        nc = kc_ref.shape[1]
        jc = lax.broadcasted_iota(jnp.int32, (rb, nc), 1)
        d_c = t_b - (jc * CMP_STRIDE + (CMP_LEN - 1))
        c_valid = d_c >= 0
        d_cf = d_c.astype(F32)
        sj = lax.broadcasted_iota(jnp.int32, (rb, nsp), 1)
        forced = (sj == 0) | (sj >= cur_s - 1)
        causal_s = sj <= cur_s
        gr = lax.broadcasted_iota(jnp.int32, (rb, rb), 0) % t_new
        gc = lax.broadcasted_iota(jnp.int32, (rb, rb), 1) % t_new
        group_sum = (gr == gc).astype(F32)
        for k in range(H_KV):
            s = _dot_t(qb16_ref[0, k], kc_ref[0]) - slope_b[k] * d_cf
            s = jnp.where(c_valid, s, NEG_INF)
            m = jnp.max(s, axis=1, keepdims=True)
            m = jnp.where(m == NEG_INF, 0.0, m)
            e = jnp.where(c_valid, jnp.exp(s - m), 0.0)
            d = jnp.sum(e, axis=1, keepdims=True)
            p = e / jnp.where(d > 0, d, 1.0)
            oc_sc[k] = _dot(p.astype(BF16), vc_ref[0])
            pb = jnp.dot(p, mm_ref[...], preferred_element_type=F32, precision=lax.Precision.HIGHEST)
            impb = jnp.dot(group_sum, pb, preferred_element_type=F32, precision=lax.Precision.HIGHEST)
            impb = jnp.where(forced, impb + FORCE_BONUS, impb)
            impb = jnp.where(causal_s, impb, NEG_INF)
            sels_sc[k] = _topk_mask(impb, sj, k_s)

    blocks_per_tile = r_tile // SLC_BLOCK
    first_blk = kt * blocks_per_tile
    lane_tile = pl.multiple_of((first_blk // LANES) * LANES, LANES)
    e_row = lax.broadcasted_iota(jnp.int32, (LANES, r_tile), 0)
    e_col = lax.broadcasted_iota(jnp.int32, (LANES, r_tile), 1) // SLC_BLOCK
    expand = (e_row == first_blk % LANES + e_col).astype(BF16)
    col_t = lax.broadcasted_iota(jnp.int32, (rb, r_tile), 1)
    dist_t = (t_b - (kt * r_tile + col_t)).astype(F32)
    for k in range(H_KV):
        sel_t = sels_sc[k, :, pl.ds(lane_tile, LANES)]
        valid = _dot(sel_t.astype(BF16), expand) > 0.5
        s = _dot_t(qb16_ref[0, k], ks_ref[0]) - slope_b[k] * dist_t
        _flash_step(s, valid, vs_ref[0], mb_sc, lb_sc, accb_sc, k)

    @pl.when(kt == pl.num_programs(1) - 1)
    def _():
        dist_b = row_b % t_new - lax.broadcasted_iota(jnp.int32, (rb, LANES), 1)
        sj = lax.broadcasted_iota(jnp.int32, (rb, nsp), 1)
        for k in range(H_KV):
            own = jnp.sum(jnp.where(sj == cur_s, sels_sc[k], 0.0), axis=1, keepdims=True) > 0.5
            s = _dot_t(qb16_ref[0, k], ksn_ref[0]) - slope_b[k] * dist_b.astype(F32)
            _flash_step(s, own & (dist_b >= 0), vsn_ref[0], mb_sc, lb_sc, accb_sc, k)
            os_sc[k] = _flash_out(lb_sc, accb_sc, k)
        _flash_init(mb_sc, lb_sc, accb_sc)
        n_w = kwa_ref.shape[1]
        dist_w = WINDOW + row_b % t_new - lax.broadcasted_iota(jnp.int32, (rb, n_w), 1)
        valid_w = (dist_w >= 0) & (dist_w < WINDOW)
        lane_b = lax.broadcasted_iota(jnp.int32, (rb, LANES), 1)
        comb = []
        for k in range(H_KV):
            s = _dot_t(qb16_ref[0, k], kwa_ref[0]) - slope_b[k] * dist_w.astype(F32)
            _flash_step(s, valid_w, vwa_ref[0], mb_sc, lb_sc, accb_sc, k)
            g = g16_ref[0, k]
            comb.append(g[:, 0:1] * oc_sc[k] + g[:, 1:2] * os_sc[k] + g[:, 2:3] * _flash_out(lb_sc, accb_sc, k))
        ob_ref[0] = jnp.where(lane_b < HEAD_DIM, comb[0], comb[1])


def _attn_sample(p, asm, kc, vc, state_win, s, t_new, past):
    r_tile = SAMPLE_KV_TILE
    assert past % r_tile == 0 and state_win.shape[1] == WINDOW and LANES % (r_tile // SLC_BLOCK) == 0
    ks, vs = asm
    lp = _round_up(past + t_new, MOBA_BLOCK)
    n_s = lp // SLC_BLOCK
    n_cmp = lp // CMP_STRIDE - 1
    nsp = _round_up(n_s, LANES)
    nc = kc.shape[1]
    mm = _cmp_to_block_matrix(nc, nsp, min(n_cmp, nc), n_s)
    lane = jnp.arange(LANES)
    lo = (lane < HEAD_DIM)

    qb = p["qb"].reshape(s, t_new, GROUP, LANES).transpose(0, 2, 1, 3).reshape(s, 1, GROUP * t_new, LANES)
    zero = jnp.zeros_like(qb)
    qb16 = jnp.concatenate([jnp.where(lo, qb, zero), jnp.where(lo, zero, qb)], axis=1)
    g16 = p["gb"][:, :3 * H_B].reshape(s, t_new, H_KV, GROUP, 3).transpose(0, 2, 3, 1, 4)
    g16 = jnp.pad(g16.reshape(s, H_KV, GROUP * t_new, 3), ((0, 0), (0, 0), (0, 0), (0, LANES - 3)))

    pad_new = lambda a: jnp.pad(a.reshape(s, t_new, a.shape[-1]), ((0, 0), (0, LANES - t_new), (0, 0)))
    win = state_win.reshape(s, WINDOW, 2, W_KV).astype(BF16)
    kwa = jnp.concatenate([win[:, :, 0], pad_new(p["kwb"])], axis=1)
    vwa = jnp.concatenate([win[:, :, 1], pad_new(p["vwb"])], axis=1)

    ins = [qb16, g16, kc, vc, mm, ks, vs, pad_new(p["ksb"]), pad_new(p["vsb"]), kwa, vwa]
    seq = lambda a: pl.BlockSpec((1,) + a.shape[1:], lambda i, k: (i,) + (0,) * (a.ndim - 1))
    tile = lambda w: pl.BlockSpec((1, r_tile, w), lambda i, k: (i, k, 0))
    in_specs = ([seq(a) for a in ins[:4]] + [pl.BlockSpec(mm.shape, lambda i, k: (0, 0))]
                + [tile(W_KV), tile(W_KV)] + [seq(a) for a in ins[7:]])
    rb = GROUP * t_new
    scratch = [
        pltpu.VMEM((H_KV, rb, LANES), F32), pltpu.VMEM((H_KV, rb, LANES), F32), pltpu.VMEM((H_KV, rb, LANES), F32),
        pltpu.VMEM((H_KV, rb, nsp), F32), pltpu.VMEM((H_KV, rb, LANES), F32), pltpu.VMEM((H_KV, rb, LANES), F32),
    ]
    ob16 = pl.pallas_call(
        functools.partial(_attn_sample_kernel, past=past, t_new=t_new, k_s=min(SLC_TOPK, n_s)),
        grid=(s, past // r_tile), in_specs=in_specs, out_specs=pl.BlockSpec((1, rb, LANES), lambda i, k: (i, 0, 0)),
        out_shape=jax.ShapeDtypeStruct((s, rb, LANES), F32),
        scratch_shapes=scratch, compiler_params=_cparams("parallel", "arbitrary"),
    )(*ins)
    return ob16.reshape(s, GROUP, t_new, LANES).transpose(0, 2, 1, 3).reshape(s * t_new, W_B).astype(BF16)


def _post_kernel(oa_ref, ob_ref, gm_ref, x_ref, wba_ref, wbb_ref, wout_ref, gnf_ref, wg_ref, wu_ref, wd_ref,
                 y_ref, h_sc, hn_sc, acc_sc):
    j = pl.program_id(1)

    @pl.when(j == 0)
    def _():
        ma = _dot(oa_ref[...], wba_ref[...])
        mb = _dot(ob_ref[...], wbb_ref[...])
        merged = gm_ref[:, :D_MODEL] * ma + gm_ref[:, D_MODEL:] * mb
        h = x_ref[...] + _dot(merged.astype(BF16), wout_ref[...])
        h_sc[...] = h
        r = lax.rsqrt(jnp.mean(h * h, axis=-1, keepdims=True) + RMS_EPS)
        hn_sc[...] = ((h * r) * gnf_ref[...]).astype(BF16)
        acc_sc[...] = jnp.zeros(acc_sc.shape, F32)

    hn = hn_sc[...]
    g = _dot(hn, wg_ref[...])
    u = _dot(hn, wu_ref[...])
    act = (g * jax.nn.sigmoid(g)) * u
    acc_sc[...] += _dot(act.astype(BF16), wd_ref[...])

    @pl.when(j == pl.num_programs(1) - 1)
    def _():
        y_ref[...] = h_sc[...] + acc_sc[...]


def _post(oa, ob, gm, x, wts):
    n = x.shape[0]
    tm = 512 if n % 512 == 0 else n
    d_ff = wts["wd"].shape[0]
    n_ff = 2 if (d_ff // 2) % LANES == 0 else 1
    fc = d_ff // n_ff
    rows = lambda w: pl.BlockSpec((tm, w), lambda i, j: (i, 0))
    full = lambda a: pl.BlockSpec(a.shape, lambda i, j: (0,) * a.ndim)
    ins = [oa, ob, gm, x, wts["wba"], wts["wbb"], wts["wout"], wts["gnf"], wts["wgate"], wts["wup"], wts["wd"]]
    in_specs = [rows(W_A), rows(W_B), rows(2 * D_MODEL), rows(D_MODEL)] + [full(a) for a in ins[4:8]] + [
        pl.BlockSpec((D_MODEL, fc), lambda i, j: (0, j)),
        pl.BlockSpec((D_MODEL, fc), lambda i, j: (0, j)),
        pl.BlockSpec((fc, D_MODEL), lambda i, j: (j, 0)),
    ]
    return pl.pallas_call(
        _post_kernel, grid=(n // tm, n_ff), in_specs=in_specs, out_specs=rows(D_MODEL),
        out_shape=jax.ShapeDtypeStruct((n, D_MODEL), F32),
        scratch_shapes=[pltpu.VMEM((tm, D_MODEL), F32), pltpu.VMEM((tm, D_MODEL), BF16), pltpu.VMEM((tm, D_MODEL), F32)],
        compiler_params=_cparams("parallel", "arbitrary"),
    )(*ins)


def _blockdiag2(w):
    z = jnp.zeros_like(w)
    return jnp.concatenate([jnp.concatenate([w, z], axis=-1), jnp.concatenate([z, w], axis=-1)], axis=-2)


def _prep_weights(g_na, w_in, b_in, g_qk_a, g_qk_b, cmp_pos, cmp_w1, cmp_w2, w_ba, w_bb, w_out, g_nf, w_up, w_down):
    def perm_qb_cols(a):
        lead = a.shape[:-1]
        return a.reshape(lead + (H_KV, GROUP, HEAD_DIM)).swapaxes(-3, -2).reshape(lead + (W_B,))

    def main_cols(a):
        return jnp.concatenate([a[..., :C_QB], perm_qb_cols(a[..., C_QB:C_KVB]), a[..., C_KVB:C_END]], axis=-1)

    ones = lambda n: jnp.ones((n,), F32)
    gv = jnp.concatenate([
        jnp.tile(g_qk_a[0], H_A), jnp.tile(g_qk_a[1], H_A), ones(W_A), jnp.tile(g_qk_b[0], H_B),
        ones(2 * W_KV), jnp.tile(g_qk_b[2], H_KV), ones(W_KV), jnp.tile(g_qk_b[3], H_KV), ones(W_KV)])
    n_gb = 3 * H_B
    hd_idx = np.arange(W_A) // HEAD_DIM
    bd = jnp.asarray((hd_idx[:, None] == hd_idx[None, :]).astype(np.float32)).astype(BF16)
    d_ff = w_down.shape[0]
    w1 = cmp_w1.astype(BF16)
    pos2 = jnp.concatenate([cmp_pos, cmp_pos], axis=-1)
    return {
        "gna": g_na.reshape(1, D_MODEL),
        "w1": main_cols(w_in).astype(BF16), "b1": main_cols(b_in).reshape(1, C_END), "gv": gv.reshape(1, C_END),
        "wg": jnp.pad(w_in[:, C_END:C_END + n_gb], ((0, 0), (0, LANES - n_gb))).astype(BF16),
        "bg": jnp.pad(b_in[C_END:C_END + n_gb], (0, LANES - n_gb)).reshape(1, LANES),
        "wgm": w_in[:, C_END + n_gb:].astype(BF16), "bgm": b_in[C_END + n_gb:].reshape(1, 2 * D_MODEL),
        "bd": bd,
        "cpos": pos2.reshape(2, 2, CMP_STRIDE, 1, LANES).swapaxes(0, 1),
        "cwa": _blockdiag2(w1[:, :CMP_STRIDE]), "cwb": _blockdiag2(w1[:, CMP_STRIDE:]),
        "cw2": _blockdiag2(cmp_w2.astype(BF16)),
        "gkc": jnp.tile(g_qk_b[1], H_KV).reshape(1, LANES),
        "wba": w_ba.astype(BF16),
        "wbb": w_bb.reshape(H_KV, GROUP, HEAD_DIM, D_MODEL).swapaxes(0, 1).reshape(W_B, D_MODEL).astype(BF16),
        "wout": w_out.astype(BF16), "gnf": g_nf.reshape(1, D_MODEL),
        "wgate": w_up[:, :d_ff].astype(BF16), "wup": w_up[:, d_ff:].astype(BF16), "wd": w_down.astype(BF16),
    }


def _prompt_layer(x, wts):
    b, t, _ = x.shape
    x2 = x.reshape(b * t, D_MODEL)
    p = _inproj(x2, wts, with_kmean=True)
    nc = t // CMP_STRIDE
    kc, vc = _compress(p["cmpk"].reshape(b, t, W_KV), p["cmpv"].reshape(b, t, W_KV), wts, tc=nc)
    oa, ob = _attn_prompt(p, kc, vc, b, t)
    y = _post(oa, ob, p["gm"], x2, wts)
    keep = min(WINDOW, t)
    return (y.reshape(b, t, D_MODEL),
            p["kva"].reshape(b, t, 2, H_A, HEAD_DIM),
            p["kvn"].reshape(b, t, 4, H_KV, HEAD_DIM),
            p["kvw"].reshape(b, t, 2, H_KV, HEAD_DIM)[:, t - keep:])


def _sample_layer(x, cache_moba, cache_nsa, state_win, page_table, wts):
    s, t_new, _ = x.shape
    past = page_table.shape[1] * PAGE_SIZE
    x2 = x.reshape(s * t_new, D_MODEL)
    p = _inproj(x2, wts, with_kmean=False)
    rows_kc, rows_vc, ks, vs = _assemble(page_table, cache_nsa)
    kc, vc = _compress(rows_kc, rows_vc, wts, tc=min(past // CMP_STRIDE, 256))
    oa = _moba_sample(p, cache_moba, page_table, s, t_new)
    ob = _attn_sample(p, (ks, vs), kc, vc, state_win, s, t_new, past)
    y = _post(oa, ob, p["gm"], x2, wts)
    kvw_new = p["kvw"].reshape(s, t_new, 2, H_KV, HEAD_DIM)
    win = jnp.concatenate([state_win, kvw_new], axis=1)[:, t_new:]
    return (y.reshape(s, t_new, D_MODEL),
            p["kva"].reshape(s, t_new, 2, H_A, HEAD_DIM),
            p["kvn"].reshape(s, t_new, 4, H_KV, HEAD_DIM),
            win)


def kernel(x_prompt, x_sample, cache_moba_kv, cache_nsa_kv, state_win_kv, page_table, g_norm_attn, w_in, b_in,
           g_qk_moba, g_qk_nsa, cmp_pos, cmp_w1, cmp_w2, w_br_moba, w_br_nsa, w_out, g_norm_ffn, w_up, w_down):
    assert g_norm_attn.shape[0] == 1, "single-layer step"
    wts = _prep_weights(g_norm_attn[0], w_in[0], b_in[0], g_qk_moba[0], g_qk_nsa[0], cmp_pos[0], cmp_w1[0],
                        cmp_w2[0], w_br_moba[0], w_br_nsa[0], w_out[0], g_norm_ffn[0], w_up[0], w_down[0])
    y_p, a_p, n_p, wn_p = _prompt_layer(x_prompt, wts)
    y_s, a_s, n_s, wn_s = _sample_layer(x_sample, cache_moba_kv[0], cache_nsa_kv[0], state_win_kv[0], page_table, wts)
    return (y_p, y_s, a_p[None], n_p[None], wn_p[None], a_s[None], n_s[None], wn_s[None])
```

```python
import functools
import math

import numpy as np
import jax
import jax.numpy as jnp
from jax import lax
from jax.experimental import pallas as pl
from jax.experimental.pallas import tpu as pltpu

F32 = jnp.float32
BF16 = jnp.bfloat16

D_MODEL = 1024
PAGE_SIZE = 128
HEAD_DIM = 64
H_A = 8
H_B = 8
H_KV = 2
GROUP = H_B // H_KV
MOBA_BLOCK = 256
MOBA_TOPK = 3
CMP_LEN = 32
CMP_STRIDE = 16
SLC_BLOCK = 64
SLC_TOPK = 16
WINDOW = 512
CMP_HID = 2 * HEAD_DIM
FORCE_BONUS = 1e4
RMS_EPS = 1e-6
W_A = H_A * HEAD_DIM
W_B = H_B * HEAD_DIM
W_KV = H_KV * HEAD_DIM
QK_SCALE = HEAD_DIM ** -0.5

LANES = 128
VMEM_LIMIT = 56 * 1024 * 1024
NEG_INF = float("-inf")

SLOPES_A = tuple(2.0 ** (-8.0 * (i + 1) / H_A) for i in range(H_A))
SLOPES_B = tuple(2.0 ** (-8.0 * (i + 1) / H_B) for i in range(H_B))


def _round_up(x, m):
    return -(-x // m) * m


def _cparams(*sem):
    return pltpu.CompilerParams(dimension_semantics=sem, vmem_limit_bytes=VMEM_LIMIT)


def _dot(a, b):
    return jnp.dot(a, b, preferred_element_type=F32)


def _dot_t(a, b):
    return lax.dot_general(a, b, (((1,), (1,)), ((), ())), preferred_element_type=F32)


def _head_sumsq(z, bd):
    zz = z * z
    hi = zz.astype(BF16)
    lo = (zz - hi.astype(F32)).astype(BF16)
    return _dot(hi, bd) + _dot(lo, bd)


def _head_norm(z, g, bd):
    ss = _head_sumsq(z, bd)
    return (z * lax.rsqrt(ss * (1.0 / HEAD_DIM) + RMS_EPS)) * g


C_QA, C_KA, C_VA, C_QB, C_KVB, C_END = 0, 512, 1024, 1536, 2048, 2816


def _inproj_kernel(x_ref, gna_ref, w1_ref, b1_ref, gv_ref, wg_ref, bg_ref, wgm_ref, bgm_ref, bd_ref,
                   qa_ref, kva_ref, kab_ref, vab_ref, qb_ref, kvn_ref, cmpk_ref, cmpv_ref, ksb_ref, vsb_ref, kvw_ref,
                   kwb_ref, vwb_ref, gb_ref, gm_ref, *maybe_km, n_blk):
    x = x_ref[...]
    r = lax.rsqrt(jnp.mean(x * x, axis=-1, keepdims=True) + RMS_EPS)
    xb = ((x * r) * gna_ref[...]).astype(BF16)
    bd = bd_ref[...]
    bd1 = bd_ref[:LANES, :LANES]

    def sec(a, b):
        return _dot(xb, w1_ref[:, a:b]) + b1_ref[:, a:b]

    qa = _head_norm(sec(C_QA, C_KA), gv_ref[:, C_QA:C_KA], bd)
    qa_ref[...] = (qa * QK_SCALE).astype(BF16)

    ka = _head_norm(sec(C_KA, C_VA), gv_ref[:, C_KA:C_VA], bd)
    va = sec(C_VA, C_QB)
    for h in range(H_A):
        kva_ref[:, 0, h, :] = ka[:, h * HEAD_DIM:(h + 1) * HEAD_DIM]
        kva_ref[:, 1, h, :] = va[:, h * HEAD_DIM:(h + 1) * HEAD_DIM]
    kab_ref[...] = ka.astype(BF16)
    vab_ref[...] = va.astype(BF16)
    if n_blk:
        km_ref = maybe_km[0]
        km_ref[0] = jnp.mean(ka.reshape(n_blk, MOBA_BLOCK, W_A), axis=1)

    qb = _head_norm(sec(C_QB, C_KVB), gv_ref[:, C_QB:C_KVB], bd)
    qb_ref[...] = (qb * QK_SCALE).astype(BF16)

    kvb = sec(C_KVB, C_END)
    ks = _head_norm(kvb[:, 256:384], gv_ref[:, C_KVB + 256:C_KVB + 384], bd1)
    vs = kvb[:, 384:512]
    kw = _head_norm(kvb[:, 512:640], gv_ref[:, C_KVB + 512:C_KVB + 640], bd1)
    vw = kvb[:, 640:768]
    cmpk_ref[...] = kvb[:, 0:128]
    cmpv_ref[...] = kvb[:, 128:256]
    for h in range(H_KV):
        lanes = slice(h * HEAD_DIM, (h + 1) * HEAD_DIM)
        kvn_ref[:, 0, h, :] = kvb[:, 0:128][:, lanes]
        kvn_ref[:, 1, h, :] = kvb[:, 128:256][:, lanes]
        kvn_ref[:, 2, h, :] = ks[:, lanes]
        kvn_ref[:, 3, h, :] = vs[:, lanes]
        kvw_ref[:, 0, h, :] = kw[:, lanes]
        kvw_ref[:, 1, h, :] = vw[:, lanes]
    ksb_ref[...] = ks.astype(BF16)
    vsb_ref[...] = vs.astype(BF16)
    kwb_ref[...] = kw.astype(BF16)
    vwb_ref[...] = vw.astype(BF16)

    gb_ref[...] = jax.nn.sigmoid(_dot(xb, wg_ref[...]) + bg_ref[...])
    gm_ref[...] = jax.nn.sigmoid(_dot(xb, wgm_ref[...]) + bgm_ref[...])


def _inproj(x, wts, with_kmean):
    n = x.shape[0]
    tm = MOBA_BLOCK if n % MOBA_BLOCK == 0 else n
    n_blk = tm // MOBA_BLOCK if with_kmean else 0
    grid = (n // tm,)

    def rows(w):
        if isinstance(w, tuple):
            return pl.BlockSpec((tm,) + w, lambda i: (i,) + (0,) * len(w))
        return pl.BlockSpec((tm, w), lambda i: (i, 0))

    def full(a):
        return pl.BlockSpec(a.shape, lambda i: (0,) * a.ndim)

    ins = [x, wts["gna"], wts["w1"], wts["b1"], wts["gv"], wts["wg"], wts["bg"], wts["wgm"], wts["bgm"], wts["bd"]]
    in_specs = [rows(D_MODEL)] + [full(a) for a in ins[1:]]
    outs = [
        (W_A, BF16), ((2, H_A, HEAD_DIM), F32), (W_A, BF16), (W_A, BF16), (W_B, BF16), ((4, H_KV, HEAD_DIM), F32),
        (W_KV, F32), (W_KV, F32), (W_KV, BF16), (W_KV, BF16), ((2, H_KV, HEAD_DIM), F32), (W_KV, BF16), (W_KV, BF16),
        (LANES, F32), (2 * D_MODEL, F32),
    ]
    out_shape = [jax.ShapeDtypeStruct((n,) + (w if isinstance(w, tuple) else (w,)), dt) for w, dt in outs]
    out_specs = [rows(w) for w, _ in outs]
    if n_blk:
        out_shape.append(jax.ShapeDtypeStruct((n // tm, n_blk, W_A), F32))
        out_specs.append(pl.BlockSpec((1, n_blk, W_A), lambda i: (i, 0, 0)))
    res = pl.pallas_call(
        functools.partial(_inproj_kernel, n_blk=n_blk),
        grid=grid, in_specs=in_specs, out_specs=out_specs, out_shape=out_shape,
        compiler_params=_cparams("parallel"),
    )(*ins)
    names = ["qa", "kva", "kab", "vab", "qb", "kvn", "cmpk", "cmpv", "ksb", "vsb", "kvw", "kwb", "vwb", "gb", "gm"]
    out = dict(zip(names, res))
    if n_blk:
        out["kmean"] = res[-1].reshape(n // MOBA_BLOCK, W_A)
    return out


def _compress_math(load_main, load_halo, pos_ref, wa_ref, wb_ref, w2_ref, gk_ref, bd_ref, kc_ref, vc_ref, tc):
    bd1 = bd_ref[:LANES, :LANES]

    def half_proj(load, w_ref, half):
        outs = []
        for br in range(2):
            acc = None
            for l in range(CMP_STRIDE):
                xl = (load(br, l) + pos_ref[half, br, l]).astype(BF16)
                t = _dot(xl, w_ref[br, l])
                acc = t if acc is None else acc + t
            outs.append(acc)
        return outs

    a_k, a_v = half_proj(load_main, wa_ref, 0)
    b_k, b_v = half_proj(load_main, wb_ref, 1)
    bh_k, bh_v = half_proj(load_halo, wb_ref, 1)
    rows = lax.broadcasted_iota(jnp.int32, (tc, 2 * CMP_HID), 0)

    def shift_up(b, bh):
        rolled = pltpu.roll(b, tc - 1, 0)
        return jnp.where(rows == tc - 1, jnp.broadcast_to(bh[0:1], b.shape), rolled)

    hid_k = a_k + shift_up(b_k, bh_k)
    hid_v = a_v + shift_up(b_v, bh_v)
    hid_k = hid_k * jax.nn.sigmoid(hid_k)
    hid_v = hid_v * jax.nn.sigmoid(hid_v)
    out_k = _dot(hid_k.astype(BF16), w2_ref[0])
    out_v = _dot(hid_v.astype(BF16), w2_ref[1])
    kc_ref[0] = _head_norm(out_k, gk_ref[...], bd1).astype(BF16)
    vc_ref[0] = out_v.astype(BF16)


def _compress_kernel(xk_ref, xv_ref, hk_ref, hv_ref, *rest, tc):
    chunk_rows = lambda ref, n: (lambda l: ref[0, pl.ds(l, n, stride=CMP_STRIDE), :])
    main = (chunk_rows(xk_ref, tc), chunk_rows(xv_ref, tc))
    halo = (chunk_rows(hk_ref, 8), chunk_rows(hv_ref, 8))
    _compress_math(lambda br, l: main[br](l), lambda br, l: halo[br](l), *rest, tc)


CMP_PAGES = 16


def _compress_paged_kernel(pt_ref, *refs):
    del pt_ref
    page_refs = refs[:CMP_PAGES + 1]
    rest, (xk_sc, xv_sc) = refs[CMP_PAGES + 1:-2], refs[-2:]
    for g, page in enumerate(page_refs):
        rows = slice(g * PAGE_SIZE, (g + 1) * PAGE_SIZE)
        xk_sc[rows, :] = page[0, 0].reshape(W_KV, PAGE_SIZE).T
        xv_sc[rows, :] = page[0, 1].reshape(W_KV, PAGE_SIZE).T
    tc = CMP_PAGES * (PAGE_SIZE // CMP_STRIDE)
    sc = (xk_sc, xv_sc)
    _compress_math(lambda br, l: sc[br][pl.ds(l, tc, stride=CMP_STRIDE), :],
                   lambda br, l: sc[br][pl.ds(tc * CMP_STRIDE + l, 8, stride=CMP_STRIDE), :], *rest, tc)


def _compress_paged(page_table, nsa_t, wts):
    s, n_pages = page_table.shape
    g_pages = CMP_PAGES
    assert n_pages % g_pages == 0
    tc = g_pages * (PAGE_SIZE // CMP_STRIDE)
    nc = n_pages * (PAGE_SIZE // CMP_STRIDE)
    page = lambda k: pl.BlockSpec(
        (1, 2) + nsa_t.shape[2:], lambda b, i, pt: (pt[b, jnp.minimum(i * g_pages + k, n_pages - 1)], 0, 0, 0, 0))
    full = lambda a: pl.BlockSpec(a.shape, lambda b, i, pt: (0,) * a.ndim)
    consts = [wts["cpos"], wts["cwa"], wts["cwb"], wts["cw2"], wts["gkc"], wts["bd"]]
    rows_sc = pltpu.VMEM(((g_pages + 1) * PAGE_SIZE, LANES), F32)
    return pl.pallas_call(
        _compress_paged_kernel,
        grid_spec=pltpu.PrefetchScalarGridSpec(
            num_scalar_prefetch=1, grid=(s, n_pages // g_pages),
            in_specs=[page(k) for k in range(g_pages + 1)] + [full(a) for a in consts],
            out_specs=[pl.BlockSpec((1, tc, LANES), lambda b, i, pt: (b, i, 0))] * 2,
            scratch_shapes=[rows_sc, rows_sc]),
        out_shape=[jax.ShapeDtypeStruct((s, nc, LANES), BF16)] * 2,
        compiler_params=_cparams("parallel", "arbitrary"),
    )(page_table, *([nsa_t] * (g_pages + 1)), *consts)


def _compress(rows_k, rows_v, wts, tc):
    b, length, _ = rows_k.shape
    nc = length // CMP_STRIDE
    nt = nc // tc
    last_h = nc // 8 - 1
    full = lambda a: pl.BlockSpec(a.shape, lambda i, j: (0,) * a.ndim)
    main = pl.BlockSpec((1, tc * CMP_STRIDE, LANES), lambda i, j: (i, j, 0))
    halo = pl.BlockSpec((1, 8 * CMP_STRIDE, LANES), lambda i, j: (i, jnp.minimum((j + 1) * (tc // 8), last_h), 0))
    ins = [rows_k, rows_v, rows_k, rows_v, wts["cpos"], wts["cwa"], wts["cwb"], wts["cw2"], wts["gkc"], wts["bd"]]
    in_specs = [main, main, halo, halo] + [full(a) for a in ins[4:]]
    out_shape = [jax.ShapeDtypeStruct((b, nc, LANES), BF16)] * 2
    out_specs = [pl.BlockSpec((1, tc, LANES), lambda i, j: (i, j, 0))] * 2
    return pl.pallas_call(
        functools.partial(_compress_kernel, tc=tc),
        grid=(b, nt), in_specs=in_specs, out_specs=out_specs, out_shape=out_shape,
        compiler_params=_cparams("parallel", "arbitrary"),
    )(*ins)


def _flash_init(m_ref, l_ref, acc_ref):
    m_ref[...] = jnp.full(m_ref.shape, NEG_INF, F32)
    l_ref[...] = jnp.zeros(l_ref.shape, F32)
    acc_ref[...] = jnp.zeros(acc_ref.shape, F32)


def _flash_step(s, valid, v, m_ref, l_ref, acc_ref, i):
    s = jnp.where(valid, s, NEG_INF)
    m_old = m_ref[i]
    m_new = jnp.maximum(m_old, jnp.max(s, axis=1, keepdims=True))
    m_safe = jnp.where(m_new == NEG_INF, 0.0, m_new)
    alpha = jnp.exp(m_old - m_safe)
    p = jnp.exp(s - m_safe[:, :1])
    l_ref[i] = alpha * l_ref[i] + jnp.sum(p, axis=1, keepdims=True)
    acc_ref[i] = alpha * acc_ref[i] + _dot(p.astype(BF16), v)
    m_ref[i] = m_new


def _flash_out(l_ref, acc_ref, i):
    l = l_ref[i]
    return acc_ref[i] / jnp.where(l > 0, l, 1.0)


def _topk_mask(v, idx_iota, k):
    width = v.shape[1]

    def body(_, carry):
        v, sel = carry
        mx = jnp.max(v, axis=1, keepdims=True)
        idx = jnp.min(jnp.where(v == mx, idx_iota, width), axis=1, keepdims=True)
        hit = idx_iota == idx
        sel = jnp.where(hit & (mx > NEG_INF), 1.0, sel)
        return jnp.where(hit, NEG_INF, v), sel

    return lax.fori_loop(0, k, body, (v, jnp.zeros(v.shape, F32)))[1]


def _cmp_to_block_matrix(ncp, nsp, n_cmp, n_s):
    m = np.zeros((ncp, nsp), np.float32)
    per = SLC_BLOCK // CMP_STRIDE
    for i in range(n_s * per):
        for t in (i, i - 1):
            if 0 <= t < n_cmp:
                m[t, i // per] += 1.0
    return jnp.asarray(m)


MASK_BIG = 2.0 ** 30
AUX_R = LANES - 1


def _key_aux_table(t, block):
    assert t // block <= AUX_R and block <= 256
    key = np.arange(t)
    tab = np.zeros((t, LANES), np.float32)
    tab[key, key // block] = 1.0
    tab[:, AUX_R] = key % block
    return jnp.asarray(tab, BF16)


KEY_CHUNK = 64


def _flash_t_step(s_ref, p_ref, mask, v_t, m_ref, l_ref, acc_ref, i):
    n_keys, n_q = s_ref.shape
    chunks = [slice(r, r + KEY_CHUNK) for r in range(0, n_keys, KEY_CHUNK)]

    def scores(rows):
        s = s_ref[rows, :]
        if mask is not None:
            s = jnp.where(mask[rows, :], s, -MASK_BIG)
        return s.reshape(KEY_CHUNK // 8, 8, n_q)

    m_old = m_ref[i]
    m8 = None
    for rows in chunks:
        cm = jnp.max(scores(rows), axis=0)
        m8 = cm if m8 is None else jnp.maximum(m8, cm)
    m_new = jnp.maximum(m_old, jnp.max(m8, axis=0, keepdims=True))
    alpha = jnp.exp(m_old - m_new)
    l8 = None
    for rows in chunks:
        p = jnp.exp(scores(rows) - m_new[None])
        ps = jnp.sum(p, axis=0)
        l8 = ps if l8 is None else l8 + ps
        p_ref[rows, :] = p.reshape(KEY_CHUNK, n_q).astype(BF16)
    l_ref[i] = alpha * l_ref[i] + l8
    acc_ref[i] = alpha[0:1] * acc_ref[i] + _dot(v_t, p_ref[...])
    m_ref[i] = m_new


def _flash_t_out(l_ref, acc_ref, i):
    l = jnp.sum(l_ref[i], axis=0, keepdims=True)
    return acc_ref[i] / jnp.where(l > 0, l, 1.0)


def _topk_mask_t(v, idx, k):
    n = v.shape[0]

    def body(_, carry):
        v, sel = carry
        mx = jnp.max(v, axis=0, keepdims=True)
        first = jnp.min(jnp.where(v == mx, idx, n), axis=0, keepdims=True)
        hit = idx == first
        sel = jnp.where(hit & (mx > NEG_INF), 1.0, sel)
        return jnp.where(hit, NEG_INF, v), sel

    return lax.fori_loop(0, k, body, (v, jnp.zeros(v.shape, F32)))[1]


def _attn_prompt_kernel(qa_ref, qb_ref, gb_ref, km_ref, ka_ref, va_ref, kc_ref, vc_ref, ks_ref, vs_ref,
                        kw_ref, vw_ref, auxa_ref, auxs_ref, mm_ref, oa_ref, ob_ref,
                        m_sc, l_sc, acc_sc, oc_sc, os_sc, qx_sc, s_sc, p_sc, *, n_cmp, k_a, k_s):
    c = pl.program_id(1)
    tq = MOBA_BLOCK
    sub = lax.broadcasted_iota(jnp.int32, (LANES, tq), 0)
    is_lo = sub < HEAD_DIM
    is_r = sub == AUX_R
    key_r = lax.broadcasted_iota(jnp.int32, (tq, tq), 0)
    qry_r = lax.broadcasted_iota(jnp.int32, (tq, tq), 1)
    causal = key_r <= qry_r

    def masked_q(ref, tile, half):
        q = ref[0, tile * LANES:(tile + 1) * LANES, :]
        keep = is_lo if half == 0 else jnp.logical_not(is_lo)
        return jnp.where(keep, q, jnp.zeros_like(q))

    def set_query(i, q_t, aux):
        qx_sc[i, :LANES, :] = q_t
        qx_sc[i, LANES:, :] = aux.astype(BF16)

    def flash_init():
        m_sc[...] = jnp.full(m_sc.shape, NEG_INF, F32)
        l_sc[...] = jnp.zeros(l_sc.shape, F32)
        acc_sc[...] = jnp.zeros(acc_sc.shape, F32)

    def sweep(k_ref, k_lanes, aux_ref, v_ref, v_rows, heads, n, mask):
        koff = pl.multiple_of(n * tq, tq)
        kaux = aux_ref[pl.ds(koff, tq), :]
        n_buf = s_sc.shape[0]

        def stage_scores(i):
            kcat = jnp.concatenate([k_ref[0, pl.ds(koff, tq), k_lanes(i)], kaux], axis=1)
            s_sc[i % n_buf] = _dot(kcat, qx_sc[i])

        for i in heads[:n_buf - 1]:
            stage_scores(i)
        for pos, i in enumerate(heads):
            if pos + n_buf - 1 < len(heads):
                stage_scores(heads[pos + n_buf - 1])
            _flash_t_step(s_sc.at[i % n_buf], p_sc.at[i % 2], mask, v_ref[0, n, v_rows(i), :], m_sc, l_sc, acc_sc, i)

    def past_loop(block_fn):
        def body(n, carry):
            block_fn(n)
            return carry
        lax.fori_loop(0, c, body, 0)

    blk = sub
    back = (c - blk).astype(F32)
    for h in range(H_A):
        j, half = divmod(h, 2)
        q_t = masked_q(qa_ref, j, half)
        gate = _dot(km_ref[0, :, j * LANES:(j + 1) * LANES].astype(BF16), q_t)
        sel = _topk_mask_t(jnp.where(blk < c, gate, NEG_INF), blk, k_a)
        bias = jnp.where(sel > 0.5, 0.0, -MASK_BIG) - (SLOPES_A[h] * tq) * back
        set_query(h, q_t, jnp.where(is_r, SLOPES_A[h], jnp.where(blk < c, bias, 0.0)))

    pair_lanes = lambda i: slice((i // 2) * LANES, (i // 2 + 1) * LANES)
    all_lanes = lambda i: slice(0, LANES)
    heads = range(H_A)
    flash_init()
    sweep(ka_ref, pair_lanes, auxa_ref, va_ref, pair_lanes, heads, c, causal)
    past_loop(lambda n: sweep(ka_ref, pair_lanes, auxa_ref, va_ref, pair_lanes, heads, n, None))
    for j in range(H_A // 2):
        o = jnp.where(is_lo, _flash_t_out(l_sc, acc_sc, 2 * j), _flash_t_out(l_sc, acc_sc, 2 * j + 1))
        oa_ref[0, j * LANES:(j + 1) * LANES, :] = o.astype(BF16)

    ncp = kc_ref.shape[1]
    nsp = mm_ref.shape[0]
    jc = lax.broadcasted_iota(jnp.int32, (ncp, tq), 0)
    d_c = c * tq + lax.broadcasted_iota(jnp.int32, (ncp, tq), 1) - (jc * CMP_STRIDE + (CMP_LEN - 1))
    c_valid = (d_c >= 0) & (jc < n_cmp)
    d_cf = d_c.astype(F32)
    sblk = lax.broadcasted_iota(jnp.int32, (nsp, tq), 0)
    cur_s = (c * tq + lax.broadcasted_iota(jnp.int32, (nsp, tq), 1)) // SLC_BLOCK
    forced = (sblk == 0) | (sblk >= cur_s - 1)
    causal_s = sblk <= cur_s
    rel_s = (sblk - c * (tq // SLC_BLOCK)).astype(F32)
    for k in range(H_KV):
        imp = jnp.zeros((ncp, tq), F32)
        for g in range(GROUP):
            i = k * GROUP + g
            s = _dot(kc_ref[0], masked_q(qb_ref, g, k)) - SLOPES_B[i] * d_cf
            s = jnp.where(c_valid, s, NEG_INF)
            m = jnp.max(s, axis=0, keepdims=True)
            m = jnp.where(m == NEG_INF, 0.0, m)
            e = jnp.where(c_valid, jnp.exp(s - m), 0.0)
            d = jnp.sum(e, axis=0, keepdims=True)
            p = e / jnp.where(d > 0, d, 1.0)
            imp = imp + p
            oc_sc[i] = _dot(vc_ref[0], p.astype(BF16))
        impb = jnp.dot(mm_ref[...], imp, preferred_element_type=F32, precision=lax.Precision.HIGHEST)
        impb = jnp.where(forced, impb + FORCE_BONUS, impb)
        impb = jnp.where(causal_s, impb, NEG_INF)
        sel = _topk_mask_t(impb, sblk, k_s)
        for g in range(GROUP):
            i = k * GROUP + g
            bias = jnp.where(sel > 0.5, 0.0, -MASK_BIG) + (SLOPES_B[i] * SLC_BLOCK) * rel_s
            set_query(i, masked_q(qb_ref, g, k), jnp.where(is_r, SLOPES_B[i], bias))

    heads = range(H_B)
    flash_init()
    sweep(ks_ref, all_lanes, auxs_ref, vs_ref, all_lanes, heads, c, causal)
    past_loop(lambda n: sweep(ks_ref, all_lanes, auxs_ref, vs_ref, all_lanes, heads, n, None))
    for i in range(H_B):
        os_sc[i] = _flash_t_out(l_sc, acc_sc, i)

    for i in range(H_B):
        qx_sc[i, LANES:, :] = jnp.where(is_r, SLOPES_B[i], -(SLOPES_B[i] * tq) * back).astype(BF16)
    flash_init()
    sweep(kw_ref, all_lanes, auxa_ref, vw_ref, all_lanes, heads, c, causal)
    n_back = WINDOW // tq
    for b in range(1, n_back + 1):
        mask = None if b < n_back else key_r > qry_r
        pl.when(c >= b)(functools.partial(sweep, kw_ref, all_lanes, auxa_ref, vw_ref, all_lanes, heads, c - b, mask))

    gb = gb_ref[0]
    for j in range(GROUP):
        def comb(i):
            return (gb[3 * i:3 * i + 1] * oc_sc[i] + gb[3 * i + 1:3 * i + 2] * os_sc[i]
                    + gb[3 * i + 2:3 * i + 3] * _flash_t_out(l_sc, acc_sc, i))
        ob_ref[0, j * LANES:(j + 1) * LANES, :] = jnp.where(is_lo, comb(j), comb(GROUP + j)).astype(BF16)


def _attn_prompt(p, kc, vc, b, t):
    tq = MOBA_BLOCK
    nb = t // tq
    n_s = t // SLC_BLOCK
    n_cmp = t // CMP_STRIDE - 1
    nsp = _round_up(n_s, LANES)
    ncp = kc.shape[1]
    assert nb <= AUX_R and n_s <= AUX_R and nsp == LANES and WINDOW % tq == 0
    km = jnp.pad(p["kmean"].reshape(b, nb, W_A), ((0, 0), (0, LANES - nb), (0, 0)))
    mm = _cmp_to_block_matrix(ncp, nsp, n_cmp, n_s).T
    r3 = lambda a: a.reshape(b, t, a.shape[-1])
    tr = lambda a: r3(a).transpose(0, 2, 1)
    qtile = lambda w: pl.BlockSpec((1, w, tq), lambda i, c: (i, 0, c))
    trb = lambda a: a.reshape(b, nb, tq, a.shape[-1]).transpose(0, 1, 3, 2)
    seq = lambda a: pl.BlockSpec((1,) + a.shape[1:], lambda i, c: (i,) + (0,) * (a.ndim - 1))
    const = lambda a: pl.BlockSpec(a.shape, lambda i, c: (0, 0))
    ins = [tr(p["qa"]), tr(p["qb"]), tr(p["gb"]), km, r3(p["kab"]), trb(p["vab"]), kc, vc.transpose(0, 2, 1),
           r3(p["ksb"]), trb(p["vsb"]), r3(p["kwb"]), trb(p["vwb"]),
           _key_aux_table(t, MOBA_BLOCK), _key_aux_table(t, SLC_BLOCK), mm]
    in_specs = [qtile(W_A), qtile(W_B), qtile(LANES)] + [seq(a) for a in ins[3:12]] + [const(a) for a in ins[12:]]
    out_shape = [jax.ShapeDtypeStruct((b, W_A, t), BF16), jax.ShapeDtypeStruct((b, W_B, t), BF16)]
    out_specs = [qtile(W_A), qtile(W_B)]
    scratch = [
        pltpu.VMEM((H_A, 8, tq), F32), pltpu.VMEM((H_A, 8, tq), F32), pltpu.VMEM((H_A, LANES, tq), F32),
        pltpu.VMEM((H_B, LANES, tq), F32), pltpu.VMEM((H_B, LANES, tq), F32), pltpu.VMEM((H_A, 2 * LANES, tq), BF16),
        pltpu.VMEM((6, tq, tq), F32), pltpu.VMEM((2, tq, tq), BF16),
    ]
    oa, ob = pl.pallas_call(
        functools.partial(_attn_prompt_kernel, n_cmp=n_cmp, k_a=min(MOBA_TOPK, nb), k_s=min(SLC_TOPK, n_s)),
        grid=(b, nb), in_specs=in_specs, out_specs=out_specs, out_shape=out_shape, scratch_shapes=scratch,
        compiler_params=_cparams("parallel", "arbitrary"),
    )(*ins)
    return oa.transpose(0, 2, 1).reshape(b * t, W_A), ob.transpose(0, 2, 1).reshape(b * t, W_B)


MOBA_SAMPLE_PAGES = 8


def _moba_sample_tables(t_new):
    rows = H_A * t_new
    head = np.arange(rows) // t_new
    qidx = np.arange(rows) % t_new
    slope = np.asarray(SLOPES_A, np.float32)[head]
    b0 = (slope[:, None] * np.arange(PAGE_SIZE)[None, :]).astype(np.float32)
    coln = np.arange(t_new * H_A)
    ok = (head[:, None] == (coln % H_A)[None, :]) & ((coln // H_A)[None, :] <= qidx[:, None])
    bnew = np.where(ok, slope[:, None] * (coln // H_A)[None, :], -MASK_BIG).astype(np.float32)
    own = (head[:, None] == (np.arange(W_A) // HEAD_DIM)[None, :]).astype(np.float32)
    slope_l = np.broadcast_to(slope[:, None], (rows, LANES)).astype(np.float32)
    return jnp.asarray(b0), jnp.asarray(bnew), jnp.asarray(own), jnp.asarray(slope_l)


def _moba_sample_kernel(pt_ref, qbd_ref, q_ref, b0_ref, bnew_ref, own_ref, slope_ref, new_ref, *rest, n_pages, k_a):
    g_pages = MOBA_SAMPLE_PAGES
    page_refs, o_ref = rest[:g_pages], rest[g_pages]
    g_sc, m_sc, l_sc, o_sc = rest[g_pages + 1:]
    del pt_ref
    step = pl.program_id(1)
    qbd = qbd_ref[0]
    rows = qbd.shape[0]
    slope = slope_ref[:, 0:1]

    def softmax_partial(s):
        m = jnp.max(s, axis=1, keepdims=True)
        e = jnp.exp(s - m)
        return m, jnp.sum(e, axis=1, keepdims=True), e.astype(BF16)

    for g in range(g_pages):
        p = step * g_pages + g
        k_t = page_refs[g][0, 0].reshape(W_A, PAGE_SIZE).astype(BF16)
        v_t = page_refs[g][0, 1].reshape(W_A, PAGE_SIZE).astype(BF16)
        qk = _dot(qbd, k_t)
        g_sc[p] = jnp.broadcast_to(jnp.sum(qk, axis=1, keepdims=True), (rows, LANES))
        m, l, e = softmax_partial(qk + b0_ref[...] + slope * ((p - n_pages) * PAGE_SIZE).astype(F32))
        m_sc[p] = jnp.broadcast_to(m, (rows, LANES))
        l_sc[p] = jnp.broadcast_to(l, (rows, LANES))
        o_sc[p] = _dot_t(e, v_t)

    @pl.when(step == pl.num_programs(1) - 1)
    def _():
        per_blk = MOBA_BLOCK // PAGE_SIZE
        nb = n_pages // per_blk
        gate = jnp.sum(g_sc[...].reshape(nb, per_blk, rows, LANES), axis=1)
        sel = _topk_mask_t(gate, lax.broadcasted_iota(jnp.int32, gate.shape, 0), k_a)
        sel = jnp.broadcast_to(sel[:, None], (nb, per_blk, rows, LANES)).reshape(n_pages, rows, LANES) > 0.5
        t_new = new_ref.shape[1]
        k_new = new_ref[0, :, 0].reshape(t_new * H_A, HEAD_DIM).astype(BF16)
        v_new = new_ref[0, :, 1].reshape(t_new * H_A, HEAD_DIM).astype(BF16)
        m_o, l_o, e_o = softmax_partial(_dot_t(q_ref[0], k_new) + bnew_ref[...])
        o_o = _dot(e_o, v_new)
        m_all = m_sc[...]
        m_star = jnp.maximum(jnp.max(jnp.where(sel, m_all, -MASK_BIG), axis=0), m_o)
        w = jnp.where(sel, jnp.exp(m_all - m_star[None]), 0.0)
        w_o = jnp.exp(m_o - m_star)
        l_star = jnp.sum(w * l_sc[...], axis=0) + w_o * l_o
        n_rep = W_A // LANES
        w_wide = jnp.concatenate([w] * n_rep, axis=2)
        o_star = jnp.sum(w_wide * o_sc[...], axis=0) + (
            jnp.concatenate([w_o[:, :HEAD_DIM]] * H_A, axis=1) * jnp.concatenate([o_o] * H_A, axis=1))
        o_ref[0] = (o_star / jnp.concatenate([l_star] * n_rep, axis=1)) * own_ref[...]


def _moba_sample(p, cache_moba, page_table, s, t_new):
    n_pages = page_table.shape[1]
    g_pages = MOBA_SAMPLE_PAGES
    assert n_pages % g_pages == 0 and n_pages % (MOBA_BLOCK // PAGE_SIZE) == 0
    rows = H_A * t_new
    nb = n_pages // (MOBA_BLOCK // PAGE_SIZE)
    cache_t = cache_moba.transpose(0, 2, 3, 4, 1)
    q4 = p["qa"].reshape(s, t_new, H_A, HEAD_DIM).transpose(0, 2, 1, 3)
    q = q4.reshape(s, rows, HEAD_DIM)
    eye = jnp.eye(H_A, dtype=q4.dtype)
    qbd = (q4[:, :, :, None, :] * eye[None, :, None, :, None]).reshape(s, rows, W_A)
    new = p["kva"].reshape(s, t_new, 2, H_A, HEAD_DIM)
    tables = _moba_sample_tables(t_new)
    const = lambda a: pl.BlockSpec(a.shape, lambda b, i, pt: (0, 0))
    seq = lambda a: pl.BlockSpec((1,) + a.shape[1:], lambda b, i, pt: (b,) + (0,) * (a.ndim - 1))
    page = lambda g: pl.BlockSpec((1,) + cache_t.shape[1:], lambda b, i, pt: (pt[b, i * g_pages + g], 0, 0, 0, 0))
    out = pl.pallas_call(
        functools.partial(_moba_sample_kernel, n_pages=n_pages, k_a=min(MOBA_TOPK, nb)),
        grid_spec=pltpu.PrefetchScalarGridSpec(
            num_scalar_prefetch=1, grid=(s, n_pages // g_pages),
            in_specs=[seq(qbd), seq(q)] + [const(a) for a in tables] + [seq(new)] + [page(g) for g in range(g_pages)],
            out_specs=pl.BlockSpec((1, rows, W_A), lambda b, i, pt: (b, 0, 0)),
            scratch_shapes=[pltpu.VMEM((n_pages, rows, LANES), F32)] * 3 + [pltpu.VMEM((n_pages, rows, W_A), F32)]),
        out_shape=jax.ShapeDtypeStruct((s, rows, W_A), F32),
        compiler_params=_cparams("parallel", "arbitrary"),
    )(page_table, qbd, q, *tables, new, *([cache_t] * g_pages))
    return out.reshape(s, H_A, t_new, W_A).sum(axis=1).reshape(s * t_new, W_A).astype(BF16)


SAMPLE_KV_TILE = 2048
SAMPLE_KV_PAGES = SAMPLE_KV_TILE // PAGE_SIZE


def _attn_sample_kernel(pt_ref, qb16_ref, g16_ref, kc_ref, vc_ref, mm_ref, ksn_ref, vsn_ref, kwa_ref, vwa_ref, *rest,
                        past, t_new, k_s):
    del pt_ref
    page_refs = rest[:SAMPLE_KV_PAGES]
    ob_ref, mb_sc, lb_sc, accb_sc, sels_sc, oc_sc, os_sc = rest[SAMPLE_KV_PAGES:]
    kt = pl.program_id(1)
    r_tile = SAMPLE_KV_TILE
    rb = GROUP * t_new
    row_b = lax.broadcasted_iota(jnp.int32, (rb, 1), 0)
    t_b = past + row_b % t_new
    slope_b = []
    for k in range(H_KV):
        sl = jnp.full((rb, 1), SLOPES_B[k * GROUP + GROUP - 1], F32)
        for g in range(GROUP - 2, -1, -1):
            sl = jnp.where(row_b < (g + 1) * t_new, SLOPES_B[k * GROUP + g], sl)
        slope_b.append(sl)
    nsp = mm_ref.shape[1]
    cur_s = past // SLC_BLOCK

    @pl.when(kt == 0)
    def _():
        _flash_init(mb_sc, lb_sc, accb_sc)
        nc = kc_ref.shape[1]
        jc = lax.broadcasted_iota(jnp.int32, (rb, nc), 1)
        d_c = t_b - (jc * CMP_STRIDE + (CMP_LEN - 1))
        c_valid = d_c >= 0
        d_cf = d_c.astype(F32)
        sj = lax.broadcasted_iota(jnp.int32, (rb, nsp), 1)
        forced = (sj == 0) | (sj >= cur_s - 1)
        causal_s = sj <= cur_s
        gr = lax.broadcasted_iota(jnp.int32, (rb, rb), 0) % t_new
        gc = lax.broadcasted_iota(jnp.int32, (rb, rb), 1) % t_new
        group_sum = (gr == gc).astype(F32)
        for k in range(H_KV):
            s = _dot_t(qb16_ref[0, k], kc_ref[0]) - slope_b[k] * d_cf
            s = jnp.where(c_valid, s, NEG_INF)
            m = jnp.max(s, axis=1, keepdims=True)
            m = jnp.where(m == NEG_INF, 0.0, m)
            e = jnp.where(c_valid, jnp.exp(s - m), 0.0)
            d = jnp.sum(e, axis=1, keepdims=True)
            p = e / jnp.where(d > 0, d, 1.0)
            oc_sc[k] = _dot(p.astype(BF16), vc_ref[0])
            pb = jnp.dot(p, mm_ref[...], preferred_element_type=F32, precision=lax.Precision.HIGHEST)
            impb = jnp.dot(group_sum, pb, preferred_element_type=F32, precision=lax.Precision.HIGHEST)
            impb = jnp.where(forced, impb + FORCE_BONUS, impb)
            impb = jnp.where(causal_s, impb, NEG_INF)
            sels_sc[k] = _topk_mask(impb, sj, k_s)

    blocks_per_tile = r_tile // SLC_BLOCK
    first_blk = kt * blocks_per_tile
    lane_tile = pl.multiple_of((first_blk // LANES) * LANES, LANES)
    e_row = lax.broadcasted_iota(jnp.int32, (LANES, r_tile), 0)
    e_col = lax.broadcasted_iota(jnp.int32, (LANES, r_tile), 1) // SLC_BLOCK
    expand = (e_row == first_blk % LANES + e_col).astype(BF16)
    col_t = lax.broadcasted_iota(jnp.int32, (rb, r_tile), 1)
    dist_t = (t_b - (kt * r_tile + col_t)).astype(F32)
    k_pages = [pg[0, 0].reshape(W_KV, PAGE_SIZE).astype(BF16) for pg in page_refs]
    v_pages = [pg[0, 1].reshape(W_KV, PAGE_SIZE).astype(BF16) for pg in page_refs]
    for k in range(H_KV):
        sel_t = sels_sc[k, :, pl.ds(lane_tile, LANES)]
        valid = _dot(sel_t.astype(BF16), expand) > 0.5
        s = jnp.concatenate([_dot(qb16_ref[0, k], k_t) for k_t in k_pages], axis=1) - slope_b[k] * dist_t
        s = jnp.where(valid, s, NEG_INF)
        m_old = mb_sc[k]
        m_new = jnp.maximum(m_old, jnp.max(s, axis=1, keepdims=True))
        m_safe = jnp.where(m_new == NEG_INF, 0.0, m_new)
        alpha = jnp.exp(m_old - m_safe)
        p = jnp.exp(s - m_safe[:, :1])
        lb_sc[k] = alpha * lb_sc[k] + jnp.sum(p, axis=1, keepdims=True)
        pv = None
        for g, v_t in enumerate(v_pages):
            t = _dot_t(p[:, g * PAGE_SIZE:(g + 1) * PAGE_SIZE].astype(BF16), v_t)
            pv = t if pv is None else pv + t
        accb_sc[k] = alpha * accb_sc[k] + pv
        mb_sc[k] = m_new

    @pl.when(kt == pl.num_programs(1) - 1)
    def _():
        dist_b = row_b % t_new - lax.broadcasted_iota(jnp.int32, (rb, LANES), 1)
        sj = lax.broadcasted_iota(jnp.int32, (rb, nsp), 1)
        for k in range(H_KV):
            own = jnp.sum(jnp.where(sj == cur_s, sels_sc[k], 0.0), axis=1, keepdims=True) > 0.5
            s = _dot_t(qb16_ref[0, k], ksn_ref[0]) - slope_b[k] * dist_b.astype(F32)
            _flash_step(s, own & (dist_b >= 0), vsn_ref[0], mb_sc, lb_sc, accb_sc, k)
            os_sc[k] = _flash_out(lb_sc, accb_sc, k)
        _flash_init(mb_sc, lb_sc, accb_sc)
        n_w = kwa_ref.shape[1]
        dist_w = WINDOW + row_b % t_new - lax.broadcasted_iota(jnp.int32, (rb, n_w), 1)
        valid_w = (dist_w >= 0) & (dist_w < WINDOW)
        lane_b = lax.broadcasted_iota(jnp.int32, (rb, LANES), 1)
        comb = []
        for k in range(H_KV):
            s = _dot_t(qb16_ref[0, k], kwa_ref[0]) - slope_b[k] * dist_w.astype(F32)
            _flash_step(s, valid_w, vwa_ref[0], mb_sc, lb_sc, accb_sc, k)
            g = g16_ref[0, k]
            comb.append(g[:, 0:1] * oc_sc[k] + g[:, 1:2] * os_sc[k] + g[:, 2:3] * _flash_out(lb_sc, accb_sc, k))
        ob_ref[0] = jnp.where(lane_b < HEAD_DIM, comb[0], comb[1])


def _attn_sample(p, nsa_t, page_table, kc, vc, state_win, s, t_new, past):
    r_tile = SAMPLE_KV_TILE
    g_pages = SAMPLE_KV_PAGES
    assert past % r_tile == 0 and state_win.shape[1] == WINDOW and LANES % (r_tile // SLC_BLOCK) == 0
    lp = _round_up(past + t_new, MOBA_BLOCK)
    n_s = lp // SLC_BLOCK
    n_cmp = lp // CMP_STRIDE - 1
    nsp = _round_up(n_s, LANES)
    nc = kc.shape[1]
    mm = _cmp_to_block_matrix(nc, nsp, min(n_cmp, nc), n_s)
    lane = jnp.arange(LANES)
    lo = (lane < HEAD_DIM)

    qb = p["qb"].reshape(s, t_new, GROUP, LANES).transpose(0, 2, 1, 3).reshape(s, 1, GROUP * t_new, LANES)
    zero = jnp.zeros_like(qb)
    qb16 = jnp.concatenate([jnp.where(lo, qb, zero), jnp.where(lo, zero, qb)], axis=1)
    g16 = p["gb"][:, :3 * H_B].reshape(s, t_new, H_KV, GROUP, 3).transpose(0, 2, 3, 1, 4)
    g16 = jnp.pad(g16.reshape(s, H_KV, GROUP * t_new, 3), ((0, 0), (0, 0), (0, 0), (0, LANES - 3)))

    pad_new = lambda a: jnp.pad(a.reshape(s, t_new, a.shape[-1]), ((0, 0), (0, LANES - t_new), (0, 0)))
    win = state_win.reshape(s, WINDOW, 2, W_KV).astype(BF16)
    kwa = jnp.concatenate([win[:, :, 0], pad_new(p["kwb"])], axis=1)
    vwa = jnp.concatenate([win[:, :, 1], pad_new(p["vwb"])], axis=1)

    ins = [qb16, g16, kc, vc, mm, pad_new(p["ksb"]), pad_new(p["vsb"]), kwa, vwa]
    seq = lambda a: pl.BlockSpec((1,) + a.shape[1:], lambda i, k, pt: (i,) + (0,) * (a.ndim - 1))
    page = lambda g: pl.BlockSpec((1, 2) + nsa_t.shape[2:], lambda i, k, pt: (pt[i, k * g_pages + g], 1, 0, 0, 0))
    in_specs = ([seq(a) for a in ins[:4]] + [pl.BlockSpec(mm.shape, lambda i, k, pt: (0, 0))]
                + [seq(a) for a in ins[5:]] + [page(g) for g in range(g_pages)])
    rb = GROUP * t_new
    scratch = [
        pltpu.VMEM((H_KV, rb, LANES), F32), pltpu.VMEM((H_KV, rb, LANES), F32), pltpu.VMEM((H_KV, rb, LANES), F32),
        pltpu.VMEM((H_KV, rb, nsp), F32), pltpu.VMEM((H_KV, rb, LANES), F32), pltpu.VMEM((H_KV, rb, LANES), F32),
    ]
    ob16 = pl.pallas_call(
        functools.partial(_attn_sample_kernel, past=past, t_new=t_new, k_s=min(SLC_TOPK, n_s)),
        grid_spec=pltpu.PrefetchScalarGridSpec(
            num_scalar_prefetch=1, grid=(s, past // r_tile), in_specs=in_specs,
            out_specs=pl.BlockSpec((1, rb, LANES), lambda i, k, pt: (i, 0, 0)), scratch_shapes=scratch),
        out_shape=jax.ShapeDtypeStruct((s, rb, LANES), F32),
        compiler_params=_cparams("parallel", "arbitrary"),
    )(page_table, *ins, *([nsa_t] * g_pages))
    return ob16.reshape(s, GROUP, t_new, LANES).transpose(0, 2, 1, 3).reshape(s * t_new, W_B).astype(BF16)


def _post_kernel(oa_ref, ob_ref, gm_ref, x_ref, wba_ref, wbb_ref, wout_ref, gnf_ref, wg_ref, wu_ref, wd_ref,
                 y_ref, h_sc, hn_sc, acc_sc):
    j = pl.program_id(1)

    @pl.when(j == 0)
    def _():
        ma = _dot(oa_ref[...], wba_ref[...])
        mb = _dot(ob_ref[...], wbb_ref[...])
        merged = gm_ref[:, :D_MODEL] * ma + gm_ref[:, D_MODEL:] * mb
        h = x_ref[...] + _dot(merged.astype(BF16), wout_ref[...])
        h_sc[...] = h
        r = lax.rsqrt(jnp.mean(h * h, axis=-1, keepdims=True) + RMS_EPS)
        hn_sc[...] = ((h * r) * gnf_ref[...]).astype(BF16)
        acc_sc[...] = jnp.zeros(acc_sc.shape, F32)

    hn = hn_sc[...]
    g = _dot(hn, wg_ref[...])
    u = _dot(hn, wu_ref[...])
    act = (g * jax.nn.sigmoid(g)) * u
    acc_sc[...] += _dot(act.astype(BF16), wd_ref[...])

    @pl.when(j == pl.num_programs(1) - 1)
    def _():
        y_ref[...] = h_sc[...] + acc_sc[...]


def _post(oa, ob, gm, x, wts):
    n = x.shape[0]
    tm = 512 if n % 512 == 0 else n
    d_ff = wts["wd"].shape[0]
    n_ff = 2 if (d_ff // 2) % LANES == 0 else 1
    fc = d_ff // n_ff
    rows = lambda w: pl.BlockSpec((tm, w), lambda i, j: (i, 0))
    full = lambda a: pl.BlockSpec(a.shape, lambda i, j: (0,) * a.ndim)
    ins = [oa, ob, gm, x, wts["wba"], wts["wbb"], wts["wout"], wts["gnf"], wts["wgate"], wts["wup"], wts["wd"]]
    in_specs = [rows(W_A), rows(W_B), rows(2 * D_MODEL), rows(D_MODEL)] + [full(a) for a in ins[4:8]] + [
        pl.BlockSpec((D_MODEL, fc), lambda i, j: (0, j)),
        pl.BlockSpec((D_MODEL, fc), lambda i, j: (0, j)),
        pl.BlockSpec((fc, D_MODEL), lambda i, j: (j, 0)),
    ]
    return pl.pallas_call(
        _post_kernel, grid=(n // tm, n_ff), in_specs=in_specs, out_specs=rows(D_MODEL),
        out_shape=jax.ShapeDtypeStruct((n, D_MODEL), F32),
        scratch_shapes=[pltpu.VMEM((tm, D_MODEL), F32), pltpu.VMEM((tm, D_MODEL), BF16), pltpu.VMEM((tm, D_MODEL), F32)],
        compiler_params=_cparams("parallel", "arbitrary"),
    )(*ins)


def _blockdiag2(w):
    z = jnp.zeros_like(w)
    return jnp.concatenate([jnp.concatenate([w, z], axis=-1), jnp.concatenate([z, w], axis=-1)], axis=-2)


def _prep_weights(g_na, w_in, b_in, g_qk_a, g_qk_b, cmp_pos, cmp_w1, cmp_w2, w_ba, w_bb, w_out, g_nf, w_up, w_down):
    def perm_qb_cols(a):
        lead = a.shape[:-1]
        return a.reshape(lead + (H_KV, GROUP, HEAD_DIM)).swapaxes(-3, -2).reshape(lead + (W_B,))

    def main_cols(a):
        return jnp.concatenate([a[..., :C_QB], perm_qb_cols(a[..., C_QB:C_KVB]), a[..., C_KVB:C_END]], axis=-1)

    ones = lambda n: jnp.ones((n,), F32)
    gv = jnp.concatenate([
        jnp.tile(g_qk_a[0], H_A), jnp.tile(g_qk_a[1], H_A), ones(W_A), jnp.tile(g_qk_b[0], H_B),
        ones(2 * W_KV), jnp.tile(g_qk_b[2], H_KV), ones(W_KV), jnp.tile(g_qk_b[3], H_KV), ones(W_KV)])
    n_gb = 3 * H_B
    hd_idx = np.arange(W_A) // HEAD_DIM
    bd = jnp.asarray((hd_idx[:, None] == hd_idx[None, :]).astype(np.float32)).astype(BF16)
    d_ff = w_down.shape[0]
    w1 = cmp_w1.astype(BF16)
    pos2 = jnp.concatenate([cmp_pos, cmp_pos], axis=-1)
    return {
        "gna": g_na.reshape(1, D_MODEL),
        "w1": main_cols(w_in).astype(BF16), "b1": main_cols(b_in).reshape(1, C_END), "gv": gv.reshape(1, C_END),
        "wg": jnp.pad(w_in[:, C_END:C_END + n_gb], ((0, 0), (0, LANES - n_gb))).astype(BF16),
        "bg": jnp.pad(b_in[C_END:C_END + n_gb], (0, LANES - n_gb)).reshape(1, LANES),
        "wgm": w_in[:, C_END + n_gb:].astype(BF16), "bgm": b_in[C_END + n_gb:].reshape(1, 2 * D_MODEL),
        "bd": bd,
        "cpos": pos2.reshape(2, 2, CMP_STRIDE, 1, LANES).swapaxes(0, 1),
        "cwa": _blockdiag2(w1[:, :CMP_STRIDE]), "cwb": _blockdiag2(w1[:, CMP_STRIDE:]),
        "cw2": _blockdiag2(cmp_w2.astype(BF16)),
        "gkc": jnp.tile(g_qk_b[1], H_KV).reshape(1, LANES),
        "wba": w_ba.astype(BF16),
        "wbb": w_bb.reshape(H_KV, GROUP, HEAD_DIM, D_MODEL).swapaxes(0, 1).reshape(W_B, D_MODEL).astype(BF16),
        "wout": w_out.astype(BF16), "gnf": g_nf.reshape(1, D_MODEL),
        "wgate": w_up[:, :d_ff].astype(BF16), "wup": w_up[:, d_ff:].astype(BF16), "wd": w_down.astype(BF16),
    }


def _prompt_layer(x, wts):
    b, t, _ = x.shape
    x2 = x.reshape(b * t, D_MODEL)
    p = _inproj(x2, wts, with_kmean=True)
    nc = t // CMP_STRIDE
    kc, vc = _compress(p["cmpk"].reshape(b, t, W_KV), p["cmpv"].reshape(b, t, W_KV), wts, tc=nc)
    oa, ob = _attn_prompt(p, kc, vc, b, t)
    y = _post(oa, ob, p["gm"], x2, wts)
    keep = min(WINDOW, t)
    return (y.reshape(b, t, D_MODEL),
            p["kva"].reshape(b, t, 2, H_A, HEAD_DIM),
            p["kvn"].reshape(b, t, 4, H_KV, HEAD_DIM),
            p["kvw"].reshape(b, t, 2, H_KV, HEAD_DIM)[:, t - keep:])


def _sample_layer(x, cache_moba, cache_nsa, state_win, page_table, wts):
    s, t_new, _ = x.shape
    past = page_table.shape[1] * PAGE_SIZE
    x2 = x.reshape(s * t_new, D_MODEL)
    p = _inproj(x2, wts, with_kmean=False)
    nsa_t = cache_nsa.transpose(0, 2, 3, 4, 1)
    kc, vc = _compress_paged(page_table, nsa_t, wts)
    oa = _moba_sample(p, cache_moba, page_table, s, t_new)
    ob = _attn_sample(p, nsa_t, page_table, kc, vc, state_win, s, t_new, past)
    y = _post(oa, ob, p["gm"], x2, wts)
    kvw_new = p["kvw"].reshape(s, t_new, 2, H_KV, HEAD_DIM)
    win = jnp.concatenate([state_win, kvw_new], axis=1)[:, t_new:]
    return (y.reshape(s, t_new, D_MODEL),
            p["kva"].reshape(s, t_new, 2, H_A, HEAD_DIM),
            p["kvn"].reshape(s, t_new, 4, H_KV, HEAD_DIM),
            win)


def kernel(x_prompt, x_sample, cache_moba_kv, cache_nsa_kv, state_win_kv, page_table, g_norm_attn, w_in, b_in,
           g_qk_moba, g_qk_nsa, cmp_pos, cmp_w1, cmp_w2, w_br_moba, w_br_nsa, w_out, g_norm_ffn, w_up, w_down):
    assert g_norm_attn.shape[0] == 1, "single-layer step"
    wts = _prep_weights(g_norm_attn[0], w_in[0], b_in[0], g_qk_moba[0], g_qk_nsa[0], cmp_pos[0], cmp_w1[0],
                        cmp_w2[0], w_br_moba[0], w_br_nsa[0], w_out[0], g_norm_ffn[0], w_up[0], w_down[0])
    y_p, a_p, n_p, wn_p = _prompt_layer(x_prompt, wts)
    y_s, a_s, n_s, wn_s = _sample_layer(x_sample, cache_moba_kv[0], cache_nsa_kv[0], state_win_kv[0], page_table, wts)
    return (y_p, y_s, a_p[None], n_p[None], wn_p[None], a_s[None], n_s[None], wn_s[None])
```

```python
import functools
import math

import numpy as np
import jax
import jax.numpy as jnp
from jax import lax
from jax.experimental import pallas as pl
from jax.experimental.pallas import tpu as pltpu

F32 = jnp.float32
BF16 = jnp.bfloat16

D_MODEL = 1024
PAGE_SIZE = 128
HEAD_DIM = 64
H_A = 8
H_B = 8
H_KV = 2
GROUP = H_B // H_KV
MOBA_BLOCK = 256
MOBA_TOPK = 3
CMP_LEN = 32
CMP_STRIDE = 16
SLC_BLOCK = 64
SLC_TOPK = 16
WINDOW = 512
CMP_HID = 2 * HEAD_DIM
FORCE_BONUS = 1e4
RMS_EPS = 1e-6
W_A = H_A * HEAD_DIM
W_B = H_B * HEAD_DIM
W_KV = H_KV * HEAD_DIM
QK_SCALE = HEAD_DIM ** -0.5

LANES = 128
VMEM_LIMIT = 56 * 1024 * 1024
NEG_INF = float("-inf")

SLOPES_A = tuple(2.0 ** (-8.0 * (i + 1) / H_A) for i in range(H_A))
SLOPES_B = tuple(2.0 ** (-8.0 * (i + 1) / H_B) for i in range(H_B))


def _round_up(x, m):
    return -(-x // m) * m


def _cparams(*sem):
    return pltpu.CompilerParams(dimension_semantics=sem, vmem_limit_bytes=VMEM_LIMIT)


def _dot(a, b):
    return jnp.dot(a, b, preferred_element_type=F32)


def _dot_t(a, b):
    return lax.dot_general(a, b, (((1,), (1,)), ((), ())), preferred_element_type=F32)


def _head_sumsq(z, bd):
    zz = z * z
    hi = zz.astype(BF16)
    lo = (zz - hi.astype(F32)).astype(BF16)
    return _dot(hi, bd) + _dot(lo, bd)


def _head_norm(z, g, bd):
    ss = _head_sumsq(z, bd)
    return (z * lax.rsqrt(ss * (1.0 / HEAD_DIM) + RMS_EPS)) * g


C_QA, C_KA, C_VA, C_QB, C_KVB, C_END = 0, 512, 1024, 1536, 2048, 2816


def _inproj_kernel(x_ref, gna_ref, w1_ref, b1_ref, gv_ref, wg_ref, bg_ref, wgm_ref, bgm_ref, bd_ref,
                   qa_ref, kva_ref, kab_ref, vab_ref, qb_ref, kvn_ref, cmpk_ref, cmpv_ref, ksb_ref, vsb_ref, kvw_ref,
                   kwb_ref, vwb_ref, gb_ref, gm_ref, *maybe_km, n_blk, token_minor):
    x = x_ref[...]
    tm = x.shape[0]
    r = lax.rsqrt(jnp.mean(x * x, axis=-1, keepdims=True) + RMS_EPS)
    xb = ((x * r) * gna_ref[...]).astype(BF16)
    bd = bd_ref[...]
    bd1 = bd_ref[:LANES, :LANES]

    def sec(a, b):
        return _dot(xb, w1_ref[:, a:b]) + b1_ref[:, a:b]

    def put_heads(ref, comp, first_head, tile):
        if token_minor:
            ref[0, comp, first_head:first_head + 2] = tile.T.reshape(2, HEAD_DIM, tm)
        else:
            for h in range(2):
                ref[:, comp, first_head + h, :] = tile[:, h * HEAD_DIM:(h + 1) * HEAD_DIM]

    qa = _head_norm(sec(C_QA, C_KA), gv_ref[:, C_QA:C_KA], bd)
    qa_ref[...] = (qa * QK_SCALE).astype(BF16)

    ka = _head_norm(sec(C_KA, C_VA), gv_ref[:, C_KA:C_VA], bd)
    va = sec(C_VA, C_QB)
    for j in range(H_A // 2):
        put_heads(kva_ref, 0, 2 * j, ka[:, j * LANES:(j + 1) * LANES])
        put_heads(kva_ref, 1, 2 * j, va[:, j * LANES:(j + 1) * LANES])
    kab_ref[...] = ka.astype(BF16)
    vab_ref[...] = va.astype(BF16)
    if n_blk:
        km_ref = maybe_km[0]
        km_ref[0] = jnp.mean(ka.reshape(n_blk, MOBA_BLOCK, W_A), axis=1)

    qb = _head_norm(sec(C_QB, C_KVB), gv_ref[:, C_QB:C_KVB], bd)
    qb_ref[...] = (qb * QK_SCALE).astype(BF16)

    kvb = sec(C_KVB, C_END)
    ks = _head_norm(kvb[:, 256:384], gv_ref[:, C_KVB + 256:C_KVB + 384], bd1)
    vs = kvb[:, 384:512]
    kw = _head_norm(kvb[:, 512:640], gv_ref[:, C_KVB + 512:C_KVB + 640], bd1)
    vw = kvb[:, 640:768]
    cmpk_ref[...] = kvb[:, 0:128]
    cmpv_ref[...] = kvb[:, 128:256]
    for comp, tile in enumerate((kvb[:, 0:128], kvb[:, 128:256], ks, vs)):
        put_heads(kvn_ref, comp, 0, tile)
    for comp, tile in enumerate((kw, vw)):
        put_heads(kvw_ref, comp, 0, tile)
    ksb_ref[...] = ks.astype(BF16)
    vsb_ref[...] = vs.astype(BF16)
    kwb_ref[...] = kw.astype(BF16)
    vwb_ref[...] = vw.astype(BF16)

    gb_ref[...] = jax.nn.sigmoid(_dot(xb, wg_ref[...]) + bg_ref[...])
    gm_ref[...] = jax.nn.sigmoid(_dot(xb, wgm_ref[...]) + bgm_ref[...])


def _inproj(x, wts, seq_len=None):
    n = x.shape[0]
    token_minor = seq_len is not None
    tm = MOBA_BLOCK if token_minor else n
    n_blk = tm // MOBA_BLOCK if token_minor else 0
    grid = (n // tm,)
    if token_minor:
        assert seq_len % tm == 0
        tiles = seq_len // tm

    def rows(w):
        if isinstance(w, tuple) and token_minor:
            return pl.BlockSpec((1,) + w + (tm,), lambda i: (i // tiles, 0, 0, 0, i % tiles))
        if isinstance(w, tuple):
            return pl.BlockSpec((tm,) + w, lambda i: (i,) + (0,) * len(w))
        return pl.BlockSpec((tm, w), lambda i: (i, 0))

    def kv_shape(w):
        return (n // seq_len,) + w + (seq_len,) if token_minor else (n,) + w

    def full(a):
        return pl.BlockSpec(a.shape, lambda i: (0,) * a.ndim)

    ins = [x, wts["gna"], wts["w1"], wts["b1"], wts["gv"], wts["wg"], wts["bg"], wts["wgm"], wts["bgm"], wts["bd"]]
    in_specs = [rows(D_MODEL)] + [full(a) for a in ins[1:]]
    outs = [
        (W_A, BF16), ((2, H_A, HEAD_DIM), F32), (W_A, BF16), (W_A, BF16), (W_B, BF16), ((4, H_KV, HEAD_DIM), F32),
        (W_KV, F32), (W_KV, F32), (W_KV, BF16), (W_KV, BF16), ((2, H_KV, HEAD_DIM), F32), (W_KV, BF16), (W_KV, BF16),
        (LANES, F32), (2 * D_MODEL, F32),
    ]
    out_shape = [jax.ShapeDtypeStruct(kv_shape(w) if isinstance(w, tuple) else (n, w), dt) for w, dt in outs]
    out_specs = [rows(w) for w, _ in outs]
    if n_blk:
        out_shape.append(jax.ShapeDtypeStruct((n // tm, n_blk, W_A), F32))
        out_specs.append(pl.BlockSpec((1, n_blk, W_A), lambda i: (i, 0, 0)))
    res = pl.pallas_call(
        functools.partial(_inproj_kernel, n_blk=n_blk, token_minor=token_minor),
        grid=grid, in_specs=in_specs, out_specs=out_specs, out_shape=out_shape,
        compiler_params=_cparams("parallel"),
    )(*ins)
    names = ["qa", "kva", "kab", "vab", "qb", "kvn", "cmpk", "cmpv", "ksb", "vsb", "kvw", "kwb", "vwb", "gb", "gm"]
    out = dict(zip(names, res))
    if n_blk:
        out["kmean"] = res[-1].reshape(n // MOBA_BLOCK, W_A)
    return out


def _compress_math(load_main, load_halo, pos_ref, wa_ref, wb_ref, w2_ref, gk_ref, bd_ref, kc_ref, vc_ref, tc):
    bd1 = bd_ref[:LANES, :LANES]

    def half_proj(load, w_ref, half):
        outs = []
        for br in range(2):
            acc = None
            for l in range(CMP_STRIDE):
                xl = (load(br, l) + pos_ref[half, br, l]).astype(BF16)
                t = _dot(xl, w_ref[br, l])
                acc = t if acc is None else acc + t
            outs.append(acc)
        return outs

    a_k, a_v = half_proj(load_main, wa_ref, 0)
    b_k, b_v = half_proj(load_main, wb_ref, 1)
    bh_k, bh_v = half_proj(load_halo, wb_ref, 1)
    rows = lax.broadcasted_iota(jnp.int32, (tc, 2 * CMP_HID), 0)

    def shift_up(b, bh):
        rolled = pltpu.roll(b, tc - 1, 0)
        return jnp.where(rows == tc - 1, jnp.broadcast_to(bh[0:1], b.shape), rolled)

    hid_k = a_k + shift_up(b_k, bh_k)
    hid_v = a_v + shift_up(b_v, bh_v)
    hid_k = hid_k * jax.nn.sigmoid(hid_k)
    hid_v = hid_v * jax.nn.sigmoid(hid_v)
    out_k = _dot(hid_k.astype(BF16), w2_ref[0])
    out_v = _dot(hid_v.astype(BF16), w2_ref[1])
    kc_ref[0] = _head_norm(out_k, gk_ref[...], bd1).astype(BF16)
    vc_ref[0] = out_v.astype(BF16)


def _compress_kernel(xk_ref, xv_ref, hk_ref, hv_ref, *rest, tc):
    chunk_rows = lambda ref, n: (lambda l: ref[0, pl.ds(l, n, stride=CMP_STRIDE), :])
    main = (chunk_rows(xk_ref, tc), chunk_rows(xv_ref, tc))
    halo = (chunk_rows(hk_ref, 8), chunk_rows(hv_ref, 8))
    _compress_math(lambda br, l: main[br](l), lambda br, l: halo[br](l), *rest, tc)


CMP_PAGES = (32, 16)


def _compress_paged_kernel(pt_ref, *refs, g_pages):
    del pt_ref
    page_refs = refs[:g_pages + 1]
    rest, (xk_sc, xv_sc) = refs[g_pages + 1:-2], refs[-2:]
    for g, page in enumerate(page_refs):
        rows = slice(g * PAGE_SIZE, (g + 1) * PAGE_SIZE)
        xk_sc[rows, :] = page[0, 0].reshape(W_KV, PAGE_SIZE).T
        xv_sc[rows, :] = page[0, 1].reshape(W_KV, PAGE_SIZE).T
    tc = g_pages * (PAGE_SIZE // CMP_STRIDE)
    sc = (xk_sc, xv_sc)
    _compress_math(lambda br, l: sc[br][pl.ds(l, tc, stride=CMP_STRIDE), :],
                   lambda br, l: sc[br][pl.ds(tc * CMP_STRIDE + l, 8, stride=CMP_STRIDE), :], *rest, tc)


def _compress_paged(page_table, nsa_t, wts):
    s, n_pages = page_table.shape
    g_pages = next(g for g in CMP_PAGES if n_pages % g == 0)
    tc = g_pages * (PAGE_SIZE // CMP_STRIDE)
    nc = n_pages * (PAGE_SIZE // CMP_STRIDE)
    page = lambda k: pl.BlockSpec(
        (1, 2) + nsa_t.shape[2:], lambda b, i, pt: (pt[b, jnp.minimum(i * g_pages + k, n_pages - 1)], 0, 0, 0, 0))
    full = lambda a: pl.BlockSpec(a.shape, lambda b, i, pt: (0,) * a.ndim)
    consts = [wts["cpos"], wts["cwa"], wts["cwb"], wts["cw2"], wts["gkc"], wts["bd"]]
    rows_sc = pltpu.VMEM(((g_pages + 1) * PAGE_SIZE, LANES), F32)
    return pl.pallas_call(
        functools.partial(_compress_paged_kernel, g_pages=g_pages),
        grid_spec=pltpu.PrefetchScalarGridSpec(
            num_scalar_prefetch=1, grid=(s, n_pages // g_pages),
            in_specs=[page(k) for k in range(g_pages + 1)] + [full(a) for a in consts],
            out_specs=[pl.BlockSpec((1, tc, LANES), lambda b, i, pt: (b, i, 0))] * 2,
            scratch_shapes=[rows_sc, rows_sc]),
        out_shape=[jax.ShapeDtypeStruct((s, nc, LANES), BF16)] * 2,
        compiler_params=_cparams("parallel", "arbitrary"),
    )(page_table, *([nsa_t] * (g_pages + 1)), *consts)


def _compress(rows_k, rows_v, wts, tc):
    b, length, _ = rows_k.shape
    nc = length // CMP_STRIDE
    nt = nc // tc
    last_h = nc // 8 - 1
    full = lambda a: pl.BlockSpec(a.shape, lambda i, j: (0,) * a.ndim)
    main = pl.BlockSpec((1, tc * CMP_STRIDE, LANES), lambda i, j: (i, j, 0))
    halo = pl.BlockSpec((1, 8 * CMP_STRIDE, LANES), lambda i, j: (i, jnp.minimum((j + 1) * (tc // 8), last_h), 0))
    ins = [rows_k, rows_v, rows_k, rows_v, wts["cpos"], wts["cwa"], wts["cwb"], wts["cw2"], wts["gkc"], wts["bd"]]
    in_specs = [main, main, halo, halo] + [full(a) for a in ins[4:]]
    out_shape = [jax.ShapeDtypeStruct((b, nc, LANES), BF16)] * 2
    out_specs = [pl.BlockSpec((1, tc, LANES), lambda i, j: (i, j, 0))] * 2
    return pl.pallas_call(
        functools.partial(_compress_kernel, tc=tc),
        grid=(b, nt), in_specs=in_specs, out_specs=out_specs, out_shape=out_shape,
        compiler_params=_cparams("parallel", "arbitrary"),
    )(*ins)


def _flash_init(m_ref, l_ref, acc_ref):
    m_ref[...] = jnp.full(m_ref.shape, NEG_INF, F32)
    l_ref[...] = jnp.zeros(l_ref.shape, F32)
    acc_ref[...] = jnp.zeros(acc_ref.shape, F32)


def _flash_step(s, valid, v, m_ref, l_ref, acc_ref, i):
    s = jnp.where(valid, s, NEG_INF)
    m_old = m_ref[i]
    m_new = jnp.maximum(m_old, jnp.max(s, axis=1, keepdims=True))
    m_safe = jnp.where(m_new == NEG_INF, 0.0, m_new)
    alpha = jnp.exp(m_old - m_safe)
    p = jnp.exp(s - m_safe[:, :1])
    l_ref[i] = alpha * l_ref[i] + jnp.sum(p, axis=1, keepdims=True)
    acc_ref[i] = alpha * acc_ref[i] + _dot(p.astype(BF16), v)
    m_ref[i] = m_new


def _flash_out(l_ref, acc_ref, i):
    l = l_ref[i]
    return acc_ref[i] / jnp.where(l > 0, l, 1.0)


def _topk_mask(v, idx_iota, k):
    width = v.shape[1]

    def body(_, carry):
        v, sel = carry
        mx = jnp.max(v, axis=1, keepdims=True)
        idx = jnp.min(jnp.where(v == mx, idx_iota, width), axis=1, keepdims=True)
        hit = idx_iota == idx
        sel = jnp.where(hit & (mx > NEG_INF), 1.0, sel)
        return jnp.where(hit, NEG_INF, v), sel

    return lax.fori_loop(0, k, body, (v, jnp.zeros(v.shape, F32)))[1]


def _cmp_to_block_matrix(ncp, nsp, n_cmp, n_s):
    m = np.zeros((ncp, nsp), np.float32)
    per = SLC_BLOCK // CMP_STRIDE
    for i in range(n_s * per):
        for t in (i, i - 1):
            if 0 <= t < n_cmp:
                m[t, i // per] += 1.0
    return jnp.asarray(m)


MASK_BIG = 2.0 ** 30
AUX_R = LANES - 1


def _key_aux_table(t, block):
    assert t // block <= AUX_R and block <= 256
    key = np.arange(t)
    tab = np.zeros((t, LANES), np.float32)
    tab[key, key // block] = 1.0
    tab[:, AUX_R] = key % block
    return jnp.asarray(tab, BF16)


KEY_CHUNK = 64


def _flash_t_step(s_ref, p_ref, mask, v_t, m_ref, l_ref, acc_ref, i):
    n_keys, n_q = s_ref.shape
    chunks = [slice(r, r + KEY_CHUNK) for r in range(0, n_keys, KEY_CHUNK)]

    def scores(rows):
        s = s_ref[rows, :]
        if mask is not None:
            s = jnp.where(mask[rows, :], s, -MASK_BIG)
        return s.reshape(KEY_CHUNK // 8, 8, n_q)

    m_old = m_ref[i]
    m8 = None
    for rows in chunks:
        cm = jnp.max(scores(rows), axis=0)
        m8 = cm if m8 is None else jnp.maximum(m8, cm)
    m_new = jnp.maximum(m_old, jnp.max(m8, axis=0, keepdims=True))
    alpha = jnp.exp(m_old - m_new)
    l8 = None
    for rows in chunks:
        p = jnp.exp(scores(rows) - m_new[None])
        ps = jnp.sum(p, axis=0)
        l8 = ps if l8 is None else l8 + ps
        p_ref[rows, :] = p.reshape(KEY_CHUNK, n_q).astype(BF16)
    l_ref[i] = alpha * l_ref[i] + l8
    acc_ref[i] = alpha[0:1] * acc_ref[i] + _dot(v_t, p_ref[...])
    m_ref[i] = m_new


def _flash_t_out(l_ref, acc_ref, i):
    l = jnp.sum(l_ref[i], axis=0, keepdims=True)
    return acc_ref[i] / jnp.where(l > 0, l, 1.0)


def _topk_mask_t(v, idx, k):
    n = v.shape[0]

    def body(_, carry):
        v, sel = carry
        mx = jnp.max(v, axis=0, keepdims=True)
        first = jnp.min(jnp.where(v == mx, idx, n), axis=0, keepdims=True)
        hit = idx == first
        sel = jnp.where(hit & (mx > NEG_INF), 1.0, sel)
        return jnp.where(hit, NEG_INF, v), sel

    return lax.fori_loop(0, k, body, (v, jnp.zeros(v.shape, F32)))[1]


def _attn_prompt_kernel(qa_ref, qb_ref, gb_ref, km_ref, ka_ref, va_ref, kc_ref, vc_ref, ks_ref, vs_ref,
                        kw_ref, vw_ref, auxa_ref, auxs_ref, mm_ref, oa_ref, ob_ref,
                        m_sc, l_sc, acc_sc, oc_sc, os_sc, qx_sc, s_sc, p_sc, *, n_cmp, k_a, k_s):
    c = pl.program_id(1)
    tq = MOBA_BLOCK
    sub = lax.broadcasted_iota(jnp.int32, (LANES, tq), 0)
    is_lo = sub < HEAD_DIM
    is_r = sub == AUX_R
    key_r = lax.broadcasted_iota(jnp.int32, (tq, tq), 0)
    qry_r = lax.broadcasted_iota(jnp.int32, (tq, tq), 1)
    causal = key_r <= qry_r

    def masked_q(ref, tile, half):
        q = ref[0, tile * LANES:(tile + 1) * LANES, :]
        keep = is_lo if half == 0 else jnp.logical_not(is_lo)
        return jnp.where(keep, q, jnp.zeros_like(q))

    def set_query(i, q_t, aux):
        qx_sc[i, :LANES, :] = q_t
        qx_sc[i, LANES:, :] = aux.astype(BF16)

    def flash_init():
        m_sc[...] = jnp.full(m_sc.shape, NEG_INF, F32)
        l_sc[...] = jnp.zeros(l_sc.shape, F32)
        acc_sc[...] = jnp.zeros(acc_sc.shape, F32)

    def sweep(k_ref, k_lanes, aux_ref, v_ref, v_rows, heads, blocks, mask):
        units = [(n, i) for n in blocks for i in heads]
        n_buf = s_sc.shape[0]

        def stage_scores(pos):
            n, i = units[pos]
            koff = pl.multiple_of(n * tq, tq)
            kcat = jnp.concatenate([k_ref[0, pl.ds(koff, tq), k_lanes(i)], aux_ref[pl.ds(koff, tq), :]], axis=1)
            s_sc[pos % n_buf] = _dot(kcat, qx_sc[i])

        for pos in range(min(n_buf - 1, len(units))):
            stage_scores(pos)
        for pos, (n, i) in enumerate(units):
            if pos + n_buf - 1 < len(units):
                stage_scores(pos + n_buf - 1)
            _flash_t_step(s_sc.at[pos % n_buf], p_sc.at[pos % 2], mask, v_ref[0, n, v_rows(i), :],
                          m_sc, l_sc, acc_sc, i)

    def past_loop(blocks_fn):
        def body(n2, carry):
            blocks_fn([2 * n2, 2 * n2 + 1])
            return carry
        lax.fori_loop(0, c // 2, body, 0)
        pl.when(c % 2 == 1)(lambda: blocks_fn([c - 1]))

    blk = sub
    back = (c - blk).astype(F32)
    for h in range(H_A):
        j, half = divmod(h, 2)
        q_t = masked_q(qa_ref, j, half)
        gate = _dot(km_ref[0, :, j * LANES:(j + 1) * LANES].astype(BF16), q_t)
        sel = _topk_mask_t(jnp.where(blk < c, gate, NEG_INF), blk, k_a)
        bias = jnp.where(sel > 0.5, 0.0, -MASK_BIG) - (SLOPES_A[h] * tq) * back
        set_query(h, q_t, jnp.where(is_r, SLOPES_A[h], jnp.where(blk < c, bias, 0.0)))

    pair_lanes = lambda i: slice((i // 2) * LANES, (i // 2 + 1) * LANES)
    all_lanes = lambda i: slice(0, LANES)
    heads = range(H_A)
    flash_init()
    sweep(ka_ref, pair_lanes, auxa_ref, va_ref, pair_lanes, heads, [c], causal)
    past_loop(lambda blocks: sweep(ka_ref, pair_lanes, auxa_ref, va_ref, pair_lanes, heads, blocks, None))
    for j in range(H_A // 2):
        o = jnp.where(is_lo, _flash_t_out(l_sc, acc_sc, 2 * j), _flash_t_out(l_sc, acc_sc, 2 * j + 1))
        oa_ref[0, j * LANES:(j + 1) * LANES, :] = o.astype(BF16)

    ncp = kc_ref.shape[1]
    nsp = mm_ref.shape[0]
    jc = lax.broadcasted_iota(jnp.int32, (ncp, tq), 0)
    d_c = c * tq + lax.broadcasted_iota(jnp.int32, (ncp, tq), 1) - (jc * CMP_STRIDE + (CMP_LEN - 1))
    c_valid = (d_c >= 0) & (jc < n_cmp)
    d_cf = d_c.astype(F32)
    sblk = lax.broadcasted_iota(jnp.int32, (nsp, tq), 0)
    cur_s = (c * tq + lax.broadcasted_iota(jnp.int32, (nsp, tq), 1)) // SLC_BLOCK
    forced = (sblk == 0) | (sblk >= cur_s - 1)
    causal_s = sblk <= cur_s
    rel_s = (sblk - c * (tq // SLC_BLOCK)).astype(F32)
    for k in range(H_KV):
        imp = jnp.zeros((ncp, tq), F32)
        for g in range(GROUP):
            i = k * GROUP + g
            s = _dot(kc_ref[0], masked_q(qb_ref, g, k)) - SLOPES_B[i] * d_cf
            s = jnp.where(c_valid, s, NEG_INF)
            m = jnp.max(s, axis=0, keepdims=True)
            m = jnp.where(m == NEG_INF, 0.0, m)
            e = jnp.where(c_valid, jnp.exp(s - m), 0.0)
            d = jnp.sum(e, axis=0, keepdims=True)
            p = e / jnp.where(d > 0, d, 1.0)
            imp = imp + p
            oc_sc[i] = _dot(vc_ref[0], p.astype(BF16))
        impb = jnp.dot(mm_ref[...], imp, preferred_element_type=F32, precision=lax.Precision.HIGHEST)
        impb = jnp.where(forced, impb + FORCE_BONUS, impb)
        impb = jnp.where(causal_s, impb, NEG_INF)
        sel = _topk_mask_t(impb, sblk, k_s)
        for g in range(GROUP):
            i = k * GROUP + g
            bias = jnp.where(sel > 0.5, 0.0, -MASK_BIG) + (SLOPES_B[i] * SLC_BLOCK) * rel_s
            set_query(i, masked_q(qb_ref, g, k), jnp.where(is_r, SLOPES_B[i], bias))

    heads = range(H_B)
    flash_init()
    sweep(ks_ref, all_lanes, auxs_ref, vs_ref, all_lanes, heads, [c], causal)
    past_loop(lambda blocks: sweep(ks_ref, all_lanes, auxs_ref, vs_ref, all_lanes, heads, blocks, None))
    for i in range(H_B):
        os_sc[i] = _flash_t_out(l_sc, acc_sc, i)

    for i in range(H_B):
        qx_sc[i, LANES:, :] = jnp.where(is_r, SLOPES_B[i], -(SLOPES_B[i] * tq) * back).astype(BF16)
    flash_init()
    sweep(kw_ref, all_lanes, auxa_ref, vw_ref, all_lanes, heads, [c], causal)
    n_back = WINDOW // tq
    for b in range(1, n_back + 1):
        mask = None if b < n_back else key_r > qry_r
        pl.when(c >= b)(functools.partial(sweep, kw_ref, all_lanes, auxa_ref, vw_ref, all_lanes, heads, [c - b], mask))

    gb = gb_ref[0]
    for j in range(GROUP):
        def comb(i):
            return (gb[3 * i:3 * i + 1] * oc_sc[i] + gb[3 * i + 1:3 * i + 2] * os_sc[i]
                    + gb[3 * i + 2:3 * i + 3] * _flash_t_out(l_sc, acc_sc, i))
        ob_ref[0, j * LANES:(j + 1) * LANES, :] = jnp.where(is_lo, comb(j), comb(GROUP + j)).astype(BF16)


def _attn_prompt(p, kc, vc, b, t):
    tq = MOBA_BLOCK
    nb = t // tq
    n_s = t // SLC_BLOCK
    n_cmp = t // CMP_STRIDE - 1
    nsp = _round_up(n_s, LANES)
    ncp = kc.shape[1]
    assert nb <= AUX_R and n_s <= AUX_R and nsp == LANES and WINDOW % tq == 0
    km = jnp.pad(p["kmean"].reshape(b, nb, W_A), ((0, 0), (0, LANES - nb), (0, 0)))
    mm = _cmp_to_block_matrix(ncp, nsp, n_cmp, n_s).T
    r3 = lambda a: a.reshape(b, t, a.shape[-1])
    tr = lambda a: r3(a).transpose(0, 2, 1)
    qtile = lambda w: pl.BlockSpec((1, w, tq), lambda i, c: (i, 0, c))
    trb = lambda a: a.reshape(b, nb, tq, a.shape[-1]).transpose(0, 1, 3, 2)
    seq = lambda a: pl.BlockSpec((1,) + a.shape[1:], lambda i, c: (i,) + (0,) * (a.ndim - 1))
    const = lambda a: pl.BlockSpec(a.shape, lambda i, c: (0, 0))
    ins = [tr(p["qa"]), tr(p["qb"]), tr(p["gb"]), km, r3(p["kab"]), trb(p["vab"]), kc, vc.transpose(0, 2, 1),
           r3(p["ksb"]), trb(p["vsb"]), r3(p["kwb"]), trb(p["vwb"]),
           _key_aux_table(t, MOBA_BLOCK), _key_aux_table(t, SLC_BLOCK), mm]
    in_specs = [qtile(W_A), qtile(W_B), qtile(LANES)] + [seq(a) for a in ins[3:12]] + [const(a) for a in ins[12:]]
    out_shape = [jax.ShapeDtypeStruct((b, W_A, t), BF16), jax.ShapeDtypeStruct((b, W_B, t), BF16)]
    out_specs = [qtile(W_A), qtile(W_B)]
    scratch = [
        pltpu.VMEM((H_A, 8, tq), F32), pltpu.VMEM((H_A, 8, tq), F32), pltpu.VMEM((H_A, LANES, tq), F32),
        pltpu.VMEM((H_B, LANES, tq), F32), pltpu.VMEM((H_B, LANES, tq), F32), pltpu.VMEM((H_A, 2 * LANES, tq), BF16),
        pltpu.VMEM((6, tq, tq), F32), pltpu.VMEM((2, tq, tq), BF16),
    ]
    oa, ob = pl.pallas_call(
        functools.partial(_attn_prompt_kernel, n_cmp=n_cmp, k_a=min(MOBA_TOPK, nb), k_s=min(SLC_TOPK, n_s)),
        grid=(b, nb), in_specs=in_specs, out_specs=out_specs, out_shape=out_shape, scratch_shapes=scratch,
        compiler_params=_cparams("parallel", "arbitrary"),
    )(*ins)
    return oa.transpose(0, 2, 1).reshape(b * t, W_A), ob.transpose(0, 2, 1).reshape(b * t, W_B)


MOBA_SAMPLE_PAGES = 8


def _moba_sample_tables(t_new):
    rows = H_A * t_new
    head = np.arange(rows) // t_new
    qidx = np.arange(rows) % t_new
    slope = np.asarray(SLOPES_A, np.float32)[head]
    b0 = (slope[:, None] * np.arange(PAGE_SIZE)[None, :]).astype(np.float32)
    coln = np.arange(t_new * H_A)
    ok = (head[:, None] == (coln % H_A)[None, :]) & ((coln // H_A)[None, :] <= qidx[:, None])
    bnew = np.where(ok, slope[:, None] * (coln // H_A)[None, :], -MASK_BIG).astype(np.float32)
    own = (head[:, None] == (np.arange(W_A) // HEAD_DIM)[None, :]).astype(np.float32)
    slope_l = np.broadcast_to(slope[:, None], (rows, LANES)).astype(np.float32)
    return jnp.asarray(b0), jnp.asarray(bnew), jnp.asarray(own), jnp.asarray(slope_l)


def _moba_sample_kernel(pt_ref, qbd_ref, q_ref, b0_ref, bnew_ref, own_ref, slope_ref, new_ref, *rest, n_pages, k_a):
    g_pages = MOBA_SAMPLE_PAGES
    page_refs, o_ref = rest[:g_pages], rest[g_pages]
    g_sc, m_sc, l_sc, o_sc = rest[g_pages + 1:]
    del pt_ref
    step = pl.program_id(1)
    qbd = qbd_ref[0]
    rows = qbd.shape[0]
    slope = slope_ref[:, 0:1]

    def softmax_partial(s):
        m = jnp.max(s, axis=1, keepdims=True)
        e = jnp.exp(s - m)
        return m, jnp.sum(e, axis=1, keepdims=True), e.astype(BF16)

    k_all = jnp.concatenate([pg[0, 0].reshape(W_A, PAGE_SIZE).astype(BF16) for pg in page_refs], axis=1)
    qk_all = _dot(qbd, k_all)
    for g in range(g_pages):
        p = step * g_pages + g
        v_t = page_refs[g][0, 1].reshape(W_A, PAGE_SIZE).astype(BF16)
        qk = qk_all[:, g * PAGE_SIZE:(g + 1) * PAGE_SIZE]
        g_sc[p] = jnp.broadcast_to(jnp.sum(qk, axis=1, keepdims=True), (rows, LANES))
        m, l, e = softmax_partial(qk + b0_ref[...] + slope * ((p - n_pages) * PAGE_SIZE).astype(F32))
        m_sc[p] = jnp.broadcast_to(m, (rows, LANES))
        l_sc[p] = jnp.broadcast_to(l, (rows, LANES))
        o_sc[p] = _dot_t(e, v_t)

    @pl.when(step == pl.num_programs(1) - 1)
    def _():
        per_blk = MOBA_BLOCK // PAGE_SIZE
        nb = n_pages // per_blk
        gate = jnp.sum(g_sc[...].reshape(nb, per_blk, rows, LANES), axis=1)
        sel = _topk_mask_t(gate, lax.broadcasted_iota(jnp.int32, gate.shape, 0), k_a)
        sel = jnp.broadcast_to(sel[:, None], (nb, per_blk, rows, LANES)).reshape(n_pages, rows, LANES) > 0.5
        t_new = new_ref.shape[1]
        k_new = new_ref[0, :, 0].reshape(t_new * H_A, HEAD_DIM).astype(BF16)
        v_new = new_ref[0, :, 1].reshape(t_new * H_A, HEAD_DIM).astype(BF16)
        m_o, l_o, e_o = softmax_partial(_dot_t(q_ref[0], k_new) + bnew_ref[...])
        o_o = _dot(e_o, v_new)
        m_all = m_sc[...]
        m_star = jnp.maximum(jnp.max(jnp.where(sel, m_all, -MASK_BIG), axis=0), m_o)
        w = jnp.where(sel, jnp.exp(m_all - m_star[None]), 0.0)
        w_o = jnp.exp(m_o - m_star)
        l_star = jnp.sum(w * l_sc[...], axis=0) + w_o * l_o
        n_rep = W_A // LANES
        w_wide = jnp.concatenate([w] * n_rep, axis=2)
        o_star = jnp.sum(w_wide * o_sc[...], axis=0) + (
            jnp.concatenate([w_o[:, :HEAD_DIM]] * H_A, axis=1) * jnp.concatenate([o_o] * H_A, axis=1))
        o_ref[0] = (o_star / jnp.concatenate([l_star] * n_rep, axis=1)) * own_ref[...]


def _moba_sample(p, cache_moba, page_table, s, t_new):
    n_pages = page_table.shape[1]
    g_pages = MOBA_SAMPLE_PAGES
    assert n_pages % g_pages == 0 and n_pages % (MOBA_BLOCK // PAGE_SIZE) == 0
    rows = H_A * t_new
    nb = n_pages // (MOBA_BLOCK // PAGE_SIZE)
    cache_t = cache_moba.transpose(0, 2, 3, 4, 1)
    q4 = p["qa"].reshape(s, t_new, H_A, HEAD_DIM).transpose(0, 2, 1, 3)
    q = q4.reshape(s, rows, HEAD_DIM)
    eye = jnp.eye(H_A, dtype=q4.dtype)
    qbd = (q4[:, :, :, None, :] * eye[None, :, None, :, None]).reshape(s, rows, W_A)
    new = p["kva"].reshape(s, t_new, 2, H_A, HEAD_DIM)
    tables = _moba_sample_tables(t_new)
    const = lambda a: pl.BlockSpec(a.shape, lambda b, i, pt: (0, 0))
    seq = lambda a: pl.BlockSpec((1,) + a.shape[1:], lambda b, i, pt: (b,) + (0,) * (a.ndim - 1))
    page = lambda g: pl.BlockSpec((1,) + cache_t.shape[1:], lambda b, i, pt: (pt[b, i * g_pages + g], 0, 0, 0, 0))
    out = pl.pallas_call(
        functools.partial(_moba_sample_kernel, n_pages=n_pages, k_a=min(MOBA_TOPK, nb)),
        grid_spec=pltpu.PrefetchScalarGridSpec(
            num_scalar_prefetch=1, grid=(s, n_pages // g_pages),
            in_specs=[seq(qbd), seq(q)] + [const(a) for a in tables] + [seq(new)] + [page(g) for g in range(g_pages)],
            out_specs=pl.BlockSpec((1, rows, W_A), lambda b, i, pt: (b, 0, 0)),
            scratch_shapes=[pltpu.VMEM((n_pages, rows, LANES), F32)] * 3 + [pltpu.VMEM((n_pages, rows, W_A), F32)]),
        out_shape=jax.ShapeDtypeStruct((s, rows, W_A), F32),
        compiler_params=_cparams("parallel", "arbitrary"),
    )(page_table, qbd, q, *tables, new, *([cache_t] * g_pages))
    return out.reshape(s, H_A, t_new, W_A).sum(axis=1).reshape(s * t_new, W_A).astype(BF16)


SAMPLE_KV_TILE = 2048
SAMPLE_KV_PAGES = SAMPLE_KV_TILE // PAGE_SIZE


def _attn_sample_kernel(pt_ref, qb16_ref, g16_ref, kc_ref, vc_ref, mm_ref, ksn_ref, vsn_ref, kwa_ref, vwa_ref, *rest,
                        past, t_new, k_s):
    del pt_ref
    page_refs = rest[:SAMPLE_KV_PAGES]
    ob_ref, mb_sc, lb_sc, accb_sc, sels_sc, oc_sc, os_sc = rest[SAMPLE_KV_PAGES:]
    kt = pl.program_id(1)
    r_tile = SAMPLE_KV_TILE
    rb = GROUP * t_new
    row_b = lax.broadcasted_iota(jnp.int32, (rb, 1), 0)
    t_b = past + row_b % t_new
    slope_b = []
    for k in range(H_KV):
        sl = jnp.full((rb, 1), SLOPES_B[k * GROUP + GROUP - 1], F32)
        for g in range(GROUP - 2, -1, -1):
            sl = jnp.where(row_b < (g + 1) * t_new, SLOPES_B[k * GROUP + g], sl)
        slope_b.append(sl)
    nsp = mm_ref.shape[1]
    cur_s = past // SLC_BLOCK

    @pl.when(kt == 0)
    def _():
        _flash_init(mb_sc, lb_sc, accb_sc)
        nc = kc_ref.shape[1]
        jc = lax.broadcasted_iota(jnp.int32, (rb, nc), 1)
        d_c = t_b - (jc * CMP_STRIDE + (CMP_LEN - 1))
        c_valid = d_c >= 0
        d_cf = d_c.astype(F32)
        sj = lax.broadcasted_iota(jnp.int32, (rb, nsp), 1)
        forced = (sj == 0) | (sj >= cur_s - 1)
        causal_s = sj <= cur_s
        gr = lax.broadcasted_iota(jnp.int32, (rb, rb), 0) % t_new
        gc = lax.broadcasted_iota(jnp.int32, (rb, rb), 1) % t_new
        group_sum = (gr == gc).astype(F32)
        for k in range(H_KV):
            s = _dot_t(qb16_ref[0, k], kc_ref[0]) - slope_b[k] * d_cf
            s = jnp.where(c_valid, s, NEG_INF)
            m = jnp.max(s, axis=1, keepdims=True)
            m = jnp.where(m == NEG_INF, 0.0, m)
            e = jnp.where(c_valid, jnp.exp(s - m), 0.0)
            d = jnp.sum(e, axis=1, keepdims=True)
            p = e / jnp.where(d > 0, d, 1.0)
            oc_sc[k] = _dot(p.astype(BF16), vc_ref[0])
            pb = jnp.dot(p, mm_ref[...], preferred_element_type=F32, precision=lax.Precision.HIGHEST)
            impb = jnp.dot(group_sum, pb, preferred_element_type=F32, precision=lax.Precision.HIGHEST)
            impb = jnp.where(forced, impb + FORCE_BONUS, impb)
            impb = jnp.where(causal_s, impb, NEG_INF)
            sels_sc[k] = _topk_mask(impb, sj, k_s)

    blocks_per_tile = r_tile // SLC_BLOCK
    first_blk = kt * blocks_per_tile
    lane_tile = pl.multiple_of((first_blk // LANES) * LANES, LANES)
    e_row = lax.broadcasted_iota(jnp.int32, (LANES, r_tile), 0)
    e_col = lax.broadcasted_iota(jnp.int32, (LANES, r_tile), 1) // SLC_BLOCK
    expand = (e_row == first_blk % LANES + e_col).astype(BF16)
    col_t = lax.broadcasted_iota(jnp.int32, (rb, r_tile), 1)
    dist_t = (t_b - (kt * r_tile + col_t)).astype(F32)
    k_all = jnp.concatenate([pg[0, 0].reshape(W_KV, PAGE_SIZE).astype(BF16) for pg in page_refs], axis=1)
    v_all = jnp.concatenate([pg[0, 1].reshape(W_KV, PAGE_SIZE).astype(BF16) for pg in page_refs], axis=1)
    for k in range(H_KV):
        sel_t = sels_sc[k, :, pl.ds(lane_tile, LANES)]
        valid = _dot(sel_t.astype(BF16), expand) > 0.5
        s = _dot(qb16_ref[0, k], k_all) - slope_b[k] * dist_t
        s = jnp.where(valid, s, NEG_INF)
        m_old = mb_sc[k]
        m_new = jnp.maximum(m_old, jnp.max(s, axis=1, keepdims=True))
        m_safe = jnp.where(m_new == NEG_INF, 0.0, m_new)
        alpha = jnp.exp(m_old - m_safe)
        p = jnp.exp(s - m_safe[:, :1])
        lb_sc[k] = alpha * lb_sc[k] + jnp.sum(p, axis=1, keepdims=True)
        accb_sc[k] = alpha * accb_sc[k] + _dot_t(p.astype(BF16), v_all)
        mb_sc[k] = m_new

    @pl.when(kt == pl.num_programs(1) - 1)
    def _():
        dist_b = row_b % t_new - lax.broadcasted_iota(jnp.int32, (rb, LANES), 1)
        sj = lax.broadcasted_iota(jnp.int32, (rb, nsp), 1)
        for k in range(H_KV):
            own = jnp.sum(jnp.where(sj == cur_s, sels_sc[k], 0.0), axis=1, keepdims=True) > 0.5
            s = _dot_t(qb16_ref[0, k], ksn_ref[0]) - slope_b[k] * dist_b.astype(F32)
            _flash_step(s, own & (dist_b >= 0), vsn_ref[0], mb_sc, lb_sc, accb_sc, k)
            os_sc[k] = _flash_out(lb_sc, accb_sc, k)
        _flash_init(mb_sc, lb_sc, accb_sc)
        n_w = kwa_ref.shape[1]
        dist_w = WINDOW + row_b % t_new - lax.broadcasted_iota(jnp.int32, (rb, n_w), 1)
        valid_w = (dist_w >= 0) & (dist_w < WINDOW)
        lane_b = lax.broadcasted_iota(jnp.int32, (rb, LANES), 1)
        comb = []
        for k in range(H_KV):
            s = _dot_t(qb16_ref[0, k], kwa_ref[0]) - slope_b[k] * dist_w.astype(F32)
            _flash_step(s, valid_w, vwa_ref[0], mb_sc, lb_sc, accb_sc, k)
            g = g16_ref[0, k]
            comb.append(g[:, 0:1] * oc_sc[k] + g[:, 1:2] * os_sc[k] + g[:, 2:3] * _flash_out(lb_sc, accb_sc, k))
        ob_ref[0] = jnp.where(lane_b < HEAD_DIM, comb[0], comb[1])


def _attn_sample(p, nsa_t, page_table, kc, vc, state_win, s, t_new, past):
    r_tile = SAMPLE_KV_TILE
    g_pages = SAMPLE_KV_PAGES
    assert past % r_tile == 0 and state_win.shape[1] == WINDOW and LANES % (r_tile // SLC_BLOCK) == 0
    lp = _round_up(past + t_new, MOBA_BLOCK)
    n_s = lp // SLC_BLOCK
    n_cmp = lp // CMP_STRIDE - 1
    nsp = _round_up(n_s, LANES)
    nc = kc.shape[1]
    mm = _cmp_to_block_matrix(nc, nsp, min(n_cmp, nc), n_s)
    lane = jnp.arange(LANES)
    lo = (lane < HEAD_DIM)

    qb = p["qb"].reshape(s, t_new, GROUP, LANES).transpose(0, 2, 1, 3).reshape(s, 1, GROUP * t_new, LANES)
    zero = jnp.zeros_like(qb)
    qb16 = jnp.concatenate([jnp.where(lo, qb, zero), jnp.where(lo, zero, qb)], axis=1)
    g16 = p["gb"][:, :3 * H_B].reshape(s, t_new, H_KV, GROUP, 3).transpose(0, 2, 3, 1, 4)
    g16 = jnp.pad(g16.reshape(s, H_KV, GROUP * t_new, 3), ((0, 0), (0, 0), (0, 0), (0, LANES - 3)))

    pad_new = lambda a: jnp.pad(a.reshape(s, t_new, a.shape[-1]), ((0, 0), (0, LANES - t_new), (0, 0)))
    win = state_win.reshape(s, WINDOW, 2, W_KV).astype(BF16)
    kwa = jnp.concatenate([win[:, :, 0], pad_new(p["kwb"])], axis=1)
    vwa = jnp.concatenate([win[:, :, 1], pad_new(p["vwb"])], axis=1)

    ins = [qb16, g16, kc, vc, mm, pad_new(p["ksb"]), pad_new(p["vsb"]), kwa, vwa]
    seq = lambda a: pl.BlockSpec((1,) + a.shape[1:], lambda i, k, pt: (i,) + (0,) * (a.ndim - 1))
    page = lambda g: pl.BlockSpec((1, 2) + nsa_t.shape[2:], lambda i, k, pt: (pt[i, k * g_pages + g], 1, 0, 0, 0))
    in_specs = ([seq(a) for a in ins[:4]] + [pl.BlockSpec(mm.shape, lambda i, k, pt: (0, 0))]
                + [seq(a) for a in ins[5:]] + [page(g) for g in range(g_pages)])
    rb = GROUP * t_new
    scratch = [
        pltpu.VMEM((H_KV, rb, LANES), F32), pltpu.VMEM((H_KV, rb, LANES), F32), pltpu.VMEM((H_KV, rb, LANES), F32),
        pltpu.VMEM((H_KV, rb, nsp), F32), pltpu.VMEM((H_KV, rb, LANES), F32), pltpu.VMEM((H_KV, rb, LANES), F32),
    ]
    ob16 = pl.pallas_call(
        functools.partial(_attn_sample_kernel, past=past, t_new=t_new, k_s=min(SLC_TOPK, n_s)),
        grid_spec=pltpu.PrefetchScalarGridSpec(
            num_scalar_prefetch=1, grid=(s, past // r_tile), in_specs=in_specs,
            out_specs=pl.BlockSpec((1, rb, LANES), lambda i, k, pt: (i, 0, 0)), scratch_shapes=scratch),
        out_shape=jax.ShapeDtypeStruct((s, rb, LANES), F32),
        compiler_params=_cparams("parallel", "arbitrary"),
    )(page_table, *ins, *([nsa_t] * g_pages))
    return ob16.reshape(s, GROUP, t_new, LANES).transpose(0, 2, 1, 3).reshape(s * t_new, W_B).astype(BF16)


def _post_kernel(oa_ref, ob_ref, gm_ref, x_ref, wba_ref, wbb_ref, wout_ref, gnf_ref, wg_ref, wu_ref, wd_ref,
                 y_ref, h_sc, hn_sc, acc_sc):
    j = pl.program_id(1)

    @pl.when(j == 0)
    def _():
        ma = _dot(oa_ref[...], wba_ref[...])
        mb = _dot(ob_ref[...], wbb_ref[...])
        merged = gm_ref[:, :D_MODEL] * ma + gm_ref[:, D_MODEL:] * mb
        h = x_ref[...] + _dot(merged.astype(BF16), wout_ref[...])
        h_sc[...] = h
        r = lax.rsqrt(jnp.mean(h * h, axis=-1, keepdims=True) + RMS_EPS)
        hn_sc[...] = ((h * r) * gnf_ref[...]).astype(BF16)
        acc_sc[...] = jnp.zeros(acc_sc.shape, F32)

    hn = hn_sc[...]
    g = _dot(hn, wg_ref[...])
    u = _dot(hn, wu_ref[...])
    act = (g * jax.nn.sigmoid(g)) * u
    acc_sc[...] += _dot(act.astype(BF16), wd_ref[...])

    @pl.when(j == pl.num_programs(1) - 1)
    def _():
        y_ref[...] = h_sc[...] + acc_sc[...]


def _post(oa, ob, gm, x, wts):
    n = x.shape[0]
    tm = 512 if n % 512 == 0 else n
    d_ff = wts["wd"].shape[0]
    n_ff = 2 if (d_ff // 2) % LANES == 0 else 1
    fc = d_ff // n_ff
    rows = lambda w: pl.BlockSpec((tm, w), lambda i, j: (i, 0))
    full = lambda a: pl.BlockSpec(a.shape, lambda i, j: (0,) * a.ndim)
    ins = [oa, ob, gm, x, wts["wba"], wts["wbb"], wts["wout"], wts["gnf"], wts["wgate"], wts["wup"], wts["wd"]]
    in_specs = [rows(W_A), rows(W_B), rows(2 * D_MODEL), rows(D_MODEL)] + [full(a) for a in ins[4:8]] + [
        pl.BlockSpec((D_MODEL, fc), lambda i, j: (0, j)),
        pl.BlockSpec((D_MODEL, fc), lambda i, j: (0, j)),
        pl.BlockSpec((fc, D_MODEL), lambda i, j: (j, 0)),
    ]
    return pl.pallas_call(
        _post_kernel, grid=(n // tm, n_ff), in_specs=in_specs, out_specs=rows(D_MODEL),
        out_shape=jax.ShapeDtypeStruct((n, D_MODEL), F32),
        scratch_shapes=[pltpu.VMEM((tm, D_MODEL), F32), pltpu.VMEM((tm, D_MODEL), BF16), pltpu.VMEM((tm, D_MODEL), F32)],
        compiler_params=_cparams("parallel", "arbitrary"),
    )(*ins)


def _blockdiag2(w):
    z = jnp.zeros_like(w)
    return jnp.concatenate([jnp.concatenate([w, z], axis=-1), jnp.concatenate([z, w], axis=-1)], axis=-2)


def _prep_weights(g_na, w_in, b_in, g_qk_a, g_qk_b, cmp_pos, cmp_w1, cmp_w2, w_ba, w_bb, w_out, g_nf, w_up, w_down):
    def perm_qb_cols(a):
        lead = a.shape[:-1]
        return a.reshape(lead + (H_KV, GROUP, HEAD_DIM)).swapaxes(-3, -2).reshape(lead + (W_B,))

    def main_cols(a):
        return jnp.concatenate([a[..., :C_QB], perm_qb_cols(a[..., C_QB:C_KVB]), a[..., C_KVB:C_END]], axis=-1)

    ones = lambda n: jnp.ones((n,), F32)
    gv = jnp.concatenate([
        jnp.tile(g_qk_a[0], H_A), jnp.tile(g_qk_a[1], H_A), ones(W_A), jnp.tile(g_qk_b[0], H_B),
        ones(2 * W_KV), jnp.tile(g_qk_b[2], H_KV), ones(W_KV), jnp.tile(g_qk_b[3], H_KV), ones(W_KV)])
    n_gb = 3 * H_B
    hd_idx = np.arange(W_A) // HEAD_DIM
    bd = jnp.asarray((hd_idx[:, None] == hd_idx[None, :]).astype(np.float32)).astype(BF16)
    d_ff = w_down.shape[0]
    w1 = cmp_w1.astype(BF16)
    pos2 = jnp.concatenate([cmp_pos, cmp_pos], axis=-1)
    return {
        "gna": g_na.reshape(1, D_MODEL),
        "w1": main_cols(w_in).astype(BF16), "b1": main_cols(b_in).reshape(1, C_END), "gv": gv.reshape(1, C_END),
        "wg": jnp.pad(w_in[:, C_END:C_END + n_gb], ((0, 0), (0, LANES - n_gb))).astype(BF16),
        "bg": jnp.pad(b_in[C_END:C_END + n_gb], (0, LANES - n_gb)).reshape(1, LANES),
        "wgm": w_in[:, C_END + n_gb:].astype(BF16), "bgm": b_in[C_END + n_gb:].reshape(1, 2 * D_MODEL),
        "bd": bd,
        "cpos": pos2.reshape(2, 2, CMP_STRIDE, 1, LANES).swapaxes(0, 1),
        "cwa": _blockdiag2(w1[:, :CMP_STRIDE]), "cwb": _blockdiag2(w1[:, CMP_STRIDE:]),
        "cw2": _blockdiag2(cmp_w2.astype(BF16)),
        "gkc": jnp.tile(g_qk_b[1], H_KV).reshape(1, LANES),
        "wba": w_ba.astype(BF16),
        "wbb": w_bb.reshape(H_KV, GROUP, HEAD_DIM, D_MODEL).swapaxes(0, 1).reshape(W_B, D_MODEL).astype(BF16),
        "wout": w_out.astype(BF16), "gnf": g_nf.reshape(1, D_MODEL),
        "wgate": w_up[:, :d_ff].astype(BF16), "wup": w_up[:, d_ff:].astype(BF16), "wd": w_down.astype(BF16),
    }


def _prompt_layer(x, wts):
    b, t, _ = x.shape
    x2 = x.reshape(b * t, D_MODEL)
    p = _inproj(x2, wts, seq_len=t)
    nc = t // CMP_STRIDE
    kc, vc = _compress(p["cmpk"].reshape(b, t, W_KV), p["cmpv"].reshape(b, t, W_KV), wts, tc=nc)
    oa, ob = _attn_prompt(p, kc, vc, b, t)
    y = _post(oa, ob, p["gm"], x2, wts)
    keep = min(WINDOW, t)
    rows_major = lambda a: a.transpose(0, 4, 1, 2, 3)
    return (y.reshape(b, t, D_MODEL), rows_major(p["kva"]), rows_major(p["kvn"]),
            rows_major(p["kvw"][..., t - keep:]))


def _sample_layer(x, cache_moba, cache_nsa, state_win, page_table, wts):
    s, t_new, _ = x.shape
    past = page_table.shape[1] * PAGE_SIZE
    x2 = x.reshape(s * t_new, D_MODEL)
    p = _inproj(x2, wts)
    nsa_t = cache_nsa.transpose(0, 2, 3, 4, 1)
    kc, vc = _compress_paged(page_table, nsa_t, wts)
    oa = _moba_sample(p, cache_moba, page_table, s, t_new)
    ob = _attn_sample(p, nsa_t, page_table, kc, vc, state_win, s, t_new, past)
    y = _post(oa, ob, p["gm"], x2, wts)
    kvw_new = p["kvw"].reshape(s, t_new, 2, H_KV, HEAD_DIM)
    win = jnp.concatenate([state_win, kvw_new], axis=1)[:, t_new:]
    return (y.reshape(s, t_new, D_MODEL),
            p["kva"].reshape(s, t_new, 2, H_A, HEAD_DIM),
            p["kvn"].reshape(s, t_new, 4, H_KV, HEAD_DIM),
            win)


def kernel(x_prompt, x_sample, cache_moba_kv, cache_nsa_kv, state_win_kv, page_table, g_norm_attn, w_in, b_in,
           g_qk_moba, g_qk_nsa, cmp_pos, cmp_w1, cmp_w2, w_br_moba, w_br_nsa, w_out, g_norm_ffn, w_up, w_down):
    assert g_norm_attn.shape[0] == 1, "single-layer step"
    wts = _prep_weights(g_norm_attn[0], w_in[0], b_in[0], g_qk_moba[0], g_qk_nsa[0], cmp_pos[0], cmp_w1[0],
                        cmp_w2[0], w_br_moba[0], w_br_nsa[0], w_out[0], g_norm_ffn[0], w_up[0], w_down[0])
    y_p, a_p, n_p, wn_p = _prompt_layer(x_prompt, wts)
    y_s, a_s, n_s, wn_s = _sample_layer(x_sample, cache_moba_kv[0], cache_nsa_kv[0], state_win_kv[0], page_table, wts)
    return (y_p, y_s, a_p[None], n_p[None], wn_p[None], a_s[None], n_s[None], wn_s[None])
```

```python
import functools
import math

import numpy as np
import jax
import jax.numpy as jnp
from jax import lax
from jax.experimental import pallas as pl
from jax.experimental.pallas import tpu as pltpu

F32 = jnp.float32
BF16 = jnp.bfloat16

D_MODEL = 1024
PAGE_SIZE = 128
HEAD_DIM = 64
H_A = 8
H_B = 8
H_KV = 2
GROUP = H_B // H_KV
MOBA_BLOCK = 256
MOBA_TOPK = 3
CMP_LEN = 32
CMP_STRIDE = 16
SLC_BLOCK = 64
SLC_TOPK = 16
WINDOW = 512
CMP_HID = 2 * HEAD_DIM
FORCE_BONUS = 1e4
RMS_EPS = 1e-6
W_A = H_A * HEAD_DIM
W_B = H_B * HEAD_DIM
W_KV = H_KV * HEAD_DIM
QK_SCALE = HEAD_DIM ** -0.5

LANES = 128
VMEM_LIMIT = 56 * 1024 * 1024
NEG_INF = float("-inf")

SLOPES_A = tuple(2.0 ** (-8.0 * (i + 1) / H_A) for i in range(H_A))
SLOPES_B = tuple(2.0 ** (-8.0 * (i + 1) / H_B) for i in range(H_B))


def _round_up(x, m):
    return -(-x // m) * m


def _cparams(*sem):
    return pltpu.CompilerParams(dimension_semantics=sem, vmem_limit_bytes=VMEM_LIMIT)


def _dot(a, b):
    return jnp.dot(a, b, preferred_element_type=F32)


def _dot_t(a, b):
    return lax.dot_general(a, b, (((1,), (1,)), ((), ())), preferred_element_type=F32)


def _head_sumsq(z, bd):
    zz = z * z
    hi = zz.astype(BF16)
    lo = (zz - hi.astype(F32)).astype(BF16)
    return _dot(hi, bd) + _dot(lo, bd)


def _head_norm(z, g, bd):
    ss = _head_sumsq(z, bd)
    return (z * lax.rsqrt(ss * (1.0 / HEAD_DIM) + RMS_EPS)) * g


C_QA, C_KA, C_VA, C_QB, C_KVB, C_END = 0, 512, 1024, 1536, 2048, 2816


def _inproj_kernel(x_ref, gna_ref, w1_ref, b1_ref, gv_ref, wg_ref, bg_ref, wgm_ref, bgm_ref, bd_ref,
                   qa_ref, kva_ref, kab_ref, vab_ref, qb_ref, kvn_ref, cmpk_ref, cmpv_ref, ksb_ref, vsb_ref, kvw_ref,
                   kwb_ref, vwb_ref, gb_ref, gm_ref, *maybe_km, n_blk, token_minor):
    x = x_ref[...]
    tm = x.shape[0]
    r = lax.rsqrt(jnp.mean(x * x, axis=-1, keepdims=True) + RMS_EPS)
    xb = ((x * r) * gna_ref[...]).astype(BF16)
    bd = bd_ref[...]
    bd1 = bd_ref[:LANES, :LANES]

    def sec(a, b):
        return _dot(xb, w1_ref[:, a:b]) + b1_ref[:, a:b]

    def put_heads(ref, comp, first_head, tile):
        if token_minor:
            ref[0, comp, first_head:first_head + 2] = tile.T.reshape(2, HEAD_DIM, tm)
        else:
            for h in range(2):
                ref[:, comp, first_head + h, :] = tile[:, h * HEAD_DIM:(h + 1) * HEAD_DIM]

    qa = _head_norm(sec(C_QA, C_KA), gv_ref[:, C_QA:C_KA], bd)
    qa_ref[...] = (qa * QK_SCALE).astype(BF16)

    ka = _head_norm(sec(C_KA, C_VA), gv_ref[:, C_KA:C_VA], bd)
    va = sec(C_VA, C_QB)
    for j in range(H_A // 2):
        put_heads(kva_ref, 0, 2 * j, ka[:, j * LANES:(j + 1) * LANES])
        put_heads(kva_ref, 1, 2 * j, va[:, j * LANES:(j + 1) * LANES])
    kab_ref[...] = ka.astype(BF16)
    vab_ref[...] = va.astype(BF16)
    if n_blk:
        km_ref = maybe_km[0]
        km_ref[0] = jnp.mean(ka.reshape(n_blk, MOBA_BLOCK, W_A), axis=1)

    qb = _head_norm(sec(C_QB, C_KVB), gv_ref[:, C_QB:C_KVB], bd)
    qb_ref[...] = (qb * QK_SCALE).astype(BF16)

    kvb = sec(C_KVB, C_END)
    ks = _head_norm(kvb[:, 256:384], gv_ref[:, C_KVB + 256:C_KVB + 384], bd1)
    vs = kvb[:, 384:512]
    kw = _head_norm(kvb[:, 512:640], gv_ref[:, C_KVB + 512:C_KVB + 640], bd1)
    vw = kvb[:, 640:768]
    cmpk_ref[...] = kvb[:, 0:128]
    cmpv_ref[...] = kvb[:, 128:256]
    for comp, tile in enumerate((kvb[:, 0:128], kvb[:, 128:256], ks, vs)):
        put_heads(kvn_ref, comp, 0, tile)
    for comp, tile in enumerate((kw, vw)):
        put_heads(kvw_ref, comp, 0, tile)
    ksb_ref[...] = ks.astype(BF16)
    vsb_ref[...] = vs.astype(BF16)
    kwb_ref[...] = kw.astype(BF16)
    vwb_ref[...] = vw.astype(BF16)

    gb_ref[...] = jax.nn.sigmoid(_dot(xb, wg_ref[...]) + bg_ref[...])
    gm_ref[...] = jax.nn.sigmoid(_dot(xb, wgm_ref[...]) + bgm_ref[...])


def _inproj(x, wts, seq_len=None):
    n = x.shape[0]
    token_minor = seq_len is not None
    tm = MOBA_BLOCK if token_minor else n
    n_blk = tm // MOBA_BLOCK if token_minor else 0
    grid = (n // tm,)
    if token_minor:
        assert seq_len % tm == 0
        tiles = seq_len // tm

    def rows(w):
        if isinstance(w, tuple) and token_minor:
            return pl.BlockSpec((1,) + w + (tm,), lambda i: (i // tiles, 0, 0, 0, i % tiles))
        if isinstance(w, tuple):
            return pl.BlockSpec((tm,) + w, lambda i: (i,) + (0,) * len(w))
        return pl.BlockSpec((tm, w), lambda i: (i, 0))

    def kv_shape(w):
        return (n // seq_len,) + w + (seq_len,) if token_minor else (n,) + w

    def full(a):
        return pl.BlockSpec(a.shape, lambda i: (0,) * a.ndim)

    ins = [x, wts["gna"], wts["w1"], wts["b1"], wts["gv"], wts["wg"], wts["bg"], wts["wgm"], wts["bgm"], wts["bd"]]
    in_specs = [rows(D_MODEL)] + [full(a) for a in ins[1:]]
    outs = [
        (W_A, BF16), ((2, H_A, HEAD_DIM), F32), (W_A, BF16), (W_A, BF16), (W_B, BF16), ((4, H_KV, HEAD_DIM), F32),
        (W_KV, F32), (W_KV, F32), (W_KV, BF16), (W_KV, BF16), ((2, H_KV, HEAD_DIM), F32), (W_KV, BF16), (W_KV, BF16),
        (LANES, F32), (2 * D_MODEL, F32),
    ]
    out_shape = [jax.ShapeDtypeStruct(kv_shape(w) if isinstance(w, tuple) else (n, w), dt) for w, dt in outs]
    out_specs = [rows(w) for w, _ in outs]
    if n_blk:
        out_shape.append(jax.ShapeDtypeStruct((n // tm, n_blk, W_A), F32))
        out_specs.append(pl.BlockSpec((1, n_blk, W_A), lambda i: (i, 0, 0)))
    res = pl.pallas_call(
        functools.partial(_inproj_kernel, n_blk=n_blk, token_minor=token_minor),
        grid=grid, in_specs=in_specs, out_specs=out_specs, out_shape=out_shape,
        compiler_params=_cparams("parallel"),
    )(*ins)
    names = ["qa", "kva", "kab", "vab", "qb", "kvn", "cmpk", "cmpv", "ksb", "vsb", "kvw", "kwb", "vwb", "gb", "gm"]
    out = dict(zip(names, res))
    if n_blk:
        out["kmean"] = res[-1].reshape(n // MOBA_BLOCK, W_A)
    return out


def _compress_math(load_main, load_halo, pos_ref, wa_ref, wb_ref, w2_ref, gk_ref, bd_ref, kc_ref, vc_ref, tc):
    bd1 = bd_ref[:LANES, :LANES]

    def half_proj(load, w_ref, half):
        outs = []
        for br in range(2):
            acc = None
            for l in range(CMP_STRIDE):
                xl = (load(br, l) + pos_ref[half, br, l]).astype(BF16)
                t = _dot(xl, w_ref[br, l])
                acc = t if acc is None else acc + t
            outs.append(acc)
        return outs

    a_k, a_v = half_proj(load_main, wa_ref, 0)
    b_k, b_v = half_proj(load_main, wb_ref, 1)
    bh_k, bh_v = half_proj(load_halo, wb_ref, 1)
    rows = lax.broadcasted_iota(jnp.int32, (tc, 2 * CMP_HID), 0)

    def shift_up(b, bh):
        rolled = pltpu.roll(b, tc - 1, 0)
        return jnp.where(rows == tc - 1, jnp.broadcast_to(bh[0:1], b.shape), rolled)

    hid_k = a_k + shift_up(b_k, bh_k)
    hid_v = a_v + shift_up(b_v, bh_v)
    hid_k = hid_k * jax.nn.sigmoid(hid_k)
    hid_v = hid_v * jax.nn.sigmoid(hid_v)
    out_k = _dot(hid_k.astype(BF16), w2_ref[0])
    out_v = _dot(hid_v.astype(BF16), w2_ref[1])
    kc_ref[0] = _head_norm(out_k, gk_ref[...], bd1).astype(BF16)
    vc_ref[0] = out_v.astype(BF16)


def _compress_kernel(xk_ref, xv_ref, hk_ref, hv_ref, *rest, tc):
    chunk_rows = lambda ref, n: (lambda l: ref[0, pl.ds(l, n, stride=CMP_STRIDE), :])
    main = (chunk_rows(xk_ref, tc), chunk_rows(xv_ref, tc))
    halo = (chunk_rows(hk_ref, 8), chunk_rows(hv_ref, 8))
    _compress_math(lambda br, l: main[br](l), lambda br, l: halo[br](l), *rest, tc)


CMP_PAGES = (32, 16)


def _compress_paged_kernel(pt_ref, *refs, g_pages):
    del pt_ref
    page_refs = refs[:g_pages + 1]
    rest, (xk_sc, xv_sc) = refs[g_pages + 1:-2], refs[-2:]
    for g, page in enumerate(page_refs):
        rows = slice(g * PAGE_SIZE, (g + 1) * PAGE_SIZE)
        xk_sc[rows, :] = page[0, 0].reshape(W_KV, PAGE_SIZE).T
        xv_sc[rows, :] = page[0, 1].reshape(W_KV, PAGE_SIZE).T
    tc = g_pages * (PAGE_SIZE // CMP_STRIDE)
    sc = (xk_sc, xv_sc)
    _compress_math(lambda br, l: sc[br][pl.ds(l, tc, stride=CMP_STRIDE), :],
                   lambda br, l: sc[br][pl.ds(tc * CMP_STRIDE + l, 8, stride=CMP_STRIDE), :], *rest, tc)


def _compress_paged(page_table, nsa_t, wts):
    s, n_pages = page_table.shape
    g_pages = next(g for g in CMP_PAGES if n_pages % g == 0)
    tc = g_pages * (PAGE_SIZE // CMP_STRIDE)
    nc = n_pages * (PAGE_SIZE // CMP_STRIDE)
    page = lambda k: pl.BlockSpec(
        (1, 2) + nsa_t.shape[2:], lambda b, i, pt: (pt[b, jnp.minimum(i * g_pages + k, n_pages - 1)], 0, 0, 0, 0))
    full = lambda a: pl.BlockSpec(a.shape, lambda b, i, pt: (0,) * a.ndim)
    consts = [wts["cpos"], wts["cwa"], wts["cwb"], wts["cw2"], wts["gkc"], wts["bd"]]
    rows_sc = pltpu.VMEM(((g_pages + 1) * PAGE_SIZE, LANES), F32)
    return pl.pallas_call(
        functools.partial(_compress_paged_kernel, g_pages=g_pages),
        grid_spec=pltpu.PrefetchScalarGridSpec(
            num_scalar_prefetch=1, grid=(s, n_pages // g_pages),
            in_specs=[page(k) for k in range(g_pages + 1)] + [full(a) for a in consts],
            out_specs=[pl.BlockSpec((1, tc, LANES), lambda b, i, pt: (b, i, 0))] * 2,
            scratch_shapes=[rows_sc, rows_sc]),
        out_shape=[jax.ShapeDtypeStruct((s, nc, LANES), BF16)] * 2,
        compiler_params=_cparams("parallel", "arbitrary"),
    )(page_table, *([nsa_t] * (g_pages + 1)), *consts)


def _compress(rows_k, rows_v, wts, tc):
    b, length, _ = rows_k.shape
    nc = length // CMP_STRIDE
    nt = nc // tc
    last_h = nc // 8 - 1
    full = lambda a: pl.BlockSpec(a.shape, lambda i, j: (0,) * a.ndim)
    main = pl.BlockSpec((1, tc * CMP_STRIDE, LANES), lambda i, j: (i, j, 0))
    halo = pl.BlockSpec((1, 8 * CMP_STRIDE, LANES), lambda i, j: (i, jnp.minimum((j + 1) * (tc // 8), last_h), 0))
    ins = [rows_k, rows_v, rows_k, rows_v, wts["cpos"], wts["cwa"], wts["cwb"], wts["cw2"], wts["gkc"], wts["bd"]]
    in_specs = [main, main, halo, halo] + [full(a) for a in ins[4:]]
    out_shape = [jax.ShapeDtypeStruct((b, nc, LANES), BF16)] * 2
    out_specs = [pl.BlockSpec((1, tc, LANES), lambda i, j: (i, j, 0))] * 2
    return pl.pallas_call(
        functools.partial(_compress_kernel, tc=tc),
        grid=(b, nt), in_specs=in_specs, out_specs=out_specs, out_shape=out_shape,
        compiler_params=_cparams("parallel", "arbitrary"),
    )(*ins)


def _flash_init(m_ref, l_ref, acc_ref):
    m_ref[...] = jnp.full(m_ref.shape, NEG_INF, F32)
    l_ref[...] = jnp.zeros(l_ref.shape, F32)
    acc_ref[...] = jnp.zeros(acc_ref.shape, F32)


def _flash_step(s, valid, v, m_ref, l_ref, acc_ref, i):
    s = jnp.where(valid, s, NEG_INF)
    m_old = m_ref[i]
    m_new = jnp.maximum(m_old, jnp.max(s, axis=1, keepdims=True))
    m_safe = jnp.where(m_new == NEG_INF, 0.0, m_new)
    alpha = jnp.exp(m_old - m_safe)
    p = jnp.exp(s - m_safe[:, :1])
    l_ref[i] = alpha * l_ref[i] + jnp.sum(p, axis=1, keepdims=True)
    acc_ref[i] = alpha * acc_ref[i] + _dot(p.astype(BF16), v)
    m_ref[i] = m_new


def _flash_out(l_ref, acc_ref, i):
    l = l_ref[i]
    return acc_ref[i] / jnp.where(l > 0, l, 1.0)


def _topk_mask(v, idx_iota, k):
    width = v.shape[1]

    def body(_, carry):
        v, sel = carry
        mx = jnp.max(v, axis=1, keepdims=True)
        idx = jnp.min(jnp.where(v == mx, idx_iota, width), axis=1, keepdims=True)
        hit = idx_iota == idx
        sel = jnp.where(hit & (mx > NEG_INF), 1.0, sel)
        return jnp.where(hit, NEG_INF, v), sel

    return lax.fori_loop(0, k, body, (v, jnp.zeros(v.shape, F32)))[1]


def _cmp_to_block_matrix(ncp, nsp, n_cmp, n_s):
    m = np.zeros((ncp, nsp), np.float32)
    per = SLC_BLOCK // CMP_STRIDE
    for i in range(n_s * per):
        for t in (i, i - 1):
            if 0 <= t < n_cmp:
                m[t, i // per] += 1.0
    return jnp.asarray(m)


MASK_BIG = 2.0 ** 30
AUX_R = LANES - 1


def _key_aux_table(t, block):
    assert t // block <= AUX_R and block <= 256
    key = np.arange(t)
    tab = np.zeros((t, LANES), np.float32)
    tab[key, key // block] = 1.0
    tab[:, AUX_R] = key % block
    return jnp.asarray(tab, BF16)


KEY_CHUNK = 64


def _flash_t_step(s_ref, p_ref, mask, v_t, m_ref, l_ref, acc_ref, i):
    n_keys, n_q = s_ref.shape
    chunks = [slice(r, r + KEY_CHUNK) for r in range(0, n_keys, KEY_CHUNK)]

    def scores(rows):
        s = s_ref[rows, :]
        if mask is not None:
            s = jnp.where(mask[rows, :], s, -MASK_BIG)
        return s.reshape(KEY_CHUNK // 8, 8, n_q)

    m_old = m_ref[i]
    m8 = None
    for rows in chunks:
        cm = jnp.max(scores(rows), axis=0)
        m8 = cm if m8 is None else jnp.maximum(m8, cm)
    m_new = jnp.maximum(m_old, jnp.max(m8, axis=0, keepdims=True))
    alpha = jnp.exp(m_old - m_new)
    l8 = None
    for rows in chunks:
        p = jnp.exp(scores(rows) - m_new[None])
        ps = jnp.sum(p, axis=0)
        l8 = ps if l8 is None else l8 + ps
        p_ref[rows, :] = p.reshape(KEY_CHUNK, n_q).astype(BF16)
    l_ref[i] = alpha * l_ref[i] + l8
    acc_ref[i] = alpha[0:1] * acc_ref[i] + _dot(v_t, p_ref[...])
    m_ref[i] = m_new


def _flash_t_out(l_ref, acc_ref, i):
    l = jnp.sum(l_ref[i], axis=0, keepdims=True)
    return acc_ref[i] / jnp.where(l > 0, l, 1.0)


def _topk_mask_t(v, idx, k):
    n = v.shape[0]

    def body(_, carry):
        v, sel = carry
        mx = jnp.max(v, axis=0, keepdims=True)
        first = jnp.min(jnp.where(v == mx, idx, n), axis=0, keepdims=True)
        hit = idx == first
        sel = jnp.where(hit & (mx > NEG_INF), 1.0, sel)
        return jnp.where(hit, NEG_INF, v), sel

    return lax.fori_loop(0, k, body, (v, jnp.zeros(v.shape, F32)))[1]


def _attn_prompt_kernel(qa_ref, qb_ref, gb_ref, km_ref, ka_ref, va_ref, kc_ref, vc_ref, ks_ref, vs_ref,
                        kw_ref, vw_ref, auxa_ref, auxs_ref, mm_ref, oa_ref, ob_ref,
                        m_sc, l_sc, acc_sc, oc_sc, os_sc, qx_sc, s_sc, p_sc, *, n_cmp, nb, n_s, k_a, k_s):
    c = pl.program_id(1)
    tq = MOBA_BLOCK
    sub = lax.broadcasted_iota(jnp.int32, (LANES, tq), 0)
    is_lo = sub < HEAD_DIM
    is_r = sub == AUX_R
    key_r = lax.broadcasted_iota(jnp.int32, (tq, tq), 0)
    qry_r = lax.broadcasted_iota(jnp.int32, (tq, tq), 1)
    causal = key_r <= qry_r

    def masked_q(ref, tile, half):
        q = ref[0, tile * LANES:(tile + 1) * LANES, :]
        keep = is_lo if half == 0 else jnp.logical_not(is_lo)
        return jnp.where(keep, q, jnp.zeros_like(q))

    def set_query(i, q_t, aux):
        qx_sc[i, :LANES, :] = q_t
        qx_sc[i, LANES:, :] = aux.astype(BF16)

    def topk_rows(v, k, n_rows):
        r = _round_up(n_rows, 8)
        tmp, out = s_sc.at[0], s_sc.at[1]
        tmp[:LANES, :] = v
        out[:LANES, :] = jnp.zeros((LANES, tq), F32)
        out[:r, :] = _topk_mask_t(tmp[:r, :], lax.broadcasted_iota(jnp.int32, (r, tq), 0), k)
        return out[:LANES, :]

    def flash_init():
        m_sc[...] = jnp.full(m_sc.shape, NEG_INF, F32)
        l_sc[...] = jnp.zeros(l_sc.shape, F32)
        acc_sc[...] = jnp.zeros(acc_sc.shape, F32)

    def sweep(k_ref, k_lanes, aux_ref, v_ref, v_rows, heads, blocks, mask):
        units = [(n, i) for n in blocks for i in heads]
        n_buf = s_sc.shape[0]

        def stage_scores(pos):
            n, i = units[pos]
            koff = pl.multiple_of(n * tq, tq)
            kcat = jnp.concatenate([k_ref[0, pl.ds(koff, tq), k_lanes(i)], aux_ref[pl.ds(koff, tq), :]], axis=1)
            s_sc[pos % n_buf] = _dot(kcat, qx_sc[i])

        for pos in range(min(n_buf - 1, len(units))):
            stage_scores(pos)
        for pos, (n, i) in enumerate(units):
            if pos + n_buf - 1 < len(units):
                stage_scores(pos + n_buf - 1)
            _flash_t_step(s_sc.at[pos % n_buf], p_sc.at[pos % 2], mask, v_ref[0, n, v_rows(i), :],
                          m_sc, l_sc, acc_sc, i)

    def past_loop(blocks_fn):
        def body(n2, carry):
            blocks_fn([2 * n2, 2 * n2 + 1])
            return carry
        lax.fori_loop(0, c // 2, body, 0)
        pl.when(c % 2 == 1)(lambda: blocks_fn([c - 1]))

    blk = sub
    back = (c - blk).astype(F32)
    for h in range(H_A):
        j, half = divmod(h, 2)
        q_t = masked_q(qa_ref, j, half)
        gate = _dot(km_ref[0, :, j * LANES:(j + 1) * LANES].astype(BF16), q_t)
        sel = topk_rows(jnp.where(blk < c, gate, NEG_INF), k_a, nb)
        bias = jnp.where(sel > 0.5, 0.0, -MASK_BIG) - (SLOPES_A[h] * tq) * back
        set_query(h, q_t, jnp.where(is_r, SLOPES_A[h], jnp.where(blk < c, bias, 0.0)))

    pair_lanes = lambda i: slice((i // 2) * LANES, (i // 2 + 1) * LANES)
    all_lanes = lambda i: slice(0, LANES)
    heads = range(H_A)
    flash_init()
    sweep(ka_ref, pair_lanes, auxa_ref, va_ref, pair_lanes, heads, [c], causal)
    past_loop(lambda blocks: sweep(ka_ref, pair_lanes, auxa_ref, va_ref, pair_lanes, heads, blocks, None))
    for j in range(H_A // 2):
        o = jnp.where(is_lo, _flash_t_out(l_sc, acc_sc, 2 * j), _flash_t_out(l_sc, acc_sc, 2 * j + 1))
        oa_ref[0, j * LANES:(j + 1) * LANES, :] = o.astype(BF16)

    ncp = kc_ref.shape[1]
    nsp = mm_ref.shape[0]
    jc = lax.broadcasted_iota(jnp.int32, (ncp, tq), 0)
    d_c = c * tq + lax.broadcasted_iota(jnp.int32, (ncp, tq), 1) - (jc * CMP_STRIDE + (CMP_LEN - 1))
    c_valid = (d_c >= 0) & (jc < n_cmp)
    d_cf = d_c.astype(F32)
    sblk = lax.broadcasted_iota(jnp.int32, (nsp, tq), 0)
    cur_s = (c * tq + lax.broadcasted_iota(jnp.int32, (nsp, tq), 1)) // SLC_BLOCK
    forced = (sblk == 0) | (sblk >= cur_s - 1)
    causal_s = sblk <= cur_s
    rel_s = (sblk - c * (tq // SLC_BLOCK)).astype(F32)
    for k in range(H_KV):
        imp = jnp.zeros((ncp, tq), F32)
        for g in range(GROUP):
            i = k * GROUP + g
            s = _dot(kc_ref[0], masked_q(qb_ref, g, k)) - SLOPES_B[i] * d_cf
            s = jnp.where(c_valid, s, NEG_INF)
            m = jnp.max(s, axis=0, keepdims=True)
            m = jnp.where(m == NEG_INF, 0.0, m)
            e = jnp.where(c_valid, jnp.exp(s - m), 0.0)
            d = jnp.sum(e, axis=0, keepdims=True)
            p = e / jnp.where(d > 0, d, 1.0)
            imp = imp + p
            oc_sc[i] = _dot(vc_ref[0], p.astype(BF16))
        impb = jnp.dot(mm_ref[...], imp, preferred_element_type=F32, precision=lax.Precision.HIGHEST)
        impb = jnp.where(forced, impb + FORCE_BONUS, impb)
        impb = jnp.where(causal_s, impb, NEG_INF)
        sel = topk_rows(impb, k_s, n_s)
        for g in range(GROUP):
            i = k * GROUP + g
            bias = jnp.where(sel > 0.5, 0.0, -MASK_BIG) + (SLOPES_B[i] * SLC_BLOCK) * rel_s
            set_query(i, masked_q(qb_ref, g, k), jnp.where(is_r, SLOPES_B[i], bias))

    heads = range(H_B)
    flash_init()
    sweep(ks_ref, all_lanes, auxs_ref, vs_ref, all_lanes, heads, [c], causal)
    past_loop(lambda blocks: sweep(ks_ref, all_lanes, auxs_ref, vs_ref, all_lanes, heads, blocks, None))
    for i in range(H_B):
        os_sc[i] = _flash_t_out(l_sc, acc_sc, i)

    for i in range(H_B):
        qx_sc[i, LANES:, :] = jnp.where(is_r, SLOPES_B[i], -(SLOPES_B[i] * tq) * back).astype(BF16)
    flash_init()
    sweep(kw_ref, all_lanes, auxa_ref, vw_ref, all_lanes, heads, [c], causal)
    n_back = WINDOW // tq
    for b in range(1, n_back + 1):
        mask = None if b < n_back else key_r > qry_r
        pl.when(c >= b)(functools.partial(sweep, kw_ref, all_lanes, auxa_ref, vw_ref, all_lanes, heads, [c - b], mask))

    gb = gb_ref[0]
    for j in range(GROUP):
        def comb(i):
            return (gb[3 * i:3 * i + 1] * oc_sc[i] + gb[3 * i + 1:3 * i + 2] * os_sc[i]
                    + gb[3 * i + 2:3 * i + 3] * _flash_t_out(l_sc, acc_sc, i))
        ob_ref[0, j * LANES:(j + 1) * LANES, :] = jnp.where(is_lo, comb(j), comb(GROUP + j)).astype(BF16)


def _attn_prompt(p, kc, vc, b, t):
    tq = MOBA_BLOCK
    nb = t // tq
    n_s = t // SLC_BLOCK
    n_cmp = t // CMP_STRIDE - 1
    nsp = _round_up(n_s, LANES)
    ncp = kc.shape[1]
    assert nb <= AUX_R and n_s <= AUX_R and nsp == LANES and WINDOW % tq == 0
    km = jnp.pad(p["kmean"].reshape(b, nb, W_A), ((0, 0), (0, LANES - nb), (0, 0)))
    mm = _cmp_to_block_matrix(ncp, nsp, n_cmp, n_s).T
    r3 = lambda a: a.reshape(b, t, a.shape[-1])
    tr = lambda a: r3(a).transpose(0, 2, 1)
    qtile = lambda w: pl.BlockSpec((1, w, tq), lambda i, c: (i, 0, c))
    trb = lambda a: a.reshape(b, nb, tq, a.shape[-1]).transpose(0, 1, 3, 2)
    seq = lambda a: pl.BlockSpec((1,) + a.shape[1:], lambda i, c: (i,) + (0,) * (a.ndim - 1))
    const = lambda a: pl.BlockSpec(a.shape, lambda i, c: (0, 0))
    ins = [tr(p["qa"]), tr(p["qb"]), tr(p["gb"]), km, r3(p["kab"]), trb(p["vab"]), kc, vc.transpose(0, 2, 1),
           r3(p["ksb"]), trb(p["vsb"]), r3(p["kwb"]), trb(p["vwb"]),
           _key_aux_table(t, MOBA_BLOCK), _key_aux_table(t, SLC_BLOCK), mm]
    in_specs = [qtile(W_A), qtile(W_B), qtile(LANES)] + [seq(a) for a in ins[3:12]] + [const(a) for a in ins[12:]]
    out_shape = [jax.ShapeDtypeStruct((b, W_A, t), BF16), jax.ShapeDtypeStruct((b, W_B, t), BF16)]
    out_specs = [qtile(W_A), qtile(W_B)]
    scratch = [
        pltpu.VMEM((H_A, 8, tq), F32), pltpu.VMEM((H_A, 8, tq), F32), pltpu.VMEM((H_A, LANES, tq), F32),
        pltpu.VMEM((H_B, LANES, tq), F32), pltpu.VMEM((H_B, LANES, tq), F32), pltpu.VMEM((H_A, 2 * LANES, tq), BF16),
        pltpu.VMEM((8, tq, tq), F32), pltpu.VMEM((2, tq, tq), BF16),
    ]
    oa, ob = pl.pallas_call(
        functools.partial(_attn_prompt_kernel, n_cmp=n_cmp, nb=nb, n_s=n_s,
                          k_a=min(MOBA_TOPK, nb), k_s=min(SLC_TOPK, n_s)),
        grid=(b, nb), in_specs=in_specs, out_specs=out_specs, out_shape=out_shape, scratch_shapes=scratch,
        compiler_params=_cparams("parallel", "arbitrary"),
    )(*ins)
    return oa.transpose(0, 2, 1).reshape(b * t, W_A), ob.transpose(0, 2, 1).reshape(b * t, W_B)


MOBA_SAMPLE_PAGES = 8


def _moba_sample_tables(t_new):
    rows = H_A * t_new
    head = np.arange(rows) // t_new
    qidx = np.arange(rows) % t_new
    slope = np.asarray(SLOPES_A, np.float32)[head]
    b0 = (slope[:, None] * np.arange(PAGE_SIZE)[None, :]).astype(np.float32)
    coln = np.arange(t_new * H_A)
    ok = (head[:, None] == (coln % H_A)[None, :]) & ((coln // H_A)[None, :] <= qidx[:, None])
    bnew = np.where(ok, slope[:, None] * (coln // H_A)[None, :], -MASK_BIG).astype(np.float32)
    own = (head[:, None] == (np.arange(W_A) // HEAD_DIM)[None, :]).astype(np.float32)
    slope_l = np.broadcast_to(slope[:, None], (rows, LANES)).astype(np.float32)
    return jnp.asarray(b0), jnp.asarray(bnew), jnp.asarray(own), jnp.asarray(slope_l)


def _moba_sample_kernel(pt_ref, qbd_ref, q_ref, b0_ref, bnew_ref, own_ref, slope_ref, new_ref, *rest, n_pages, k_a):
    g_pages = MOBA_SAMPLE_PAGES
    page_refs, o_ref = rest[:g_pages], rest[g_pages]
    g_sc, m_sc, l_sc, o_sc = rest[g_pages + 1:]
    del pt_ref
    step = pl.program_id(1)
    qbd = qbd_ref[0]
    rows = qbd.shape[0]
    slope = slope_ref[:, 0:1]

    def softmax_partial(s):
        m = jnp.max(s, axis=1, keepdims=True)
        e = jnp.exp(s - m)
        return m, jnp.sum(e, axis=1, keepdims=True), e.astype(BF16)

    k_all = jnp.concatenate([pg[0, 0].reshape(W_A, PAGE_SIZE).astype(BF16) for pg in page_refs], axis=1)
    qk_all = _dot(qbd, k_all)
    for g in range(g_pages):
        p = step * g_pages + g
        v_t = page_refs[g][0, 1].reshape(W_A, PAGE_SIZE).astype(BF16)
        qk = qk_all[:, g * PAGE_SIZE:(g + 1) * PAGE_SIZE]
        g_sc[p] = jnp.broadcast_to(jnp.sum(qk, axis=1, keepdims=True), (rows, LANES))
        m, l, e = softmax_partial(qk + b0_ref[...] + slope * ((p - n_pages) * PAGE_SIZE).astype(F32))
        m_sc[p] = jnp.broadcast_to(m, (rows, LANES))
        l_sc[p] = jnp.broadcast_to(l, (rows, LANES))
        o_sc[p] = _dot_t(e, v_t)

    @pl.when(step == pl.num_programs(1) - 1)
    def _():
        per_blk = MOBA_BLOCK // PAGE_SIZE
        nb = n_pages // per_blk
        gate = jnp.sum(g_sc[...].reshape(nb, per_blk, rows, LANES), axis=1)
        sel = _topk_mask_t(gate, lax.broadcasted_iota(jnp.int32, gate.shape, 0), k_a)
        sel = jnp.broadcast_to(sel[:, None], (nb, per_blk, rows, LANES)).reshape(n_pages, rows, LANES) > 0.5
        t_new = new_ref.shape[1]
        k_new = new_ref[0, :, 0].reshape(t_new * H_A, HEAD_DIM).astype(BF16)
        v_new = new_ref[0, :, 1].reshape(t_new * H_A, HEAD_DIM).astype(BF16)
        m_o, l_o, e_o = softmax_partial(_dot_t(q_ref[0], k_new) + bnew_ref[...])
        o_o = _dot(e_o, v_new)
        m_all = m_sc[...]
        m_star = jnp.maximum(jnp.max(jnp.where(sel, m_all, -MASK_BIG), axis=0), m_o)
        w = jnp.where(sel, jnp.exp(m_all - m_star[None]), 0.0)
        w_o = jnp.exp(m_o - m_star)
        l_star = jnp.sum(w * l_sc[...], axis=0) + w_o * l_o
        n_rep = W_A // LANES
        w_wide = jnp.concatenate([w] * n_rep, axis=2)
        o_star = jnp.sum(w_wide * o_sc[...], axis=0) + (
            jnp.concatenate([w_o[:, :HEAD_DIM]] * H_A, axis=1) * jnp.concatenate([o_o] * H_A, axis=1))
        o_ref[0] = (o_star / jnp.concatenate([l_star] * n_rep, axis=1)) * own_ref[...]


def _moba_sample(p, cache_moba, page_table, s, t_new):
    n_pages = page_table.shape[1]
    g_pages = MOBA_SAMPLE_PAGES
    assert n_pages % g_pages == 0 and n_pages % (MOBA_BLOCK // PAGE_SIZE) == 0
    rows = H_A * t_new
    nb = n_pages // (MOBA_BLOCK // PAGE_SIZE)
    cache_t = cache_moba.transpose(0, 2, 3, 4, 1)
    q4 = p["qa"].reshape(s, t_new, H_A, HEAD_DIM).transpose(0, 2, 1, 3)
    q = q4.reshape(s, rows, HEAD_DIM)
    eye = jnp.eye(H_A, dtype=q4.dtype)
    qbd = (q4[:, :, :, None, :] * eye[None, :, None, :, None]).reshape(s, rows, W_A)
    new = p["kva"].reshape(s, t_new, 2, H_A, HEAD_DIM)
    tables = _moba_sample_tables(t_new)
    const = lambda a: pl.BlockSpec(a.shape, lambda b, i, pt: (0, 0))
    seq = lambda a: pl.BlockSpec((1,) + a.shape[1:], lambda b, i, pt: (b,) + (0,) * (a.ndim - 1))
    page = lambda g: pl.BlockSpec((1,) + cache_t.shape[1:], lambda b, i, pt: (pt[b, i * g_pages + g], 0, 0, 0, 0))
    out = pl.pallas_call(
        functools.partial(_moba_sample_kernel, n_pages=n_pages, k_a=min(MOBA_TOPK, nb)),
        grid_spec=pltpu.PrefetchScalarGridSpec(
            num_scalar_prefetch=1, grid=(s, n_pages // g_pages),
            in_specs=[seq(qbd), seq(q)] + [const(a) for a in tables] + [seq(new)] + [page(g) for g in range(g_pages)],
            out_specs=pl.BlockSpec((1, rows, W_A), lambda b, i, pt: (b, 0, 0)),
            scratch_shapes=[pltpu.VMEM((n_pages, rows, LANES), F32)] * 3 + [pltpu.VMEM((n_pages, rows, W_A), F32)]),
        out_shape=jax.ShapeDtypeStruct((s, rows, W_A), F32),
        compiler_params=_cparams("parallel", "arbitrary"),
    )(page_table, qbd, q, *tables, new, *([cache_t] * g_pages))
    return out.reshape(s, H_A, t_new, W_A).sum(axis=1).reshape(s * t_new, W_A).astype(BF16)


SAMPLE_KV_TILES = (4096, 2048)


def _attn_sample_kernel(pt_ref, qb16_ref, g16_ref, kc_ref, vc_ref, mm_ref, ksn_ref, vsn_ref, kwa_ref, vwa_ref, *rest,
                        past, t_new, k_s, r_tile):
    del pt_ref
    g_pages = r_tile // PAGE_SIZE
    page_refs = rest[:g_pages]
    ob_ref, mb_sc, lb_sc, accb_sc, sels_sc, oc_sc, os_sc = rest[g_pages:]
    kt = pl.program_id(1)
    rb = GROUP * t_new
    row_b = lax.broadcasted_iota(jnp.int32, (rb, 1), 0)
    t_b = past + row_b % t_new
    slope_b = []
    for k in range(H_KV):
        sl = jnp.full((rb, 1), SLOPES_B[k * GROUP + GROUP - 1], F32)
        for g in range(GROUP - 2, -1, -1):
            sl = jnp.where(row_b < (g + 1) * t_new, SLOPES_B[k * GROUP + g], sl)
        slope_b.append(sl)
    nsp = mm_ref.shape[1]
    cur_s = past // SLC_BLOCK

    @pl.when(kt == 0)
    def _():
        _flash_init(mb_sc, lb_sc, accb_sc)
        nc = kc_ref.shape[1]
        jc = lax.broadcasted_iota(jnp.int32, (rb, nc), 1)
        d_c = t_b - (jc * CMP_STRIDE + (CMP_LEN - 1))
        c_valid = d_c >= 0
        d_cf = d_c.astype(F32)
        sj = lax.broadcasted_iota(jnp.int32, (rb, nsp), 1)
        forced = (sj == 0) | (sj >= cur_s - 1)
        causal_s = sj <= cur_s
        gr = lax.broadcasted_iota(jnp.int32, (rb, rb), 0) % t_new
        gc = lax.broadcasted_iota(jnp.int32, (rb, rb), 1) % t_new
        group_sum = (gr == gc).astype(F32)
        for k in range(H_KV):
            s = _dot_t(qb16_ref[0, k], kc_ref[0]) - slope_b[k] * d_cf
            s = jnp.where(c_valid, s, NEG_INF)
            m = jnp.max(s, axis=1, keepdims=True)
            m = jnp.where(m == NEG_INF, 0.0, m)
            e = jnp.where(c_valid, jnp.exp(s - m), 0.0)
            d = jnp.sum(e, axis=1, keepdims=True)
            p = e / jnp.where(d > 0, d, 1.0)
            oc_sc[k] = _dot(p.astype(BF16), vc_ref[0])
            pb = jnp.dot(p, mm_ref[...], preferred_element_type=F32, precision=lax.Precision.HIGHEST)
            impb = jnp.dot(group_sum, pb, preferred_element_type=F32, precision=lax.Precision.HIGHEST)
            impb = jnp.where(forced, impb + FORCE_BONUS, impb)
            impb = jnp.where(causal_s, impb, NEG_INF)
            sels_sc[k] = _topk_mask(impb, sj, k_s)

    blocks_per_tile = r_tile // SLC_BLOCK
    first_blk = kt * blocks_per_tile
    lane_tile = pl.multiple_of((first_blk // LANES) * LANES, LANES)
    e_row = lax.broadcasted_iota(jnp.int32, (LANES, r_tile), 0)
    e_col = lax.broadcasted_iota(jnp.int32, (LANES, r_tile), 1) // SLC_BLOCK
    expand = (e_row == first_blk % LANES + e_col).astype(BF16)
    col_t = lax.broadcasted_iota(jnp.int32, (rb, r_tile), 1)
    dist_t = (t_b - (kt * r_tile + col_t)).astype(F32)
    k_all = jnp.concatenate([pg[0, 0].reshape(W_KV, PAGE_SIZE).astype(BF16) for pg in page_refs], axis=1)
    v_all = jnp.concatenate([pg[0, 1].reshape(W_KV, PAGE_SIZE).astype(BF16) for pg in page_refs], axis=1)
    for k in range(H_KV):
        sel_t = sels_sc[k, :, pl.ds(lane_tile, LANES)]
        valid = _dot(sel_t.astype(BF16), expand) > 0.5
        s = _dot(qb16_ref[0, k], k_all) - slope_b[k] * dist_t
        s = jnp.where(valid, s, NEG_INF)
        m_old = mb_sc[k]
        m_new = jnp.maximum(m_old, jnp.max(s, axis=1, keepdims=True))
        m_safe = jnp.where(m_new == NEG_INF, 0.0, m_new)
        alpha = jnp.exp(m_old - m_safe)
        p = jnp.exp(s - m_safe[:, :1])
        lb_sc[k] = alpha * lb_sc[k] + jnp.sum(p, axis=1, keepdims=True)
        accb_sc[k] = alpha * accb_sc[k] + _dot_t(p.astype(BF16), v_all)
        mb_sc[k] = m_new

    @pl.when(kt == pl.num_programs(1) - 1)
    def _():
        dist_b = row_b % t_new - lax.broadcasted_iota(jnp.int32, (rb, LANES), 1)
        sj = lax.broadcasted_iota(jnp.int32, (rb, nsp), 1)
        for k in range(H_KV):
            own = jnp.sum(jnp.where(sj == cur_s, sels_sc[k], 0.0), axis=1, keepdims=True) > 0.5
            s = _dot_t(qb16_ref[0, k], ksn_ref[0]) - slope_b[k] * dist_b.astype(F32)
            _flash_step(s, own & (dist_b >= 0), vsn_ref[0], mb_sc, lb_sc, accb_sc, k)
            os_sc[k] = _flash_out(lb_sc, accb_sc, k)
        _flash_init(mb_sc, lb_sc, accb_sc)
        n_w = kwa_ref.shape[1]
        dist_w = WINDOW + row_b % t_new - lax.broadcasted_iota(jnp.int32, (rb, n_w), 1)
        valid_w = (dist_w >= 0) & (dist_w < WINDOW)
        lane_b = lax.broadcasted_iota(jnp.int32, (rb, LANES), 1)
        comb = []
        for k in range(H_KV):
            s = _dot_t(qb16_ref[0, k], kwa_ref[0]) - slope_b[k] * dist_w.astype(F32)
            _flash_step(s, valid_w, vwa_ref[0], mb_sc, lb_sc, accb_sc, k)
            g = g16_ref[0, k]
            comb.append(g[:, 0:1] * oc_sc[k] + g[:, 1:2] * os_sc[k] + g[:, 2:3] * _flash_out(lb_sc, accb_sc, k))
        ob_ref[0] = jnp.where(lane_b < HEAD_DIM, comb[0], comb[1])


def _attn_sample(p, nsa_t, page_table, kc, vc, state_win, s, t_new, past):
    r_tile = next(r for r in SAMPLE_KV_TILES if past % r == 0)
    g_pages = r_tile // PAGE_SIZE
    assert state_win.shape[1] == WINDOW and LANES % (r_tile // SLC_BLOCK) == 0
    lp = _round_up(past + t_new, MOBA_BLOCK)
    n_s = lp // SLC_BLOCK
    n_cmp = lp // CMP_STRIDE - 1
    nsp = _round_up(n_s, LANES)
    nc = kc.shape[1]
    mm = _cmp_to_block_matrix(nc, nsp, min(n_cmp, nc), n_s)
    lane = jnp.arange(LANES)
    lo = (lane < HEAD_DIM)

    qb = p["qb"].reshape(s, t_new, GROUP, LANES).transpose(0, 2, 1, 3).reshape(s, 1, GROUP * t_new, LANES)
    zero = jnp.zeros_like(qb)
    qb16 = jnp.concatenate([jnp.where(lo, qb, zero), jnp.where(lo, zero, qb)], axis=1)
    g16 = p["gb"][:, :3 * H_B].reshape(s, t_new, H_KV, GROUP, 3).transpose(0, 2, 3, 1, 4)
    g16 = jnp.pad(g16.reshape(s, H_KV, GROUP * t_new, 3), ((0, 0), (0, 0), (0, 0), (0, LANES - 3)))

    pad_new = lambda a: jnp.pad(a.reshape(s, t_new, a.shape[-1]), ((0, 0), (0, LANES - t_new), (0, 0)))
    win = state_win.reshape(s, WINDOW, 2, W_KV).astype(BF16)
    kwa = jnp.concatenate([win[:, :, 0], pad_new(p["kwb"])], axis=1)
    vwa = jnp.concatenate([win[:, :, 1], pad_new(p["vwb"])], axis=1)

    ins = [qb16, g16, kc, vc, mm, pad_new(p["ksb"]), pad_new(p["vsb"]), kwa, vwa]
    seq = lambda a: pl.BlockSpec((1,) + a.shape[1:], lambda i, k, pt: (i,) + (0,) * (a.ndim - 1))
    page = lambda g: pl.BlockSpec((1, 2) + nsa_t.shape[2:], lambda i, k, pt: (pt[i, k * g_pages + g], 1, 0, 0, 0))
    in_specs = ([seq(a) for a in ins[:4]] + [pl.BlockSpec(mm.shape, lambda i, k, pt: (0, 0))]
                + [seq(a) for a in ins[5:]] + [page(g) for g in range(g_pages)])
    rb = GROUP * t_new
    scratch = [
        pltpu.VMEM((H_KV, rb, LANES), F32), pltpu.VMEM((H_KV, rb, LANES), F32), pltpu.VMEM((H_KV, rb, LANES), F32),
        pltpu.VMEM((H_KV, rb, nsp), F32), pltpu.VMEM((H_KV, rb, LANES), F32), pltpu.VMEM((H_KV, rb, LANES), F32),
    ]
    ob16 = pl.pallas_call(
        functools.partial(_attn_sample_kernel, past=past, t_new=t_new, k_s=min(SLC_TOPK, n_s), r_tile=r_tile),
        grid_spec=pltpu.PrefetchScalarGridSpec(
            num_scalar_prefetch=1, grid=(s, past // r_tile), in_specs=in_specs,
            out_specs=pl.BlockSpec((1, rb, LANES), lambda i, k, pt: (i, 0, 0)), scratch_shapes=scratch),
        out_shape=jax.ShapeDtypeStruct((s, rb, LANES), F32),
        compiler_params=_cparams("parallel", "arbitrary"),
    )(page_table, *ins, *([nsa_t] * g_pages))
    return ob16.reshape(s, GROUP, t_new, LANES).transpose(0, 2, 1, 3).reshape(s * t_new, W_B).astype(BF16)


def _post_kernel(oa_ref, ob_ref, gm_ref, x_ref, wba_ref, wbb_ref, wout_ref, gnf_ref, wg_ref, wu_ref, wd_ref,
                 y_ref, h_sc, hn_sc, acc_sc):
    j = pl.program_id(1)

    @pl.when(j == 0)
    def _():
        ma = _dot(oa_ref[...], wba_ref[...])
        mb = _dot(ob_ref[...], wbb_ref[...])
        merged = gm_ref[:, :D_MODEL] * ma + gm_ref[:, D_MODEL:] * mb
        h = x_ref[...] + _dot(merged.astype(BF16), wout_ref[...])
        h_sc[...] = h
        r = lax.rsqrt(jnp.mean(h * h, axis=-1, keepdims=True) + RMS_EPS)
        hn_sc[...] = ((h * r) * gnf_ref[...]).astype(BF16)
        acc_sc[...] = jnp.zeros(acc_sc.shape, F32)

    hn = hn_sc[...]
    g = _dot(hn, wg_ref[...])
    u = _dot(hn, wu_ref[...])
    act = (g * jax.nn.sigmoid(g)) * u
    acc_sc[...] += _dot(act.astype(BF16), wd_ref[...])

    @pl.when(j == pl.num_programs(1) - 1)
    def _():
        y_ref[...] = h_sc[...] + acc_sc[...]


def _post(oa, ob, gm, x, wts):
    n = x.shape[0]
    tm = 512 if n % 512 == 0 else n
    d_ff = wts["wd"].shape[0]
    n_ff = 2 if (d_ff // 2) % LANES == 0 else 1
    fc = d_ff // n_ff
    rows = lambda w: pl.BlockSpec((tm, w), lambda i, j: (i, 0))
    full = lambda a: pl.BlockSpec(a.shape, lambda i, j: (0,) * a.ndim)
    ins = [oa, ob, gm, x, wts["wba"], wts["wbb"], wts["wout"], wts["gnf"], wts["wgate"], wts["wup"], wts["wd"]]
    in_specs = [rows(W_A), rows(W_B), rows(2 * D_MODEL), rows(D_MODEL)] + [full(a) for a in ins[4:8]] + [
        pl.BlockSpec((D_MODEL, fc), lambda i, j: (0, j)),
        pl.BlockSpec((D_MODEL, fc), lambda i, j: (0, j)),
        pl.BlockSpec((fc, D_MODEL), lambda i, j: (j, 0)),
    ]
    return pl.pallas_call(
        _post_kernel, grid=(n // tm, n_ff), in_specs=in_specs, out_specs=rows(D_MODEL),
        out_shape=jax.ShapeDtypeStruct((n, D_MODEL), F32),
        scratch_shapes=[pltpu.VMEM((tm, D_MODEL), F32), pltpu.VMEM((tm, D_MODEL), BF16), pltpu.VMEM((tm, D_MODEL), F32)],
        compiler_params=_cparams("parallel", "arbitrary"),
    )(*ins)


def _blockdiag2(w):
    z = jnp.zeros_like(w)
    return jnp.concatenate([jnp.concatenate([w, z], axis=-1), jnp.concatenate([z, w], axis=-1)], axis=-2)


def _prep_weights(g_na, w_in, b_in, g_qk_a, g_qk_b, cmp_pos, cmp_w1, cmp_w2, w_ba, w_bb, w_out, g_nf, w_up, w_down):
    def perm_qb_cols(a):
        lead = a.shape[:-1]
        return a.reshape(lead + (H_KV, GROUP, HEAD_DIM)).swapaxes(-3, -2).reshape(lead + (W_B,))

    def main_cols(a):
        return jnp.concatenate([a[..., :C_QB], perm_qb_cols(a[..., C_QB:C_KVB]), a[..., C_KVB:C_END]], axis=-1)

    ones = lambda n: jnp.ones((n,), F32)
    gv = jnp.concatenate([
        jnp.tile(g_qk_a[0], H_A), jnp.tile(g_qk_a[1], H_A), ones(W_A), jnp.tile(g_qk_b[0], H_B),
        ones(2 * W_KV), jnp.tile(g_qk_b[2], H_KV), ones(W_KV), jnp.tile(g_qk_b[3], H_KV), ones(W_KV)])
    n_gb = 3 * H_B
    hd_idx = np.arange(W_A) // HEAD_DIM
    bd = jnp.asarray((hd_idx[:, None] == hd_idx[None, :]).astype(np.float32)).astype(BF16)
    d_ff = w_down.shape[0]
    w1 = cmp_w1.astype(BF16)
    pos2 = jnp.concatenate([cmp_pos, cmp_pos], axis=-1)
    return {
        "gna": g_na.reshape(1, D_MODEL),
        "w1": main_cols(w_in).astype(BF16), "b1": main_cols(b_in).reshape(1, C_END), "gv": gv.reshape(1, C_END),
        "wg": jnp.pad(w_in[:, C_END:C_END + n_gb], ((0, 0), (0, LANES - n_gb))).astype(BF16),
        "bg": jnp.pad(b_in[C_END:C_END + n_gb], (0, LANES - n_gb)).reshape(1, LANES),
        "wgm": w_in[:, C_END + n_gb:].astype(BF16), "bgm": b_in[C_END + n_gb:].reshape(1, 2 * D_MODEL),
        "bd": bd,
        "cpos": pos2.reshape(2, 2, CMP_STRIDE, 1, LANES).swapaxes(0, 1),
        "cwa": _blockdiag2(w1[:, :CMP_STRIDE]), "cwb": _blockdiag2(w1[:, CMP_STRIDE:]),
        "cw2": _blockdiag2(cmp_w2.astype(BF16)),
        "gkc": jnp.tile(g_qk_b[1], H_KV).reshape(1, LANES),
        "wba": w_ba.astype(BF16),
        "wbb": w_bb.reshape(H_KV, GROUP, HEAD_DIM, D_MODEL).swapaxes(0, 1).reshape(W_B, D_MODEL).astype(BF16),
        "wout": w_out.astype(BF16), "gnf": g_nf.reshape(1, D_MODEL),
        "wgate": w_up[:, :d_ff].astype(BF16), "wup": w_up[:, d_ff:].astype(BF16), "wd": w_down.astype(BF16),
    }


def _prompt_layer(x, wts):
    b, t, _ = x.shape
    x2 = x.reshape(b * t, D_MODEL)
    p = _inproj(x2, wts, seq_len=t)
    nc = t // CMP_STRIDE
    kc, vc = _compress(p["cmpk"].reshape(b, t, W_KV), p["cmpv"].reshape(b, t, W_KV), wts, tc=nc)
    oa, ob = _attn_prompt(p, kc, vc, b, t)
    y = _post(oa, ob, p["gm"], x2, wts)
    keep = min(WINDOW, t)
    rows_major = lambda a: a.transpose(0, 4, 1, 2, 3)
    return (y.reshape(b, t, D_MODEL), rows_major(p["kva"]), rows_major(p["kvn"]),
            rows_major(p["kvw"][..., t - keep:]))


def _sample_layer(x, cache_moba, cache_nsa, state_win, page_table, wts):
    s, t_new, _ = x.shape
    past = page_table.shape[1] * PAGE_SIZE
    x2 = x.reshape(s * t_new, D_MODEL)
    p = _inproj(x2, wts)
    nsa_t = cache_nsa.transpose(0, 2, 3, 4, 1)
    kc, vc = _compress_paged(page_table, nsa_t, wts)
    oa = _moba_sample(p, cache_moba, page_table, s, t_new)
    ob = _attn_sample(p, nsa_t, page_table, kc, vc, state_win, s, t_new, past)
    y = _post(oa, ob, p["gm"], x2, wts)
    kvw_new = p["kvw"].reshape(s, t_new, 2, H_KV, HEAD_DIM)
    win = jnp.concatenate([state_win, kvw_new], axis=1)[:, t_new:]
    return (y.reshape(s, t_new, D_MODEL),
            p["kva"].reshape(s, t_new, 2, H_A, HEAD_DIM),
            p["kvn"].reshape(s, t_new, 4, H_KV, HEAD_DIM),
            win)


def kernel(x_prompt, x_sample, cache_moba_kv, cache_nsa_kv, state_win_kv, page_table, g_norm_attn, w_in, b_in,
           g_qk_moba, g_qk_nsa, cmp_pos, cmp_w1, cmp_w2, w_br_moba, w_br_nsa, w_out, g_norm_ffn, w_up, w_down):
    assert g_norm_attn.shape[0] == 1, "single-layer step"
    wts = _prep_weights(g_norm_attn[0], w_in[0], b_in[0], g_qk_moba[0], g_qk_nsa[0], cmp_pos[0], cmp_w1[0],
                        cmp_w2[0], w_br_moba[0], w_br_nsa[0], w_out[0], g_norm_ffn[0], w_up[0], w_down[0])
    y_p, a_p, n_p, wn_p = _prompt_layer(x_prompt, wts)
    y_s, a_s, n_s, wn_s = _sample_layer(x_sample, cache_moba_kv[0], cache_nsa_kv[0], state_win_kv[0], page_table, wts)
    return (y_p, y_s, a_p[None], n_p[None], wn_p[None], a_s[None], n_s[None], wn_s[None])
```

```python
import functools
import math

import numpy as np
import jax
import jax.numpy as jnp
from jax import lax
from jax.experimental import pallas as pl
from jax.experimental.pallas import tpu as pltpu

F32 = jnp.float32
BF16 = jnp.bfloat16

D_MODEL = 1024
PAGE_SIZE = 128
HEAD_DIM = 64
H_A = 8
H_B = 8
H_KV = 2
GROUP = H_B // H_KV
MOBA_BLOCK = 256
MOBA_TOPK = 3
CMP_LEN = 32
CMP_STRIDE = 16
SLC_BLOCK = 64
SLC_TOPK = 16
WINDOW = 512
CMP_HID = 2 * HEAD_DIM
FORCE_BONUS = 1e4
RMS_EPS = 1e-6
W_A = H_A * HEAD_DIM
W_B = H_B * HEAD_DIM
W_KV = H_KV * HEAD_DIM
QK_SCALE = HEAD_DIM ** -0.5

LANES = 128
VMEM_LIMIT = 56 * 1024 * 1024
NEG_INF = float("-inf")

SLOPES_A = tuple(2.0 ** (-8.0 * (i + 1) / H_A) for i in range(H_A))
SLOPES_B = tuple(2.0 ** (-8.0 * (i + 1) / H_B) for i in range(H_B))


def _round_up(x, m):
    return -(-x // m) * m


def _cparams(*sem):
    return pltpu.CompilerParams(dimension_semantics=sem, vmem_limit_bytes=VMEM_LIMIT)


def _dot(a, b):
    return jnp.dot(a, b, preferred_element_type=F32)


def _dot_t(a, b):
    return lax.dot_general(a, b, (((1,), (1,)), ((), ())), preferred_element_type=F32)


def _head_sumsq(z, bd):
    zz = z * z
    hi = zz.astype(BF16)
    lo = (zz - hi.astype(F32)).astype(BF16)
    return _dot(hi, bd) + _dot(lo, bd)


def _head_norm(z, g, bd):
    ss = _head_sumsq(z, bd)
    return (z * lax.rsqrt(ss * (1.0 / HEAD_DIM) + RMS_EPS)) * g


C_QA, C_KA, C_VA, C_QB, C_KVB, C_END = 0, 512, 1024, 1536, 2048, 2816


def _inproj_kernel(x_ref, gna_ref, w1_ref, b1_ref, gv_ref, wg_ref, bg_ref, wgm_ref, bgm_ref, bd_ref,
                   qa_ref, kva_ref, kab_ref, vab_ref, qb_ref, kvn_ref, cmpk_ref, cmpv_ref, ksb_ref, vsb_ref, kvw_ref,
                   kwb_ref, vwb_ref, gb_ref, gm_ref, *maybe_km, n_blk, token_minor):
    x = x_ref[...]
    tm = x.shape[0]
    r = lax.rsqrt(jnp.mean(x * x, axis=-1, keepdims=True) + RMS_EPS)
    xb = ((x * r) * gna_ref[...]).astype(BF16)
    bd = bd_ref[...]
    bd1 = bd_ref[:LANES, :LANES]

    def sec(a, b):
        return _dot(xb, w1_ref[:, a:b]) + b1_ref[:, a:b]

    def put_heads(ref, comp, first_head, tile):
        if token_minor:
            ref[0, comp, first_head:first_head + 2] = tile.T.reshape(2, HEAD_DIM, tm)
        else:
            for h in range(2):
                ref[:, comp, first_head + h, :] = tile[:, h * HEAD_DIM:(h + 1) * HEAD_DIM]

    qa = _head_norm(sec(C_QA, C_KA), gv_ref[:, C_QA:C_KA], bd)
    qa_ref[...] = (qa * QK_SCALE).astype(BF16)

    ka = _head_norm(sec(C_KA, C_VA), gv_ref[:, C_KA:C_VA], bd)
    va = sec(C_VA, C_QB)
    for j in range(H_A // 2):
        put_heads(kva_ref, 0, 2 * j, ka[:, j * LANES:(j + 1) * LANES])
        put_heads(kva_ref, 1, 2 * j, va[:, j * LANES:(j + 1) * LANES])
    kab_ref[...] = ka.astype(BF16)
    vab_ref[...] = va.astype(BF16)
    if n_blk:
        km_ref = maybe_km[0]
        km_ref[0] = jnp.mean(ka.reshape(n_blk, MOBA_BLOCK, W_A), axis=1)

    qb = _head_norm(sec(C_QB, C_KVB), gv_ref[:, C_QB:C_KVB], bd)
    qb_ref[...] = (qb * QK_SCALE).astype(BF16)

    kvb = sec(C_KVB, C_END)
    ks = _head_norm(kvb[:, 256:384], gv_ref[:, C_KVB + 256:C_KVB + 384], bd1)
    vs = kvb[:, 384:512]
    kw = _head_norm(kvb[:, 512:640], gv_ref[:, C_KVB + 512:C_KVB + 640], bd1)
    vw = kvb[:, 640:768]
    cmpk_ref[...] = kvb[:, 0:128]
    cmpv_ref[...] = kvb[:, 128:256]
    for comp, tile in enumerate((kvb[:, 0:128], kvb[:, 128:256], ks, vs)):
        put_heads(kvn_ref, comp, 0, tile)
    for comp, tile in enumerate((kw, vw)):
        put_heads(kvw_ref, comp, 0, tile)
    ksb_ref[...] = ks.astype(BF16)
    vsb_ref[...] = vs.astype(BF16)
    kwb_ref[...] = kw.astype(BF16)
    vwb_ref[...] = vw.astype(BF16)

    gb_ref[...] = jax.nn.sigmoid(_dot(xb, wg_ref[...]) + bg_ref[...])
    gm_ref[...] = jax.nn.sigmoid(_dot(xb, wgm_ref[...]) + bgm_ref[...])


def _inproj(x, wts, seq_len=None):
    n = x.shape[0]
    token_minor = seq_len is not None
    tm = MOBA_BLOCK if token_minor else n
    n_blk = tm // MOBA_BLOCK if token_minor else 0
    grid = (n // tm,)
    if token_minor:
        assert seq_len % tm == 0
        tiles = seq_len // tm

    def rows(w):
        if isinstance(w, tuple) and token_minor:
            return pl.BlockSpec((1,) + w + (tm,), lambda i: (i // tiles, 0, 0, 0, i % tiles))
        if isinstance(w, tuple):
            return pl.BlockSpec((tm,) + w, lambda i: (i,) + (0,) * len(w))
        return pl.BlockSpec((tm, w), lambda i: (i, 0))

    def kv_shape(w):
        return (n // seq_len,) + w + (seq_len,) if token_minor else (n,) + w

    def full(a):
        return pl.BlockSpec(a.shape, lambda i: (0,) * a.ndim)

    ins = [x, wts["gna"], wts["w1"], wts["b1"], wts["gv"], wts["wg"], wts["bg"], wts["wgm"], wts["bgm"], wts["bd"]]
    in_specs = [rows(D_MODEL)] + [full(a) for a in ins[1:]]
    outs = [
        (W_A, BF16), ((2, H_A, HEAD_DIM), F32), (W_A, BF16), (W_A, BF16), (W_B, BF16), ((4, H_KV, HEAD_DIM), F32),
        (W_KV, F32), (W_KV, F32), (W_KV, BF16), (W_KV, BF16), ((2, H_KV, HEAD_DIM), F32), (W_KV, BF16), (W_KV, BF16),
        (LANES, F32), (2 * D_MODEL, F32),
    ]
    out_shape = [jax.ShapeDtypeStruct(kv_shape(w) if isinstance(w, tuple) else (n, w), dt) for w, dt in outs]
    out_specs = [rows(w) for w, _ in outs]
    if n_blk:
        out_shape.append(jax.ShapeDtypeStruct((n // tm, n_blk, W_A), F32))
        out_specs.append(pl.BlockSpec((1, n_blk, W_A), lambda i: (i, 0, 0)))
    res = pl.pallas_call(
        functools.partial(_inproj_kernel, n_blk=n_blk, token_minor=token_minor),
        grid=grid, in_specs=in_specs, out_specs=out_specs, out_shape=out_shape,
        compiler_params=_cparams("parallel"),
    )(*ins)
    names = ["qa", "kva", "kab", "vab", "qb", "kvn", "cmpk", "cmpv", "ksb", "vsb", "kvw", "kwb", "vwb", "gb", "gm"]
    out = dict(zip(names, res))
    if n_blk:
        out["kmean"] = res[-1].reshape(n // MOBA_BLOCK, W_A)
    return out


def _compress_math(load_main, load_halo, pos_ref, wa_ref, wb_ref, w2_ref, gk_ref, bd_ref, kc_ref, vc_ref, tc):
    bd1 = bd_ref[:LANES, :LANES]

    def half_proj(load, w_ref, half):
        outs = []
        for br in range(2):
            acc = None
            for l in range(CMP_STRIDE):
                xl = (load(br, l) + pos_ref[half, br, l]).astype(BF16)
                t = _dot(xl, w_ref[br, l])
                acc = t if acc is None else acc + t
            outs.append(acc)
        return outs

    a_k, a_v = half_proj(load_main, wa_ref, 0)
    b_k, b_v = half_proj(load_main, wb_ref, 1)
    bh_k, bh_v = half_proj(load_halo, wb_ref, 1)
    rows = lax.broadcasted_iota(jnp.int32, (tc, 2 * CMP_HID), 0)

    def shift_up(b, bh):
        rolled = pltpu.roll(b, tc - 1, 0)
        return jnp.where(rows == tc - 1, jnp.broadcast_to(bh[0:1], b.shape), rolled)

    hid_k = a_k + shift_up(b_k, bh_k)
    hid_v = a_v + shift_up(b_v, bh_v)
    hid_k = hid_k * jax.nn.sigmoid(hid_k)
    hid_v = hid_v * jax.nn.sigmoid(hid_v)
    out_k = _dot(hid_k.astype(BF16), w2_ref[0])
    out_v = _dot(hid_v.astype(BF16), w2_ref[1])
    kc_ref[0] = _head_norm(out_k, gk_ref[...], bd1).astype(BF16)
    vc_ref[0] = out_v.astype(BF16)


def _compress_kernel(xk_ref, xv_ref, hk_ref, hv_ref, *rest, tc):
    chunk_rows = lambda ref, n: (lambda l: ref[0, pl.ds(l, n, stride=CMP_STRIDE), :])
    main = (chunk_rows(xk_ref, tc), chunk_rows(xv_ref, tc))
    halo = (chunk_rows(hk_ref, 8), chunk_rows(hv_ref, 8))
    _compress_math(lambda br, l: main[br](l), lambda br, l: halo[br](l), *rest, tc)


CMP_PAGES = (32, 16)


def _compress_paged_kernel(pt_ref, *refs, g_pages):
    del pt_ref
    page_refs = refs[:g_pages + 1]
    rest, (xk_sc, xv_sc) = refs[g_pages + 1:-2], refs[-2:]
    for g, page in enumerate(page_refs):
        rows = slice(g * PAGE_SIZE, (g + 1) * PAGE_SIZE)
        xk_sc[rows, :] = page[0, 0].reshape(W_KV, PAGE_SIZE).T
        xv_sc[rows, :] = page[0, 1].reshape(W_KV, PAGE_SIZE).T
    tc = g_pages * (PAGE_SIZE // CMP_STRIDE)
    sc = (xk_sc, xv_sc)
    _compress_math(lambda br, l: sc[br][pl.ds(l, tc, stride=CMP_STRIDE), :],
                   lambda br, l: sc[br][pl.ds(tc * CMP_STRIDE + l, 8, stride=CMP_STRIDE), :], *rest, tc)


def _compress_paged(page_table, nsa_t, wts):
    s, n_pages = page_table.shape
    g_pages = next(g for g in CMP_PAGES if n_pages % g == 0)
    tc = g_pages * (PAGE_SIZE // CMP_STRIDE)
    nc = n_pages * (PAGE_SIZE // CMP_STRIDE)
    page = lambda k: pl.BlockSpec(
        (1, 2) + nsa_t.shape[2:], lambda b, i, pt: (pt[b, jnp.minimum(i * g_pages + k, n_pages - 1)], 0, 0, 0, 0))
    full = lambda a: pl.BlockSpec(a.shape, lambda b, i, pt: (0,) * a.ndim)
    consts = [wts["cpos"], wts["cwa"], wts["cwb"], wts["cw2"], wts["gkc"], wts["bd"]]
    rows_sc = pltpu.VMEM(((g_pages + 1) * PAGE_SIZE, LANES), F32)
    return pl.pallas_call(
        functools.partial(_compress_paged_kernel, g_pages=g_pages),
        grid_spec=pltpu.PrefetchScalarGridSpec(
            num_scalar_prefetch=1, grid=(s, n_pages // g_pages),
            in_specs=[page(k) for k in range(g_pages + 1)] + [full(a) for a in consts],
            out_specs=[pl.BlockSpec((1, tc, LANES), lambda b, i, pt: (b, i, 0))] * 2,
            scratch_shapes=[rows_sc, rows_sc]),
        out_shape=[jax.ShapeDtypeStruct((s, nc, LANES), BF16)] * 2,
        compiler_params=_cparams("parallel", "arbitrary"),
    )(page_table, *([nsa_t] * (g_pages + 1)), *consts)


def _compress(rows_k, rows_v, wts, tc):
    b, length, _ = rows_k.shape
    nc = length // CMP_STRIDE
    nt = nc // tc
    last_h = nc // 8 - 1
    full = lambda a: pl.BlockSpec(a.shape, lambda i, j: (0,) * a.ndim)
    main = pl.BlockSpec((1, tc * CMP_STRIDE, LANES), lambda i, j: (i, j, 0))
    halo = pl.BlockSpec((1, 8 * CMP_STRIDE, LANES), lambda i, j: (i, jnp.minimum((j + 1) * (tc // 8), last_h), 0))
    ins = [rows_k, rows_v, rows_k, rows_v, wts["cpos"], wts["cwa"], wts["cwb"], wts["cw2"], wts["gkc"], wts["bd"]]
    in_specs = [main, main, halo, halo] + [full(a) for a in ins[4:]]
    out_shape = [jax.ShapeDtypeStruct((b, nc, LANES), BF16)] * 2
    out_specs = [pl.BlockSpec((1, tc, LANES), lambda i, j: (i, j, 0))] * 2
    return pl.pallas_call(
        functools.partial(_compress_kernel, tc=tc),
        grid=(b, nt), in_specs=in_specs, out_specs=out_specs, out_shape=out_shape,
        compiler_params=_cparams("parallel", "arbitrary"),
    )(*ins)


def _flash_init(m_ref, l_ref, acc_ref):
    m_ref[...] = jnp.full(m_ref.shape, NEG_INF, F32)
    l_ref[...] = jnp.zeros(l_ref.shape, F32)
    acc_ref[...] = jnp.zeros(acc_ref.shape, F32)


def _flash_step(s, valid, v, m_ref, l_ref, acc_ref, i):
    s = jnp.where(valid, s, NEG_INF)
    m_old = m_ref[i]
    m_new = jnp.maximum(m_old, jnp.max(s, axis=1, keepdims=True))
    m_safe = jnp.where(m_new == NEG_INF, 0.0, m_new)
    alpha = jnp.exp(m_old - m_safe)
    p = jnp.exp(s - m_safe[:, :1])
    l_ref[i] = alpha * l_ref[i] + jnp.sum(p, axis=1, keepdims=True)
    acc_ref[i] = alpha * acc_ref[i] + _dot(p.astype(BF16), v)
    m_ref[i] = m_new


def _flash_out(l_ref, acc_ref, i):
    l = l_ref[i]
    return acc_ref[i] / jnp.where(l > 0, l, 1.0)


def _topk_mask(v, idx_iota, k):
    width = v.shape[1]

    def body(_, carry):
        v, sel = carry
        mx = jnp.max(v, axis=1, keepdims=True)
        idx = jnp.min(jnp.where(v == mx, idx_iota, width), axis=1, keepdims=True)
        hit = idx_iota == idx
        sel = jnp.where(hit & (mx > NEG_INF), 1.0, sel)
        return jnp.where(hit, NEG_INF, v), sel

    return lax.fori_loop(0, k, body, (v, jnp.zeros(v.shape, F32)))[1]


def _cmp_to_block_matrix(ncp, nsp, n_cmp, n_s):
    m = np.zeros((ncp, nsp), np.float32)
    per = SLC_BLOCK // CMP_STRIDE
    for i in range(n_s * per):
        for t in (i, i - 1):
            if 0 <= t < n_cmp:
                m[t, i // per] += 1.0
    return jnp.asarray(m)


MASK_BIG = 2.0 ** 30
AUX_R = LANES - 1


def _key_aux_table(t, block):
    assert t // block <= AUX_R and block <= 256
    key = np.arange(t)
    tab = np.zeros((t, LANES), np.float32)
    tab[key, key // block] = 1.0
    tab[:, AUX_R] = key % block
    return jnp.asarray(tab, BF16)


KEY_CHUNK = 64
PAST_UNROLL = 3


def _flash_t_step(s_ref, p_ref, mask, v_t, m_ref, l_ref, acc_ref, i):
    n_keys, n_q = s_ref.shape
    chunks = [slice(r, r + KEY_CHUNK) for r in range(0, n_keys, KEY_CHUNK)]

    def scores(rows):
        s = s_ref[rows, :]
        if mask is not None:
            s = jnp.where(mask[rows, :], s, -MASK_BIG)
        return s.reshape(KEY_CHUNK // 8, 8, n_q)

    m_old = m_ref[i]
    m8 = None
    for rows in chunks:
        cm = jnp.max(scores(rows), axis=0)
        m8 = cm if m8 is None else jnp.maximum(m8, cm)
    m_new = jnp.maximum(m_old, jnp.max(m8, axis=0, keepdims=True))
    alpha = jnp.exp(m_old - m_new)
    l8 = None
    for rows in chunks:
        p = jnp.exp(scores(rows) - m_new[None])
        ps = jnp.sum(p, axis=0)
        l8 = ps if l8 is None else l8 + ps
        p_ref[rows, :] = p.reshape(KEY_CHUNK, n_q).astype(BF16)
    l_ref[i] = alpha * l_ref[i] + l8
    acc_ref[i] = alpha[0:1] * acc_ref[i] + _dot(v_t, p_ref[...])
    m_ref[i] = m_new


def _flash_t_out(l_ref, acc_ref, i):
    l = jnp.sum(l_ref[i], axis=0, keepdims=True)
    return acc_ref[i] / jnp.where(l > 0, l, 1.0)


def _topk_mask_t(v, idx, k):
    n = v.shape[0]

    def body(_, carry):
        v, sel = carry
        mx = jnp.max(v, axis=0, keepdims=True)
        first = jnp.min(jnp.where(v == mx, idx, n), axis=0, keepdims=True)
        hit = idx == first
        sel = jnp.where(hit & (mx > NEG_INF), 1.0, sel)
        return jnp.where(hit, NEG_INF, v), sel

    return lax.fori_loop(0, k, body, (v, jnp.zeros(v.shape, F32)))[1]


def _attn_prompt_kernel(qa_ref, qb_ref, gb_ref, km_ref, ka_ref, va_ref, kc_ref, vc_ref, ks_ref, vs_ref,
                        kw_ref, vw_ref, auxa_ref, auxs_ref, mm_ref, oa_ref, ob_ref,
                        m_sc, l_sc, acc_sc, oc_sc, os_sc, qx_sc, s_sc, p_sc, *, n_cmp, nb, n_s, k_a, k_s):
    c = pl.program_id(1)
    tq = MOBA_BLOCK
    sub = lax.broadcasted_iota(jnp.int32, (LANES, tq), 0)
    is_lo = sub < HEAD_DIM
    is_r = sub == AUX_R
    key_r = lax.broadcasted_iota(jnp.int32, (tq, tq), 0)
    qry_r = lax.broadcasted_iota(jnp.int32, (tq, tq), 1)
    causal = key_r <= qry_r

    def masked_q(ref, tile, half):
        q = ref[0, tile * LANES:(tile + 1) * LANES, :]
        keep = is_lo if half == 0 else jnp.logical_not(is_lo)
        return jnp.where(keep, q, jnp.zeros_like(q))

    def set_query(i, q_t, aux):
        qx_sc[i, :LANES, :] = q_t
        qx_sc[i, LANES:, :] = aux.astype(BF16)

    def topk_rows(v, k, n_rows):
        r = _round_up(n_rows, 8)
        tmp, out = s_sc.at[0], s_sc.at[1]
        tmp[:LANES, :] = v
        out[:LANES, :] = jnp.zeros((LANES, tq), F32)
        out[:r, :] = _topk_mask_t(tmp[:r, :], lax.broadcasted_iota(jnp.int32, (r, tq), 0), k)
        return out[:LANES, :]

    def flash_init():
        m_sc[...] = jnp.full(m_sc.shape, NEG_INF, F32)
        l_sc[...] = jnp.zeros(l_sc.shape, F32)
        acc_sc[...] = jnp.zeros(acc_sc.shape, F32)

    def sweep(k_ref, k_lanes, aux_ref, v_ref, v_rows, heads, blocks, mask):
        masks = mask if isinstance(mask, list) else [mask] * len(blocks)
        units = [(n, i, m) for n, m in zip(blocks, masks) for i in heads]
        n_buf = s_sc.shape[0]

        def stage_scores(pos):
            n, i, _ = units[pos]
            koff = pl.multiple_of(n * tq, tq)
            kcat = jnp.concatenate([k_ref[0, pl.ds(koff, tq), k_lanes(i)], aux_ref[pl.ds(koff, tq), :]], axis=1)
            s_sc[pos % n_buf] = _dot(kcat, qx_sc[i])

        for pos in range(min(n_buf - 1, len(units))):
            stage_scores(pos)
        for pos, (n, i, unit_mask) in enumerate(units):
            if pos + n_buf - 1 < len(units):
                stage_scores(pos + n_buf - 1)
            _flash_t_step(s_sc.at[pos % n_buf], p_sc.at[pos % 2], unit_mask, v_ref[0, n, v_rows(i), :],
                          m_sc, l_sc, acc_sc, i)

    def past_loop(blocks_fn):
        u = PAST_UNROLL

        def body(nu, carry):
            blocks_fn([u * nu + k for k in range(u)])
            return carry
        lax.fori_loop(0, c // u, body, 0)
        for rem in range(1, u):
            pl.when(c % u == rem)(lambda rem=rem: blocks_fn([c - rem + k for k in range(rem)]))

    blk = sub
    back = (c - blk).astype(F32)
    for h in range(H_A):
        j, half = divmod(h, 2)
        q_t = masked_q(qa_ref, j, half)
        gate = _dot(km_ref[0, :, j * LANES:(j + 1) * LANES].astype(BF16), q_t)
        sel = topk_rows(jnp.where(blk < c, gate, NEG_INF), k_a, nb)
        bias = jnp.where(sel > 0.5, 0.0, -MASK_BIG) - (SLOPES_A[h] * tq) * back
        set_query(h, q_t, jnp.where(is_r, SLOPES_A[h], jnp.where(blk < c, bias, 0.0)))

    pair_lanes = lambda i: slice((i // 2) * LANES, (i // 2 + 1) * LANES)
    all_lanes = lambda i: slice(0, LANES)
    heads = range(H_A)
    flash_init()
    sweep(ka_ref, pair_lanes, auxa_ref, va_ref, pair_lanes, heads, [c], causal)
    past_loop(lambda blocks: sweep(ka_ref, pair_lanes, auxa_ref, va_ref, pair_lanes, heads, blocks, None))
    for j in range(H_A // 2):
        o = jnp.where(is_lo, _flash_t_out(l_sc, acc_sc, 2 * j), _flash_t_out(l_sc, acc_sc, 2 * j + 1))
        oa_ref[0, j * LANES:(j + 1) * LANES, :] = o.astype(BF16)

    ncp = kc_ref.shape[1]
    nsp = mm_ref.shape[0]
    jc = lax.broadcasted_iota(jnp.int32, (ncp, tq), 0)
    d_c = c * tq + lax.broadcasted_iota(jnp.int32, (ncp, tq), 1) - (jc * CMP_STRIDE + (CMP_LEN - 1))
    c_valid = (d_c >= 0) & (jc < n_cmp)
    d_cf = d_c.astype(F32)
    sblk = lax.broadcasted_iota(jnp.int32, (nsp, tq), 0)
    cur_s = (c * tq + lax.broadcasted_iota(jnp.int32, (nsp, tq), 1)) // SLC_BLOCK
    forced = (sblk == 0) | (sblk >= cur_s - 1)
    causal_s = sblk <= cur_s
    rel_s = (sblk - c * (tq // SLC_BLOCK)).astype(F32)
    for k in range(H_KV):
        imp = jnp.zeros((ncp, tq), F32)
        for g in range(GROUP):
            i = k * GROUP + g
            s = _dot(kc_ref[0], masked_q(qb_ref, g, k)) - SLOPES_B[i] * d_cf
            s = jnp.where(c_valid, s, NEG_INF)
            m = jnp.max(s, axis=0, keepdims=True)
            m = jnp.where(m == NEG_INF, 0.0, m)
            e = jnp.where(c_valid, jnp.exp(s - m), 0.0)
            d = jnp.sum(e, axis=0, keepdims=True)
            p = e / jnp.where(d > 0, d, 1.0)
            imp = imp + p
            oc_sc[i] = _dot(vc_ref[0], p.astype(BF16))
        impb = jnp.dot(mm_ref[...], imp, preferred_element_type=F32, precision=lax.Precision.HIGHEST)
        impb = jnp.where(forced, impb + FORCE_BONUS, impb)
        impb = jnp.where(causal_s, impb, NEG_INF)
        sel = topk_rows(impb, k_s, n_s)
        for g in range(GROUP):
            i = k * GROUP + g
            bias = jnp.where(sel > 0.5, 0.0, -MASK_BIG) + (SLOPES_B[i] * SLC_BLOCK) * rel_s
            set_query(i, masked_q(qb_ref, g, k), jnp.where(is_r, SLOPES_B[i], bias))

    heads = range(H_B)
    flash_init()
    sweep(ks_ref, all_lanes, auxs_ref, vs_ref, all_lanes, heads, [c], causal)
    past_loop(lambda blocks: sweep(ks_ref, all_lanes, auxs_ref, vs_ref, all_lanes, heads, blocks, None))
    for i in range(H_B):
        os_sc[i] = _flash_t_out(l_sc, acc_sc, i)

    for i in range(H_B):
        qx_sc[i, LANES:, :] = jnp.where(is_r, SLOPES_B[i], -(SLOPES_B[i] * tq) * back).astype(BF16)
    flash_init()
    n_back = WINDOW // tq
    back_masks = [causal] + [None] * (n_back - 1) + [key_r > qry_r]
    for reach in range(n_back + 1):
        pl.when(jnp.minimum(c, n_back) == reach)(functools.partial(
            sweep, kw_ref, all_lanes, auxa_ref, vw_ref, all_lanes, heads,
            [c - b for b in range(reach + 1)], back_masks[:reach + 1]))

    gb = gb_ref[0]
    for j in range(GROUP):
        def comb(i):
            return (gb[3 * i:3 * i + 1] * oc_sc[i] + gb[3 * i + 1:3 * i + 2] * os_sc[i]
                    + gb[3 * i + 2:3 * i + 3] * _flash_t_out(l_sc, acc_sc, i))
        ob_ref[0, j * LANES:(j + 1) * LANES, :] = jnp.where(is_lo, comb(j), comb(GROUP + j)).astype(BF16)


def _attn_prompt(p, kc, vc, b, t):
    tq = MOBA_BLOCK
    nb = t // tq
    n_s = t // SLC_BLOCK
    n_cmp = t // CMP_STRIDE - 1
    nsp = _round_up(n_s, LANES)
    ncp = kc.shape[1]
    assert nb <= AUX_R and n_s <= AUX_R and nsp == LANES and WINDOW % tq == 0
    km = jnp.pad(p["kmean"].reshape(b, nb, W_A), ((0, 0), (0, LANES - nb), (0, 0)))
    mm = _cmp_to_block_matrix(ncp, nsp, n_cmp, n_s).T
    r3 = lambda a: a.reshape(b, t, a.shape[-1])
    tr = lambda a: r3(a).transpose(0, 2, 1)
    qtile = lambda w: pl.BlockSpec((1, w, tq), lambda i, c: (i, 0, c))
    trb = lambda a: a.reshape(b, nb, tq, a.shape[-1]).transpose(0, 1, 3, 2)
    seq = lambda a: pl.BlockSpec((1,) + a.shape[1:], lambda i, c: (i,) + (0,) * (a.ndim - 1))
    const = lambda a: pl.BlockSpec(a.shape, lambda i, c: (0, 0))
    ins = [tr(p["qa"]), tr(p["qb"]), tr(p["gb"]), km, r3(p["kab"]), trb(p["vab"]), kc, vc.transpose(0, 2, 1),
           r3(p["ksb"]), trb(p["vsb"]), r3(p["kwb"]), trb(p["vwb"]),
           _key_aux_table(t, MOBA_BLOCK), _key_aux_table(t, SLC_BLOCK), mm]
    in_specs = [qtile(W_A), qtile(W_B), qtile(LANES)] + [seq(a) for a in ins[3:12]] + [const(a) for a in ins[12:]]
    out_shape = [jax.ShapeDtypeStruct((b, W_A, t), BF16), jax.ShapeDtypeStruct((b, W_B, t), BF16)]
    out_specs = [qtile(W_A), qtile(W_B)]
    scratch = [
        pltpu.VMEM((H_A, 8, tq), F32), pltpu.VMEM((H_A, 8, tq), F32), pltpu.VMEM((H_A, LANES, tq), F32),
        pltpu.VMEM((H_B, LANES, tq), F32), pltpu.VMEM((H_B, LANES, tq), F32), pltpu.VMEM((H_A, 2 * LANES, tq), BF16),
        pltpu.VMEM((8, tq, tq), F32), pltpu.VMEM((2, tq, tq), BF16),
    ]
    oa, ob = pl.pallas_call(
        functools.partial(_attn_prompt_kernel, n_cmp=n_cmp, nb=nb, n_s=n_s,
                          k_a=min(MOBA_TOPK, nb), k_s=min(SLC_TOPK, n_s)),
        grid=(b, nb), in_specs=in_specs, out_specs=out_specs, out_shape=out_shape, scratch_shapes=scratch,
        compiler_params=_cparams("parallel", "arbitrary"),
    )(*ins)
    return oa.transpose(0, 2, 1).reshape(b * t, W_A), ob.transpose(0, 2, 1).reshape(b * t, W_B)


MOBA_SAMPLE_PAGES = 8


def _moba_sample_tables(t_new):
    rows = H_A * t_new
    head = np.arange(rows) // t_new
    qidx = np.arange(rows) % t_new
    slope = np.asarray(SLOPES_A, np.float32)[head]
    b0 = (slope[:, None] * np.arange(PAGE_SIZE)[None, :]).astype(np.float32)
    coln = np.arange(t_new * H_A)
    ok = (head[:, None] == (coln % H_A)[None, :]) & ((coln // H_A)[None, :] <= qidx[:, None])
    bnew = np.where(ok, slope[:, None] * (coln // H_A)[None, :], -MASK_BIG).astype(np.float32)
    own = (head[:, None] == (np.arange(W_A) // HEAD_DIM)[None, :]).astype(np.float32)
    slope_l = np.broadcast_to(slope[:, None], (rows, LANES)).astype(np.float32)
    return jnp.asarray(b0), jnp.asarray(bnew), jnp.asarray(own), jnp.asarray(slope_l)


def _moba_sample_kernel(pt_ref, qbd_ref, q_ref, b0_ref, bnew_ref, own_ref, slope_ref, new_ref, *rest, n_pages, k_a):
    g_pages = MOBA_SAMPLE_PAGES
    page_refs, o_ref = rest[:g_pages], rest[g_pages]
    g_sc, m_sc, l_sc, o_sc = rest[g_pages + 1:]
    del pt_ref
    step = pl.program_id(1)
    qbd = qbd_ref[0]
    rows = qbd.shape[0]
    slope = slope_ref[:, 0:1]

    def softmax_partial(s):
        m = jnp.max(s, axis=1, keepdims=True)
        e = jnp.exp(s - m)
        return m, jnp.sum(e, axis=1, keepdims=True), e.astype(BF16)

    k_all = jnp.concatenate([pg[0, 0].reshape(W_A, PAGE_SIZE).astype(BF16) for pg in page_refs], axis=1)
    qk_all = _dot(qbd, k_all)
    for g in range(g_pages):
        p = step * g_pages + g
        v_t = page_refs[g][0, 1].reshape(W_A, PAGE_SIZE).astype(BF16)
        qk = qk_all[:, g * PAGE_SIZE:(g + 1) * PAGE_SIZE]
        g_sc[p] = jnp.broadcast_to(jnp.sum(qk, axis=1, keepdims=True), (rows, LANES))
        m, l, e = softmax_partial(qk + b0_ref[...] + slope * ((p - n_pages) * PAGE_SIZE).astype(F32))
        m_sc[p] = jnp.broadcast_to(m, (rows, LANES))
        l_sc[p] = jnp.broadcast_to(l, (rows, LANES))
        o_sc[p] = _dot_t(e, v_t)

    @pl.when(step == pl.num_programs(1) - 1)
    def _():
        per_blk = MOBA_BLOCK // PAGE_SIZE
        nb = n_pages // per_blk
        gate = jnp.sum(g_sc[...].reshape(nb, per_blk, rows, LANES), axis=1)
        sel = _topk_mask_t(gate, lax.broadcasted_iota(jnp.int32, gate.shape, 0), k_a)
        sel = jnp.broadcast_to(sel[:, None], (nb, per_blk, rows, LANES)).reshape(n_pages, rows, LANES) > 0.5
        t_new = new_ref.shape[1]
        k_new = new_ref[0, :, 0].reshape(t_new * H_A, HEAD_DIM).astype(BF16)
        v_new = new_ref[0, :, 1].reshape(t_new * H_A, HEAD_DIM).astype(BF16)
        m_o, l_o, e_o = softmax_partial(_dot_t(q_ref[0], k_new) + bnew_ref[...])
        o_o = _dot(e_o, v_new)
        m_all = m_sc[...]
        m_star = jnp.maximum(jnp.max(jnp.where(sel, m_all, -MASK_BIG), axis=0), m_o)
        w = jnp.where(sel, jnp.exp(m_all - m_star[None]), 0.0)
        w_o = jnp.exp(m_o - m_star)
        l_star = jnp.sum(w * l_sc[...], axis=0) + w_o * l_o
        n_rep = W_A // LANES
        w_wide = jnp.concatenate([w] * n_rep, axis=2)
        o_star = jnp.sum(w_wide * o_sc[...], axis=0) + (
            jnp.concatenate([w_o[:, :HEAD_DIM]] * H_A, axis=1) * jnp.concatenate([o_o] * H_A, axis=1))
        o_ref[0] = (o_star / jnp.concatenate([l_star] * n_rep, axis=1)) * own_ref[...]


def _moba_sample(p, cache_moba, page_table, s, t_new):
    n_pages = page_table.shape[1]
    g_pages = MOBA_SAMPLE_PAGES
    assert n_pages % g_pages == 0 and n_pages % (MOBA_BLOCK // PAGE_SIZE) == 0
    rows = H_A * t_new
    nb = n_pages // (MOBA_BLOCK // PAGE_SIZE)
    cache_t = cache_moba.transpose(0, 2, 3, 4, 1)
    q4 = p["qa"].reshape(s, t_new, H_A, HEAD_DIM).transpose(0, 2, 1, 3)
    q = q4.reshape(s, rows, HEAD_DIM)
    eye = jnp.eye(H_A, dtype=q4.dtype)
    qbd = (q4[:, :, :, None, :] * eye[None, :, None, :, None]).reshape(s, rows, W_A)
    new = p["kva"].reshape(s, t_new, 2, H_A, HEAD_DIM)
    tables = _moba_sample_tables(t_new)
    const = lambda a: pl.BlockSpec(a.shape, lambda b, i, pt: (0, 0))
    seq = lambda a: pl.BlockSpec((1,) + a.shape[1:], lambda b, i, pt: (b,) + (0,) * (a.ndim - 1))
    page = lambda g: pl.BlockSpec((1,) + cache_t.shape[1:], lambda b, i, pt: (pt[b, i * g_pages + g], 0, 0, 0, 0))
    out = pl.pallas_call(
        functools.partial(_moba_sample_kernel, n_pages=n_pages, k_a=min(MOBA_TOPK, nb)),
        grid_spec=pltpu.PrefetchScalarGridSpec(
            num_scalar_prefetch=1, grid=(s, n_pages // g_pages),
            in_specs=[seq(qbd), seq(q)] + [const(a) for a in tables] + [seq(new)] + [page(g) for g in range(g_pages)],
            out_specs=pl.BlockSpec((1, rows, W_A), lambda b, i, pt: (b, 0, 0)),
            scratch_shapes=[pltpu.VMEM((n_pages, rows, LANES), F32)] * 3 + [pltpu.VMEM((n_pages, rows, W_A), F32)]),
        out_shape=jax.ShapeDtypeStruct((s, rows, W_A), F32),
        compiler_params=_cparams("parallel", "arbitrary"),
    )(page_table, qbd, q, *tables, new, *([cache_t] * g_pages))
    return out.reshape(s, H_A, t_new, W_A).sum(axis=1).reshape(s * t_new, W_A).astype(BF16)


SAMPLE_KV_TILES = (4096, 2048)


def _attn_sample_kernel(pt_ref, qb16_ref, g16_ref, kc_ref, vc_ref, mm_ref, ksn_ref, vsn_ref, kwa_ref, vwa_ref, *rest,
                        past, t_new, k_s, r_tile):
    del pt_ref
    g_pages = r_tile // PAGE_SIZE
    page_refs = rest[:g_pages]
    ob_ref, mb_sc, lb_sc, accb_sc, sels_sc, oc_sc, os_sc = rest[g_pages:]
    kt = pl.program_id(1)
    rb = GROUP * t_new
    row_b = lax.broadcasted_iota(jnp.int32, (rb, 1), 0)
    t_b = past + row_b % t_new
    slope_b = []
    for k in range(H_KV):
        sl = jnp.full((rb, 1), SLOPES_B[k * GROUP + GROUP - 1], F32)
        for g in range(GROUP - 2, -1, -1):
            sl = jnp.where(row_b < (g + 1) * t_new, SLOPES_B[k * GROUP + g], sl)
        slope_b.append(sl)
    nsp = mm_ref.shape[1]
    cur_s = past // SLC_BLOCK

    @pl.when(kt == 0)
    def _():
        _flash_init(mb_sc, lb_sc, accb_sc)
        nc = kc_ref.shape[1]
        jc = lax.broadcasted_iota(jnp.int32, (rb, nc), 1)
        d_c = t_b - (jc * CMP_STRIDE + (CMP_LEN - 1))
        c_valid = d_c >= 0
        d_cf = d_c.astype(F32)
        sj = lax.broadcasted_iota(jnp.int32, (rb, nsp), 1)
        forced = (sj == 0) | (sj >= cur_s - 1)
        causal_s = sj <= cur_s
        gr = lax.broadcasted_iota(jnp.int32, (rb, rb), 0) % t_new
        gc = lax.broadcasted_iota(jnp.int32, (rb, rb), 1) % t_new
        group_sum = (gr == gc).astype(F32)
        for k in range(H_KV):
            s = _dot_t(qb16_ref[0, k], kc_ref[0]) - slope_b[k] * d_cf
            s = jnp.where(c_valid, s, NEG_INF)
            m = jnp.max(s, axis=1, keepdims=True)
            m = jnp.where(m == NEG_INF, 0.0, m)
            e = jnp.where(c_valid, jnp.exp(s - m), 0.0)
            d = jnp.sum(e, axis=1, keepdims=True)
            p = e / jnp.where(d > 0, d, 1.0)
            oc_sc[k] = _dot(p.astype(BF16), vc_ref[0])
            pb = jnp.dot(p, mm_ref[...], preferred_element_type=F32, precision=lax.Precision.HIGHEST)
            impb = jnp.dot(group_sum, pb, preferred_element_type=F32, precision=lax.Precision.HIGHEST)
            impb = jnp.where(forced, impb + FORCE_BONUS, impb)
            impb = jnp.where(causal_s, impb, NEG_INF)
            sels_sc[k] = _topk_mask(impb, sj, k_s)

    blocks_per_tile = r_tile // SLC_BLOCK
    first_blk = kt * blocks_per_tile
    lane_tile = pl.multiple_of((first_blk // LANES) * LANES, LANES)
    e_row = lax.broadcasted_iota(jnp.int32, (LANES, r_tile), 0)
    e_col = lax.broadcasted_iota(jnp.int32, (LANES, r_tile), 1) // SLC_BLOCK
    expand = (e_row == first_blk % LANES + e_col).astype(BF16)
    col_t = lax.broadcasted_iota(jnp.int32, (rb, r_tile), 1)
    dist_t = (t_b - (kt * r_tile + col_t)).astype(F32)
    k_all = jnp.concatenate([pg[0, 0].reshape(W_KV, PAGE_SIZE).astype(BF16) for pg in page_refs], axis=1)
    v_all = jnp.concatenate([pg[0, 1].reshape(W_KV, PAGE_SIZE).astype(BF16) for pg in page_refs], axis=1)
    for k in range(H_KV):
        sel_t = sels_sc[k, :, pl.ds(lane_tile, LANES)]
        valid = _dot(sel_t.astype(BF16), expand) > 0.5
        s = _dot(qb16_ref[0, k], k_all) - slope_b[k] * dist_t
        s = jnp.where(valid, s, NEG_INF)
        m_old = mb_sc[k]
        m_new = jnp.maximum(m_old, jnp.max(s, axis=1, keepdims=True))
        m_safe = jnp.where(m_new == NEG_INF, 0.0, m_new)
        alpha = jnp.exp(m_old - m_safe)
        p = jnp.exp(s - m_safe[:, :1])
        lb_sc[k] = alpha * lb_sc[k] + jnp.sum(p, axis=1, keepdims=True)
        accb_sc[k] = alpha * accb_sc[k] + _dot_t(p.astype(BF16), v_all)
        mb_sc[k] = m_new

    @pl.when(kt == pl.num_programs(1) - 1)
    def _():
        dist_b = row_b % t_new - lax.broadcasted_iota(jnp.int32, (rb, LANES), 1)
        sj = lax.broadcasted_iota(jnp.int32, (rb, nsp), 1)
        for k in range(H_KV):
            own = jnp.sum(jnp.where(sj == cur_s, sels_sc[k], 0.0), axis=1, keepdims=True) > 0.5
            s = _dot_t(qb16_ref[0, k], ksn_ref[0]) - slope_b[k] * dist_b.astype(F32)
            _flash_step(s, own & (dist_b >= 0), vsn_ref[0], mb_sc, lb_sc, accb_sc, k)
            os_sc[k] = _flash_out(lb_sc, accb_sc, k)
        _flash_init(mb_sc, lb_sc, accb_sc)
        n_w = kwa_ref.shape[1]
        dist_w = WINDOW + row_b % t_new - lax.broadcasted_iota(jnp.int32, (rb, n_w), 1)
        valid_w = (dist_w >= 0) & (dist_w < WINDOW)
        lane_b = lax.broadcasted_iota(jnp.int32, (rb, LANES), 1)
        comb = []
        for k in range(H_KV):
            s = _dot_t(qb16_ref[0, k], kwa_ref[0]) - slope_b[k] * dist_w.astype(F32)
            _flash_step(s, valid_w, vwa_ref[0], mb_sc, lb_sc, accb_sc, k)
            g = g16_ref[0, k]
            comb.append(g[:, 0:1] * oc_sc[k] + g[:, 1:2] * os_sc[k] + g[:, 2:3] * _flash_out(lb_sc, accb_sc, k))
        ob_ref[0] = jnp.where(lane_b < HEAD_DIM, comb[0], comb[1])


def _attn_sample(p, nsa_t, page_table, kc, vc, state_win, s, t_new, past):
    r_tile = next(r for r in SAMPLE_KV_TILES if past % r == 0)
    g_pages = r_tile // PAGE_SIZE
    assert state_win.shape[1] == WINDOW and LANES % (r_tile // SLC_BLOCK) == 0
    lp = _round_up(past + t_new, MOBA_BLOCK)
    n_s = lp // SLC_BLOCK
    n_cmp = lp // CMP_STRIDE - 1
    nsp = _round_up(n_s, LANES)
    nc = kc.shape[1]
    mm = _cmp_to_block_matrix(nc, nsp, min(n_cmp, nc), n_s)
    lane = jnp.arange(LANES)
    lo = (lane < HEAD_DIM)

    qb = p["qb"].reshape(s, t_new, GROUP, LANES).transpose(0, 2, 1, 3).reshape(s, 1, GROUP * t_new, LANES)
    zero = jnp.zeros_like(qb)
    qb16 = jnp.concatenate([jnp.where(lo, qb, zero), jnp.where(lo, zero, qb)], axis=1)
    g16 = p["gb"][:, :3 * H_B].reshape(s, t_new, H_KV, GROUP, 3).transpose(0, 2, 3, 1, 4)
    g16 = jnp.pad(g16.reshape(s, H_KV, GROUP * t_new, 3), ((0, 0), (0, 0), (0, 0), (0, LANES - 3)))

    pad_new = lambda a: jnp.pad(a.reshape(s, t_new, a.shape[-1]), ((0, 0), (0, LANES - t_new), (0, 0)))
    win = state_win.reshape(s, WINDOW, 2, W_KV).astype(BF16)
    kwa = jnp.concatenate([win[:, :, 0], pad_new(p["kwb"])], axis=1)
    vwa = jnp.concatenate([win[:, :, 1], pad_new(p["vwb"])], axis=1)

    ins = [qb16, g16, kc, vc, mm, pad_new(p["ksb"]), pad_new(p["vsb"]), kwa, vwa]
    seq = lambda a: pl.BlockSpec((1,) + a.shape[1:], lambda i, k, pt: (i,) + (0,) * (a.ndim - 1))
    page = lambda g: pl.BlockSpec((1, 2) + nsa_t.shape[2:], lambda i, k, pt: (pt[i, k * g_pages + g], 1, 0, 0, 0))
    in_specs = ([seq(a) for a in ins[:4]] + [pl.BlockSpec(mm.shape, lambda i, k, pt: (0, 0))]
                + [seq(a) for a in ins[5:]] + [page(g) for g in range(g_pages)])
    rb = GROUP * t_new
    scratch = [
        pltpu.VMEM((H_KV, rb, LANES), F32), pltpu.VMEM((H_KV, rb, LANES), F32), pltpu.VMEM((H_KV, rb, LANES), F32),
        pltpu.VMEM((H_KV, rb, nsp), F32), pltpu.VMEM((H_KV, rb, LANES), F32), pltpu.VMEM((H_KV, rb, LANES), F32),
    ]
    ob16 = pl.pallas_call(
        functools.partial(_attn_sample_kernel, past=past, t_new=t_new, k_s=min(SLC_TOPK, n_s), r_tile=r_tile),
        grid_spec=pltpu.PrefetchScalarGridSpec(
            num_scalar_prefetch=1, grid=(s, past // r_tile), in_specs=in_specs,
            out_specs=pl.BlockSpec((1, rb, LANES), lambda i, k, pt: (i, 0, 0)), scratch_shapes=scratch),
        out_shape=jax.ShapeDtypeStruct((s, rb, LANES), F32),
        compiler_params=_cparams("parallel", "arbitrary"),
    )(page_table, *ins, *([nsa_t] * g_pages))
    return ob16.reshape(s, GROUP, t_new, LANES).transpose(0, 2, 1, 3).reshape(s * t_new, W_B).astype(BF16)


def _post_kernel(oa_ref, ob_ref, gm_ref, x_ref, wba_ref, wbb_ref, wout_ref, gnf_ref, wg_ref, wu_ref, wd_ref,
                 y_ref, h_sc, hn_sc, acc_sc):
    j = pl.program_id(1)

    @pl.when(j == 0)
    def _():
        ma = _dot(oa_ref[...], wba_ref[...])
        mb = _dot(ob_ref[...], wbb_ref[...])
        merged = gm_ref[:, :D_MODEL] * ma + gm_ref[:, D_MODEL:] * mb
        h = x_ref[...] + _dot(merged.astype(BF16), wout_ref[...])
        h_sc[...] = h
        r = lax.rsqrt(jnp.mean(h * h, axis=-1, keepdims=True) + RMS_EPS)
        hn_sc[...] = ((h * r) * gnf_ref[...]).astype(BF16)
        acc_sc[...] = jnp.zeros(acc_sc.shape, F32)

    hn = hn_sc[...]
    g = _dot(hn, wg_ref[...])
    u = _dot(hn, wu_ref[...])
    act = (g * jax.nn.sigmoid(g)) * u
    acc_sc[...] += _dot(act.astype(BF16), wd_ref[...])

    @pl.when(j == pl.num_programs(1) - 1)
    def _():
        y_ref[...] = h_sc[...] + acc_sc[...]


def _post(oa, ob, gm, x, wts):
    n = x.shape[0]
    tm = 512 if n % 512 == 0 else n
    d_ff = wts["wd"].shape[0]
    n_ff = 2 if (d_ff // 2) % LANES == 0 else 1
    fc = d_ff // n_ff
    rows = lambda w: pl.BlockSpec((tm, w), lambda i, j: (i, 0))
    full = lambda a: pl.BlockSpec(a.shape, lambda i, j: (0,) * a.ndim)
    ins = [oa, ob, gm, x, wts["wba"], wts["wbb"], wts["wout"], wts["gnf"], wts["wgate"], wts["wup"], wts["wd"]]
    in_specs = [rows(W_A), rows(W_B), rows(2 * D_MODEL), rows(D_MODEL)] + [full(a) for a in ins[4:8]] + [
        pl.BlockSpec((D_MODEL, fc), lambda i, j: (0, j)),
        pl.BlockSpec((D_MODEL, fc), lambda i, j: (0, j)),
        pl.BlockSpec((fc, D_MODEL), lambda i, j: (j, 0)),
    ]
    return pl.pallas_call(
        _post_kernel, grid=(n // tm, n_ff), in_specs=in_specs, out_specs=rows(D_MODEL),
        out_shape=jax.ShapeDtypeStruct((n, D_MODEL), F32),
        scratch_shapes=[pltpu.VMEM((tm, D_MODEL), F32), pltpu.VMEM((tm, D_MODEL), BF16), pltpu.VMEM((tm, D_MODEL), F32)],
        compiler_params=_cparams("parallel", "arbitrary"),
    )(*ins)


def _blockdiag2(w):
    z = jnp.zeros_like(w)
    return jnp.concatenate([jnp.concatenate([w, z], axis=-1), jnp.concatenate([z, w], axis=-1)], axis=-2)


def _prep_weights(g_na, w_in, b_in, g_qk_a, g_qk_b, cmp_pos, cmp_w1, cmp_w2, w_ba, w_bb, w_out, g_nf, w_up, w_down):
    def perm_qb_cols(a):
        lead = a.shape[:-1]
        return a.reshape(lead + (H_KV, GROUP, HEAD_DIM)).swapaxes(-3, -2).reshape(lead + (W_B,))

    def main_cols(a):
        return jnp.concatenate([a[..., :C_QB], perm_qb_cols(a[..., C_QB:C_KVB]), a[..., C_KVB:C_END]], axis=-1)

    ones = lambda n: jnp.ones((n,), F32)
    gv = jnp.concatenate([
        jnp.tile(g_qk_a[0], H_A), jnp.tile(g_qk_a[1], H_A), ones(W_A), jnp.tile(g_qk_b[0], H_B),
        ones(2 * W_KV), jnp.tile(g_qk_b[2], H_KV), ones(W_KV), jnp.tile(g_qk_b[3], H_KV), ones(W_KV)])
    n_gb = 3 * H_B
    hd_idx = np.arange(W_A) // HEAD_DIM
    bd = jnp.asarray((hd_idx[:, None] == hd_idx[None, :]).astype(np.float32)).astype(BF16)
    d_ff = w_down.shape[0]
    w1 = cmp_w1.astype(BF16)
    pos2 = jnp.concatenate([cmp_pos, cmp_pos], axis=-1)
    return {
        "gna": g_na.reshape(1, D_MODEL),
        "w1": main_cols(w_in).astype(BF16), "b1": main_cols(b_in).reshape(1, C_END), "gv": gv.reshape(1, C_END),
        "wg": jnp.pad(w_in[:, C_END:C_END + n_gb], ((0, 0), (0, LANES - n_gb))).astype(BF16),
        "bg": jnp.pad(b_in[C_END:C_END + n_gb], (0, LANES - n_gb)).reshape(1, LANES),
        "wgm": w_in[:, C_END + n_gb:].astype(BF16), "bgm": b_in[C_END + n_gb:].reshape(1, 2 * D_MODEL),
        "bd": bd,
        "cpos": pos2.reshape(2, 2, CMP_STRIDE, 1, LANES).swapaxes(0, 1),
        "cwa": _blockdiag2(w1[:, :CMP_STRIDE]), "cwb": _blockdiag2(w1[:, CMP_STRIDE:]),
        "cw2": _blockdiag2(cmp_w2.astype(BF16)),
        "gkc": jnp.tile(g_qk_b[1], H_KV).reshape(1, LANES),
        "wba": w_ba.astype(BF16),
        "wbb": w_bb.reshape(H_KV, GROUP, HEAD_DIM, D_MODEL).swapaxes(0, 1).reshape(W_B, D_MODEL).astype(BF16),
        "wout": w_out.astype(BF16), "gnf": g_nf.reshape(1, D_MODEL),
        "wgate": w_up[:, :d_ff].astype(BF16), "wup": w_up[:, d_ff:].astype(BF16), "wd": w_down.astype(BF16),
    }


def _prompt_layer(x, wts):
    b, t, _ = x.shape
    x2 = x.reshape(b * t, D_MODEL)
    p = _inproj(x2, wts, seq_len=t)
    nc = t // CMP_STRIDE
    kc, vc = _compress(p["cmpk"].reshape(b, t, W_KV), p["cmpv"].reshape(b, t, W_KV), wts, tc=nc)
    oa, ob = _attn_prompt(p, kc, vc, b, t)
    y = _post(oa, ob, p["gm"], x2, wts)
    keep = min(WINDOW, t)
    rows_major = lambda a: a.transpose(0, 4, 1, 2, 3)
    return (y.reshape(b, t, D_MODEL), rows_major(p["kva"]), rows_major(p["kvn"]),
            rows_major(p["kvw"][..., t - keep:]))


def _sample_layer(x, cache_moba, cache_nsa, state_win, page_table, wts):
    s, t_new, _ = x.shape
    past = page_table.shape[1] * PAGE_SIZE
    x2 = x.reshape(s * t_new, D_MODEL)
    p = _inproj(x2, wts)
    nsa_t = cache_nsa.transpose(0, 2, 3, 4, 1)
    kc, vc = _compress_paged(page_table, nsa_t, wts)
    oa = _moba_sample(p, cache_moba, page_table, s, t_new)
    ob = _attn_sample(p, nsa_t, page_table, kc, vc, state_win, s, t_new, past)
    y = _post(oa, ob, p["gm"], x2, wts)
    kvw_new = p["kvw"].reshape(s, t_new, 2, H_KV, HEAD_DIM)
    win = jnp.concatenate([state_win, kvw_new], axis=1)[:, t_new:]
    return (y.reshape(s, t_new, D_MODEL),
            p["kva"].reshape(s, t_new, 2, H_A, HEAD_DIM),
            p["kvn"].reshape(s, t_new, 4, H_KV, HEAD_DIM),
            win)


def kernel(x_prompt, x_sample, cache_moba_kv, cache_nsa_kv, state_win_kv, page_table, g_norm_attn, w_in, b_in,
           g_qk_moba, g_qk_nsa, cmp_pos, cmp_w1, cmp_w2, w_br_moba, w_br_nsa, w_out, g_norm_ffn, w_up, w_down):
    assert g_norm_attn.shape[0] == 1, "single-layer step"
    wts = _prep_weights(g_norm_attn[0], w_in[0], b_in[0], g_qk_moba[0], g_qk_nsa[0], cmp_pos[0], cmp_w1[0],
                        cmp_w2[0], w_br_moba[0], w_br_nsa[0], w_out[0], g_norm_ffn[0], w_up[0], w_down[0])
    y_p, a_p, n_p, wn_p = _prompt_layer(x_prompt, wts)
    y_s, a_s, n_s, wn_s = _sample_layer(x_sample, cache_moba_kv[0], cache_nsa_kv[0], state_win_kv[0], page_table, wts)
    return (y_p, y_s, a_p[None], n_p[None], wn_p[None], a_s[None], n_s[None], wn_s[None])
```
